```python
import jax, jax.numpy as jnp
from jax import lax
import numpy as np

D_MODEL = 1024
BATCH = 2
SEQ = 16384
DEPTH = 4

N_BRANCH = 4
HEAD_DIM = 64
BRANCH_WIDTH = D_MODEL // N_BRANCH
N_HEADS = BRANCH_WIDTH // HEAD_DIM
NORM_EPS = 1e-6
RW_DECAY_LORA = 64
RW_AAA_LORA = 64
RW_GATE_LORA = 128
RW_DECAY_SCALE = 0.606531
RW_LN_EPS = 64e-5
SB_BLOCK = 128
RET_CHUNK = 128
RET_GN_EPS = 1e-5
ROPE_BASE = 10000.0
CONV_WIDTH = 31
CONV_LN_EPS = 1e-5
D_FF = 2816
N_EXPERTS = 8
TOP_K = 2
D_FF_EXPERT = 1408
MOE_BLOCK = 256

RW_IN = 3 * BRANCH_WIDTH + RW_DECAY_LORA + RW_AAA_LORA + RW_GATE_LORA
SB_IN = 3 * BRANCH_WIDTH
RET_IN = 4 * BRANCH_WIDTH
CONV_IN = 2 * BRANCH_WIDTH
N_IN = RW_IN + SB_IN + RET_IN + CONV_IN
MIX_SPLITS = [RW_IN, RW_IN + SB_IN, RW_IN + SB_IN + RET_IN]
RW_SPLITS = [BRANCH_WIDTH, 2 * BRANCH_WIDTH, 3 * BRANCH_WIDTH,
             3 * BRANCH_WIDTH + RW_DECAY_LORA, 3 * BRANCH_WIDTH + RW_DECAY_LORA + RW_AAA_LORA]

kernel_name = "hybrid_rwkv7_stickbreak_retnet_conformer_moe"


def _rmsnorm(x, gain, eps=NORM_EPS):
    x32 = x.astype(jnp.float32)
    y = x32 * lax.rsqrt(jnp.mean(x32 * x32, axis=-1, keepdims=True) + eps)
    return (y * gain).astype(x.dtype)


def _layernorm(x, gain, bias, eps):
    x32 = x.astype(jnp.float32)
    xc = x32 - jnp.mean(x32, axis=-1, keepdims=True)
    var = jnp.mean(xc * xc, axis=-1, keepdims=True)
    return (xc * lax.rsqrt(var + eps) * gain + bias).astype(x.dtype)


def _token_shift(x):
    return jnp.pad(x, ((0, 0), (1, 0), (0, 0)))[:, :-1]


def _split_heads(t):
    return t.reshape(t.shape[0], t.shape[1], N_HEADS, HEAD_DIM)


def _rwkv7_mixer(p, mu, w0, w2, a0, a2, g2, k_k, k_a, r_k, ln_w, ln_b):
    B, S, _ = p.shape
    p = p + (_token_shift(p) - p) * mu
    r, k, v, pw, pa, pg = jnp.split(p, RW_SPLITS, axis=-1)
    w = jnp.exp(-RW_DECAY_SCALE * jax.nn.sigmoid((w0 + jnp.tanh(pw) @ w2).astype(jnp.float32)))
    a = jax.nn.sigmoid(a0 + pa @ a2)
    g = jax.nn.sigmoid(pg) @ g2
    r, k, v, w, a = (_split_heads(t) for t in (r, k, v, w, a))
    kk32 = (k * k_k.reshape(N_HEADS, HEAD_DIM)).astype(jnp.float32)
    kk = kk32 / jnp.maximum(jnp.sqrt(jnp.sum(kk32 * kk32, axis=-1, keepdims=True)), 1e-12)
    k = k * (1.0 + (a - 1.0) * k_a.reshape(N_HEADS, HEAD_DIM))

    def step(state, inp):
        r_t, w_t, k_t, v_t, kk_t, a_t = inp
        sa = jnp.einsum('bhvk,bhk->bhv', state, -kk_t)
        state = (state * w_t[:, :, None, :] + sa[..., None] * (kk_t * a_t)[:, :, None, :]
                 + v_t[..., None] * k_t[:, :, None, :])
        return state, jnp.einsum('bhvk,bhk->bhv', state, r_t)

    seq_first = lambda t: jnp.moveaxis(t.astype(jnp.float32), 1, 0)
    state0 = jnp.zeros((B, N_HEADS, HEAD_DIM, HEAD_DIM), jnp.float32)
    _, y = lax.scan(step, state0, tuple(seq_first(t) for t in (r, w, k, v, kk, a)))
    y = jnp.moveaxis(y, 0, 1)
    y = _layernorm(y, ln_w.reshape(N_HEADS, HEAD_DIM), ln_b.reshape(N_HEADS, HEAD_DIM), RW_LN_EPS)
    y = y + jnp.sum(r * k * r_k, axis=-1, keepdims=True) * v
    return (y.reshape(B, S, BRANCH_WIDTH) * g).astype(p.dtype)


def _stick_breaking_mixer(p, q_norm, k_norm):
    B, S, _ = p.shape
    nb = S // SB_BLOCK
    blocks = lambda t: _split_heads(t).reshape(B, nb, SB_BLOCK, N_HEADS, HEAD_DIM).transpose(0, 3, 1, 2, 4)
    q, k, v = (blocks(t) for t in jnp.split(p, 3, axis=-1))
    q = _rmsnorm(q, q_norm) * (HEAD_DIM ** -0.5)
    k = _rmsnorm(k, k_norm)
    idx = jnp.arange(SB_BLOCK)
    rev_incl = (idx[:, None] >= idx[None, :]).astype(jnp.float32)
    diag_mask = idx[None, :] < idx[:, None]
    acc = jnp.zeros((B, N_HEADS, nb, SB_BLOCK), jnp.float32)
    o = jnp.zeros((B, N_HEADS, nb, SB_BLOCK, HEAD_DIM), jnp.float32)
    for d in range(nb):
        n = nb - d
        z = jnp.einsum('bhnqd,bhnkd->bhnqk', q[:, :, d:], k[:, :, :n]).astype(jnp.float32)
        ls = jax.nn.log_sigmoid(-z)
        if d == 0:
            ls = jnp.where(diag_mask, ls, 0.0)
        cum = jnp.einsum('bhnqk,kj->bhnqj', ls, rev_incl)
        log_a = z + cum + acc[:, :, d:, :, None]
        if d == 0:
            log_a = jnp.where(diag_mask, log_a, -jnp.inf)
        attn = jnp.exp(log_a)
        o = o.at[:, :, d:].add(jnp.einsum('bhnqk,bhnkd->bhnqd', attn.astype(v.dtype), v[:, :, :n]).astype(jnp.float32))
        acc = acc.at[:, :, d:].add(cum[..., 0])
    return o.transpose(0, 2, 3, 1, 4).reshape(B, S, BRANCH_WIDTH).astype(p.dtype)


def _rotary(t):
    S = t.shape[1]
    inv_freq = ROPE_BASE ** (-jnp.arange(0, HEAD_DIM, 2, dtype=jnp.float32) / HEAD_DIM)
    ang = jnp.arange(S, dtype=jnp.float32)[:, None] * inv_freq[None, :]
    cos = jnp.cos(ang)[None, :, None, :].astype(t.dtype)
    sin = jnp.sin(ang)[None, :, None, :].astype(t.dtype)
    t1, t2 = jnp.split(t, 2, axis=-1)
    return jnp.concatenate([t1 * cos - t2 * sin, t1 * sin + t2 * cos], axis=-1)


def _retention_mixer(p, gn_w):
    B, S, _ = p.shape
    dt = p.dtype
    q, k, v, gt = jnp.split(p, 4, axis=-1)
    q = _rotary(_split_heads(q))
    k = _rotary(_split_heads(k)) * (HEAD_DIM ** -0.5)
    v = _split_heads(v)
    nc = S // RET_CHUNK
    chunk = lambda t: t.reshape(B, nc, RET_CHUNK, N_HEADS, HEAD_DIM).transpose(0, 3, 1, 2, 4)
    qc, kc, vc = chunk(q), chunk(k), chunk(v)
    log_gamma = jnp.log(1.0 - 2.0 ** (-5.0 - jnp.arange(N_HEADS, dtype=jnp.float32)))
    idx = jnp.arange(RET_CHUNK, dtype=jnp.float32)
    rel = idx[:, None] - idx[None, :]
    intra_decay = jnp.where(rel >= 0, jnp.exp(jnp.maximum(rel, 0.0) * log_gamma[:, None, None]), 0.0)
    k_decay = jnp.exp((RET_CHUNK - 1 - idx)[None, :] * log_gamma[:, None])
    q_decay = jnp.exp((idx + 1.0)[None, :] * log_gamma[:, None])
    chunk_decay = jnp.exp(RET_CHUNK * log_gamma)
    scores = jnp.einsum('bhcqd,bhckd->bhcqk', qc, kc) * intra_decay[None, :, None].astype(dt)
    intra = jnp.einsum('bhcqk,bhckd->bhcqd', scores, vc)
    kv = jnp.einsum('bhckd,bhcke->cbhde', kc * k_decay[None, :, None, :, None].astype(dt), vc)

    def step(state, kv_c):
        return chunk_decay[None, :, None, None] * state + kv_c, state

    _, state_prev = lax.scan(step, jnp.zeros(kv.shape[1:], jnp.float32), kv.astype(jnp.float32))
    inter = jnp.einsum('bhcqd,cbhde->bhcqe', qc * q_decay[None, :, None, :, None].astype(dt),
                       state_prev.astype(dt))
    o = (intra + inter).transpose(0, 2, 3, 1, 4).reshape(B, S, N_HEADS, HEAD_DIM)
    o = _layernorm(o, gn_w.reshape(N_HEADS, HEAD_DIM), 0.0, RET_GN_EPS)
    return o.reshape(B, S, BRANCH_WIDTH) * jax.nn.silu(gt)


def _conformer_conv_mixer(p, dw, db, ln_w, ln_b):
    u_a, u_b = jnp.split(p, 2, axis=-1)
    u = u_a * jax.nn.sigmoid(u_b)
    u = lax.conv_general_dilated(u, dw[:, None, :].astype(u.dtype), window_strides=(1,),
                                 padding=[(CONV_WIDTH - 1, 0)],
                                 dimension_numbers=('NWC', 'WIO', 'NWC'),
                                 feature_group_count=BRANCH_WIDTH) + db
    return jax.nn.silu(_layernorm(u, ln_w, ln_b, CONV_LN_EPS))


def _swiglu(h, w1, w3, w2):
    return (jax.nn.silu(h @ w1) * (h @ w3)) @ w2


def _moe_ffn(h, router, w1, w3, w2):
    B, S, D = h.shape
    T = B * S
    xt = h.reshape(T, D)
    logits = (xt @ router).astype(jnp.float32)
    top_logit, top_idx = lax.top_k(logits, TOP_K)
    gates = jax.nn.softmax(top_logit, axis=-1)
    n_assign = T * TOP_K
    expert = top_idx.reshape(-1)
    token = jnp.arange(n_assign) // TOP_K
    order = jnp.argsort(expert)
    expert_s, token_s, weight_s = expert[order], token[order], gates.reshape(-1)[order]
    x_s = xt[token_s]
    counts = jnp.bincount(expert, length=N_EXPERTS)
    n_blk = n_assign // MOE_BLOCK
    seg_start = jnp.sort(jnp.concatenate([jnp.arange(n_blk) * MOE_BLOCK, jnp.cumsum(counts)[:-1]]))
    seg_end = jnp.concatenate([seg_start[1:], jnp.array([n_assign], seg_start.dtype)])
    seg_blk = jnp.minimum(seg_start // MOE_BLOCK, n_blk - 1)
    seg_exp = expert_s[jnp.minimum(seg_start, n_assign - 1)]

    def segment(args):
        start, end, blk, e = args
        rows = blk * MOE_BLOCK + jnp.arange(MOE_BLOCK)
        xb = lax.dynamic_slice_in_dim(x_s, blk * MOE_BLOCK, MOE_BLOCK, axis=0)
        yb = _swiglu(xb, w1[e], w3[e], w2[e])
        return jnp.where(((rows >= start) & (rows < end))[:, None], yb, 0.0)

    y_seg = lax.map(segment, (seg_start, seg_end, seg_blk, seg_exp))
    y_s = jax.ops.segment_sum(y_seg, seg_blk, num_segments=n_blk).reshape(n_assign, D)
    out = jax.ops.segment_sum(y_s * weight_s[:, None].astype(y_s.dtype), token_s, num_segments=T)
    return out.reshape(B, S, D)


def setup_inputs(seed: int = 0) -> dict:
    key = jax.random.key(seed)
    ks = iter(jax.random.split(key, 40))
    nrm = lambda shape, scale: scale * jax.random.normal(next(ks), shape, jnp.float32)
    L, C = DEPTH, BRANCH_WIDTH
    nd, nm = (DEPTH + 1) // 2, DEPTH // 2
    return {
        "x": nrm((BATCH, SEQ, D_MODEL), 1.0),
        "norm_mix": 1.0 + nrm((L, D_MODEL), 0.1),
        "w_in": nrm((L, D_MODEL, N_IN), D_MODEL ** -0.5),
        "rw_mu": jax.random.uniform(next(ks), (L, RW_IN), jnp.float32),
        "rw_w0": -1.5 + nrm((L, C), 0.5),
        "rw_w2": nrm((L, RW_DECAY_LORA, C), 0.5 * RW_DECAY_LORA ** -0.5),
        "rw_a0": nrm((L, C), 0.5),
        "rw_a2": nrm((L, RW_AAA_LORA, C), RW_AAA_LORA ** -0.5),
        "rw_g2": nrm((L, RW_GATE_LORA, C), RW_GATE_LORA ** -0.5),
        "rw_k_k": 0.85 + nrm((L, C), 0.1),
        "rw_k_a": 1.0 + nrm((L, C), 0.1),
        "rw_r_k": nrm((L, N_HEADS, HEAD_DIM), 0.1),
        "rw_ln_w": 1.0 + nrm((L, C), 0.1),
        "rw_ln_b": nrm((L, C), 0.01),
        "sb_q_norm": 1.0 + nrm((L, HEAD_DIM), 0.1),
        "sb_k_norm": 1.0 + nrm((L, HEAD_DIM), 0.1),
        "ret_gn": 1.0 + nrm((L, C), 0.1),
        "conv_dw": nrm((L, CONV_WIDTH, C), CONV_WIDTH ** -0.5),
        "conv_b": nrm((L, C), 0.01),
        "conv_ln_w": 1.0 + nrm((L, C), 0.1),
        "conv_ln_b": nrm((L, C), 0.01),
        "w_gate": nrm((L, N_BRANCH, D_MODEL, D_MODEL), D_MODEL ** -0.5),
        "w_branch": nrm((L, N_BRANCH, C, D_MODEL), C ** -0.5),
        "w_out": nrm((L, D_MODEL, D_MODEL), 0.5 * D_MODEL ** -0.5),
        "norm_ffn": 1.0 + nrm((L, D_MODEL), 0.1),
        "ffn_w1": nrm((nd, D_MODEL, D_FF), D_MODEL ** -0.5),
        "ffn_w3": nrm((nd, D_MODEL, D_FF), D_MODEL ** -0.5),
        "ffn_w2": nrm((nd, D_FF, D_MODEL), 0.5 * D_FF ** -0.5),
        "router": nrm((nm, D_MODEL, N_EXPERTS), D_MODEL ** -0.5),
        "moe_w1": nrm((nm, N_EXPERTS, D_MODEL, D_FF_EXPERT), D_MODEL ** -0.5),
        "moe_w3": nrm((nm, N_EXPERTS, D_MODEL, D_FF_EXPERT), D_MODEL ** -0.5),
        "moe_w2": nrm((nm, N_EXPERTS, D_FF_EXPERT, D_MODEL), 0.5 * D_FF_EXPERT ** -0.5),
    }


def reference(x, norm_mix, w_in, rw_mu, rw_w0, rw_w2, rw_a0, rw_a2, rw_g2, rw_k_k, rw_k_a, rw_r_k,
              rw_ln_w, rw_ln_b, sb_q_norm, sb_k_norm, ret_gn, conv_dw, conv_b, conv_ln_w, conv_ln_b,
              w_gate, w_branch, w_out, norm_ffn, ffn_w1, ffn_w3, ffn_w2, router, moe_w1, moe_w3, moe_w2):
    for l in range(DEPTH):
        h = _rmsnorm(x, norm_mix[l])
        p_rw, p_sb, p_ret, p_conv = jnp.split(h @ w_in[l], MIX_SPLITS, axis=-1)
        branches = (
            _rwkv7_mixer(p_rw, rw_mu[l], rw_w0[l], rw_w2[l], rw_a0[l], rw_a2[l], rw_g2[l],
                         rw_k_k[l], rw_k_a[l], rw_r_k[l], rw_ln_w[l], rw_ln_b[l]),
            _stick_breaking_mixer(p_sb, sb_q_norm[l], sb_k_norm[l]),
            _retention_mixer(p_ret, ret_gn[l]),
            _conformer_conv_mixer(p_conv, conv_dw[l], conv_b[l], conv_ln_w[l], conv_ln_b[l]),
        )
        merged = jax.nn.sigmoid(h @ w_gate[l, 0]) * (branches[0] @ w_branch[l, 0])
        for i in range(1, N_BRANCH):
            merged = merged + jax.nn.sigmoid(h @ w_gate[l, i]) * (branches[i] @ w_branch[l, i])
        x = x + merged @ w_out[l]
        h = _rmsnorm(x, norm_ffn[l])
        if l % 2 == 0:
            x = x + _swiglu(h, ffn_w1[l // 2], ffn_w3[l // 2], ffn_w2[l // 2])
        else:
            x = x + _moe_ffn(h, router[l // 2], moe_w1[l // 2], moe_w3[l // 2], moe_w2[l // 2])
    return x
```

```python
import functools

import jax
import jax.numpy as jnp
from jax import lax
from jax.experimental import pallas as pl
from jax.experimental.pallas import tpu as pltpu

F32 = jnp.float32
BF16 = jnp.bfloat16

LANES = 128
VMEM_LIMIT = 56 * 1024 * 1024

D_MODEL = 1024
HEAD_DIM = 64
N_HEADS = 4
WIDTH = N_HEADS * HEAD_DIM
NORM_EPS = 1e-6
RW_DECAY_SCALE = 0.606531
RW_LN_EPS = 64e-5
RW_CHUNK = 64
RET_CHUNK = 128
RET_GN_EPS = 1e-5
ROPE_BASE = 10000.0
CONV_WIDTH = 31
CONV_HALO = 32
CONV_LN_EPS = 1e-5
N_EXPERTS = 8
N_IN = 3328
COL_RW = 0
COL_SB_Q, COL_SB_K, COL_SB_V = 8, 10, 12
COL_RET_Q, COL_RET_K, COL_RET_V, COL_RET_G = 14, 16, 18, 20
COL_CONV_A, COL_CONV_B = 11, 12


def _nn(a, b):
    return lax.dot_general(a, b, (((1,), (0,)), ((), ())), preferred_element_type=F32)


def _nt(a, b):
    return lax.dot_general(a, b, (((1,), (1,)), ((), ())), preferred_element_type=F32)


def _mm(a, b):
    return _nn(a.astype(BF16), b.astype(BF16))


def _split(x):
    hi = x.astype(BF16)
    lo = (x - hi.astype(F32)).astype(BF16)
    return hi, lo


def _nn_rhs_exact(x, m):
    hi, lo = _split(x)
    return _nn(hi, m) + _nn(lo, m)


def _nn_lhs_exact(m, x):
    hi, lo = _split(x)
    return _nn(m, hi) + _nn(m, lo)


def _nn_3pass(a, b):
    ah, al = _split(a)
    bh, bl = _split(b)
    return _nn(ah, bh) + (_nn(ah, bl) + _nn(al, bh))


def _iota(shape, axis):
    return lax.broadcasted_iota(jnp.int32, shape, axis)


def _head_block_matrix(n, value):
    same = (_iota((n, n), 0) >> 6) == (_iota((n, n), 1) >> 6)
    return jnp.where(same, value, 0.0).astype(BF16)


def _sigmoid(x):
    return 1.0 / (1.0 + jnp.exp(-x))


def _rmsnorm_rows(x, gain):
    return x * lax.rsqrt(jnp.mean(x * x, axis=-1, keepdims=True) + NORM_EPS) * gain


def _params(*sem):
    return pltpu.CompilerParams(dimension_semantics=sem, vmem_limit_bytes=VMEM_LIMIT)


def _inproj_kernel(x_ref, g_ref, w_ref, p_ref):
    h = _rmsnorm_rows(x_ref[...], g_ref[...])
    p_ref[...] = _nn(h.astype(BF16), w_ref[...])


def _inproj(x2, gain, w_bf16, tm=512):
    T, D = x2.shape
    N = w_bf16.shape[1]
    return pl.pallas_call(
        _inproj_kernel,
        grid=(T // tm,),
        in_specs=[pl.BlockSpec((tm, D), lambda i: (i, 0)),
                  pl.BlockSpec((1, D), lambda i: (0, 0)),
                  pl.BlockSpec((D, N), lambda i: (0, 0))],
        out_specs=pl.BlockSpec((tm, N), lambda i: (i, 0)),
        out_shape=jax.ShapeDtypeStruct((T, N), F32),
        compiler_params=_params("parallel"),
        name="inproj",
    )(x2, gain, w_bf16)


def _rw_kernel(p_ref, mu_ref, w0_ref, a0_ref, wa2_ref, g2_ref, kk_ref, ka_ref, rk_ref, lnw_ref, lnb_ref,
               o_ref, prev_ref, zt_ref, r_s, kk_s, kp_s, v_s, a_s, lw_s, cl_s, y_s, *, ts):
    C = RW_CHUNK
    W = WIDTH
    i = pl.program_id(1)

    @pl.when(i == 0)
    def _():
        prev_ref[...] = jnp.zeros_like(prev_ref)
        zt_ref[...] = jnp.zeros_like(zt_ref)

    p = p_ref[...]
    row = _iota((ts, 1), 0)
    shifted = jnp.where(row == 0, prev_ref[0:1, :], pltpu.roll(p, 1, 0))
    prev_ref[0:1, :] = p_ref[ts - 1:ts, :]
    pm = p + (shifted - p) * mu_ref[...]

    r = pm[:, 0:W]
    k = pm[:, W:2 * W]
    v = pm[:, 2 * W:3 * W]
    lora = pm[:, 3 * W:3 * W + LANES]
    pg = pm[:, 3 * W + LANES:]
    lane = _iota((1, LANES), 1)
    wa = _mm(jnp.where(lane < 64, jnp.tanh(lora), lora), wa2_ref[...])
    lw = -RW_DECAY_SCALE * _sigmoid(w0_ref[...] + wa[:, 0:W])
    a = _sigmoid(a0_ref[...] + wa[:, W:])
    g = _mm(_sigmoid(pg), g2_ref[...])

    ones_bd = _head_block_matrix(W, 1.0)
    kk = k * kk_ref[...]
    kk = kk / jnp.maximum(jnp.sqrt(_nn_rhs_exact(kk * kk, ones_bd)), 1e-12)
    kp = k * (1.0 + (a - 1.0) * ka_ref[...])
    bonus = _nn_rhs_exact(r * kp * rk_ref[...], ones_bd) * v

    r_s[...] = r
    kk_s[...] = kk
    kp_s[...] = kp
    v_s[...] = v
    a_s[...] = a
    lw_s[...] = lw

    r4 = _iota((W, W), 0)
    c4 = _iota((W, W), 1)
    tri4 = jnp.where(((r4 >> 6) == (c4 >> 6)) & (c4 <= r4), 1.0, 0.0).astype(BF16)
    for b in range(ts // W):
        cl_s[b * W:(b + 1) * W, :] = _nn_lhs_exact(tri4, lw[b * W:(b + 1) * W, :])

    head = _iota((1, W), 1) >> 6
    rl = r4 & (C - 1)
    cl_ = c4 & (C - 1)
    strict = rl > cl_
    incl = rl >= cl_
    eye = jnp.where(r4 == c4, 1.0, 0.0)

    def stack4(x):
        return jnp.concatenate([jnp.where(head == h, x, 0.0) for h in range(N_HEADS)], axis=0)

    def chunk(c, carry):
        rows = pl.ds(pl.multiple_of(c * C, C), C)
        cl = cl_s[rows, :]
        ecl = jnp.exp(cl)
        eml = jnp.exp(-cl)
        eprev = jnp.exp(cl - lw_s[rows, :])
        kkc = kk_s[rows, :]
        a4 = stack4(-kkc * eprev).astype(BF16)
        b4 = stack4(kkc * a_s[rows, :] * eml).astype(BF16)
        k4 = stack4(kp_s[rows, :] * eml).astype(BF16)
        r4_ = stack4(r_s[rows, :] * ecl).astype(BF16)
        v4 = stack4(v_s[rows, :])
        v4b = v4.astype(BF16)
        g_end = ecl[C - 1:C, :]

        lab = jnp.where(strict, _nt(a4, b4), 0.0)
        lak = jnp.where(strict, _nt(a4, k4), 0.0)
        mrb = jnp.where(incl, _nt(r4_, b4), 0.0)
        mrk = jnp.where(incl, _nt(r4_, k4), 0.0)
        tinv = eye + lab
        m = lab
        for _ in range(5):
            m = _mm(m, m)
            tinv = tinv + _mm(tinv, m)

        zt = zt_ref[...]
        ztb = zt.astype(BF16)
        u4 = _mm(tinv, _nt(a4, ztb) + _mm(lak, v4b))
        y4 = _nt(r4_, ztb) + _mm(mrb, u4) + _mm(mrk, v4b)
        y_s[rows, :] = (y4[0:C] + y4[C:2 * C]) + (y4[2 * C:3 * C] + y4[3 * C:4 * C])
        uvt = jnp.concatenate([u4, v4], axis=0).T.astype(BF16)
        bk = jnp.concatenate([b4, k4], axis=0)
        zt_ref[...] = (zt + _nn(uvt, bk)) * g_end
        return carry

    lax.fori_loop(0, ts // C, chunk, 0)

    mean_bd = _head_block_matrix(W, 1.0 / HEAD_DIM)
    y = y_s[...]
    yc = y - _nn_rhs_exact(y, mean_bd)
    var = _nn_rhs_exact(yc * yc, mean_bd)
    y = yc * lax.rsqrt(var + RW_LN_EPS) * lnw_ref[...] + lnb_ref[...]
    o_ref[...] = (y + bonus) * g


def _rwkv(p, B, S, mu, w0, a0, wa2, g2, k_k, k_a, r_k, ln_w, ln_b, ts=512):
    ts = min(ts, S)
    nb = S // ts
    T = B * S
    W = WIDTH
    vec = lambda n: pl.BlockSpec((1, n), lambda b, i: (0, 0))
    full = lambda a: pl.BlockSpec(a.shape, lambda b, i: (0, 0))
    return pl.pallas_call(
        functools.partial(_rw_kernel, ts=ts),
        grid=(B, nb),
        in_specs=[pl.BlockSpec((ts, 4 * W), lambda b, i: (b * nb + i, COL_RW)),
                  vec(4 * W), vec(W), vec(W), full(wa2), full(g2), vec(W), vec(W), vec(W), vec(W), vec(W)],
        out_specs=pl.BlockSpec((ts, W), lambda b, i: (b * nb + i, 0)),
        out_shape=jax.ShapeDtypeStruct((T, W), F32),
        scratch_shapes=[pltpu.VMEM((8, 4 * W), F32), pltpu.VMEM((W, W), F32)]
                       + [pltpu.VMEM((ts, W), F32)] * 8,
        compiler_params=_params("arbitrary", "arbitrary"),
        name="rwkv7",
    )(p, mu, w0, a0, wa2, g2, k_k, k_a, r_k, ln_w, ln_b)


def _sb_kernel(q_ref, k_ref, v_ref, qg_ref, kg_ref, o_ref, kb, vb, acc_o, acc_l, *, tq, tk, seq, prep):
    i = pl.program_id(2)
    lane = _iota((1, LANES), 1)
    mean_bd = _head_block_matrix(LANES, 1.0 / HEAD_DIM)

    @pl.when(i == 0)
    def _():
        def body(c, carry):
            rows = pl.ds(pl.multiple_of(c * prep, prep), prep)
            kf = k_ref[rows, :]
            ms = _nn_rhs_exact(kf * kf, mean_bd)
            kb[rows, :] = (kf * lax.rsqrt(ms + NORM_EPS) * kg_ref[...]).astype(BF16)
            vb[rows, :] = v_ref[rows, :].astype(BF16)
            return carry
        lax.fori_loop(0, seq // prep, body, 0)

    q = q_ref[...]
    ms = _nn_rhs_exact(q * q, mean_bd)
    qn = q * lax.rsqrt(ms + NORM_EPS) * qg_ref[...] * (HEAD_DIM ** -0.5)
    q2 = jnp.concatenate([jnp.where(lane < 64, qn, 0.0), jnp.where(lane >= 64, qn, 0.0)], axis=0).astype(BF16)

    acc_o[...] = jnp.zeros_like(acc_o)
    acc_l[...] = jnp.zeros_like(acc_l)
    rev_incl = jnp.where(_iota((tk, tk), 0) >= _iota((tk, tk), 1), 1.0, 0.0).astype(BF16)
    qpos = i * tq + (_iota((2 * tq, tk), 0) & (tq - 1))

    def step(j, masked):
        rows = pl.ds(pl.multiple_of(j * tk, tk), tk)
        z = _nt(q2, kb[rows, :])
        ls = -(jnp.maximum(z, 0.0) + jnp.log(1.0 + jnp.exp(-jnp.abs(z))))
        if masked:
            valid = (j * tk + _iota((2 * tq, tk), 1)) < qpos
            ls = jnp.where(valid, ls, 0.0)
        cum = _nn_rhs_exact(ls, rev_incl)
        attn = jnp.exp(z + cum + acc_l[...])
        if masked:
            attn = jnp.where(valid, attn, 0.0)
        acc_o[...] += _nn(attn.astype(BF16), vb[rows, :])
        acc_l[...] += cum[:, 0:1]

    nd = tq // tk
    for d in range(nd):
        step(i * nd + (nd - 1 - d), True)

    def body(t, carry):
        step(i * nd - 1 - t, False)
        return carry
    lax.fori_loop(0, i * nd, body, 0)

    o_ref[...] = jnp.where(lane < 64, acc_o[0:tq, :], acc_o[tq:2 * tq, :])


def _stick_breaking(p, B, S, qg, kg, tq=256, tk=128):
    tq = min(tq, S)
    nq = S // tq
    T = B * S
    prep = min(512, S)
    return pl.pallas_call(
        functools.partial(_sb_kernel, tq=tq, tk=tk, seq=S, prep=prep),
        grid=(B, 2, nq),
        in_specs=[pl.BlockSpec((tq, LANES), lambda b, h, i: (b * nq + i, COL_SB_Q + h)),
                  pl.BlockSpec((S, LANES), lambda b, h, i: (b, COL_SB_K + h)),
                  pl.BlockSpec((S, LANES), lambda b, h, i: (b, COL_SB_V + h)),
                  pl.BlockSpec((1, LANES), lambda b, h, i: (0, 0)),
                  pl.BlockSpec((1, LANES), lambda b, h, i: (0, 0))],
        out_specs=pl.BlockSpec((tq, LANES), lambda b, h, i: (b * nq + i, h)),
        out_shape=jax.ShapeDtypeStruct((T, WIDTH), F32),
        scratch_shapes=[pltpu.VMEM((S, LANES), BF16), pltpu.VMEM((S, LANES), BF16),
                        pltpu.VMEM((2 * tq, LANES), F32), pltpu.VMEM((2 * tq, LANES), F32)],
        compiler_params=_params("arbitrary", "arbitrary", "arbitrary"),
        name="stick_breaking",
    )(p, p, p, qg, kg)


def _ret_kernel(q_ref, k_ref, v_ref, gt_ref, cos_ref, sin_ref, gn_ref, dm_ref, kdec_ref, qdec_ref, cdec_ref,
                o_ref, st_ref, *, ts):
    C = RET_CHUNK
    i = pl.program_id(2)

    @pl.when(i == 0)
    def _():
        st_ref[...] = jnp.zeros_like(st_ref)

    lane = _iota((1, LANES), 1)
    first_half = (lane & 63) < 32
    same_head = (_iota((LANES, LANES), 0) >> 6) == (_iota((LANES, LANES), 1) >> 6)
    mean_bd = _head_block_matrix(LANES, 1.0 / HEAD_DIM)

    def rotary(t, cos, sin_signed):
        swapped = jnp.where(first_half, pltpu.roll(t, LANES - 32, 1), pltpu.roll(t, 32, 1))
        return t * cos + swapped * sin_signed

    for c in range(ts // C):
        rows = slice(c * C, (c + 1) * C)
        cos = cos_ref[rows, :]
        sin = sin_ref[rows, :]
        q = rotary(q_ref[rows, :], cos, sin)
        k = rotary(k_ref[rows, :], cos, sin) * (HEAD_DIM ** -0.5)
        vb = v_ref[rows, :].astype(BF16)
        kb = k.astype(BF16)
        q2 = jnp.concatenate([jnp.where(lane < 64, q, 0.0), jnp.where(lane >= 64, q, 0.0)], axis=0).astype(BF16)
        scores = _nt(q2, kb) * dm_ref[0]
        intra2 = _nn(scores.astype(BF16), vb)
        intra = jnp.where(lane < 64, intra2[0:C], intra2[C:2 * C])
        st = st_ref[...]
        inter = _mm(q * qdec_ref[0], st)
        kv = _nn((k * kdec_ref[0]).T.astype(BF16), vb)
        st_ref[...] = cdec_ref[0] * st + jnp.where(same_head, kv, 0.0)
        o = intra + inter
        oc = o - _nn_rhs_exact(o, mean_bd)
        var = _nn_rhs_exact(oc * oc, mean_bd)
        on = oc * lax.rsqrt(var + RET_GN_EPS) * gn_ref[...]
        gt = gt_ref[rows, :]
        o_ref[rows, :] = on * (gt * _sigmoid(gt))


def _retention_tables(S):
    C = RET_CHUNK
    inv_freq = ROPE_BASE ** (-jnp.arange(0, HEAD_DIM, 2, dtype=F32) / HEAD_DIM)
    ang = jnp.arange(S, dtype=F32)[:, None] * inv_freq[None, :]
    cos = jnp.tile(jnp.cos(ang), (1, 4))
    sin = jnp.tile(jnp.concatenate([-jnp.sin(ang), jnp.sin(ang)], axis=1), (1, 2))
    log_gamma = jnp.log(1.0 - 2.0 ** (-5.0 - jnp.arange(N_HEADS, dtype=F32)))
    idx = jnp.arange(C, dtype=F32)
    rel = idx[:, None] - idx[None, :]
    intra = jnp.where(rel >= 0, jnp.exp(jnp.maximum(rel, 0.0) * log_gamma[:, None, None]), 0.0)
    dm = intra.reshape(2, 2 * C, C)
    lanes = lambda t: jnp.repeat(t.reshape(2, 2, C).transpose(0, 2, 1), HEAD_DIM, axis=2)
    kdec = lanes(jnp.exp((C - 1 - idx)[None, :] * log_gamma[:, None]))
    qdec = lanes(jnp.exp((idx + 1.0)[None, :] * log_gamma[:, None]))
    cdec = jnp.repeat(jnp.exp(C * log_gamma).reshape(2, 1, 2), HEAD_DIM, axis=2)
    return cos, sin, dm, kdec, qdec, cdec


def _retention(p, B, S, gn, tables, ts=512):
    ts = min(ts, S)
    nb = S // ts
    T = B * S
    C = RET_CHUNK
    cos, sin, dm, kdec, qdec, cdec = tables
    col = lambda c0: pl.BlockSpec((ts, LANES), lambda b, h, i: (b * nb + i, c0 + h))
    tab = pl.BlockSpec((ts, LANES), lambda b, h, i: (i, 0))
    return pl.pallas_call(
        functools.partial(_ret_kernel, ts=ts),
        grid=(B, 2, nb),
        in_specs=[col(COL_RET_Q), col(COL_RET_K), col(COL_RET_V), col(COL_RET_G), tab, tab,
                  pl.BlockSpec((1, LANES), lambda b, h, i: (0, h)),
                  pl.BlockSpec((1, 2 * C, C), lambda b, h, i: (h, 0, 0)),
                  pl.BlockSpec((1, C, LANES), lambda b, h, i: (h, 0, 0)),
                  pl.BlockSpec((1, C, LANES), lambda b, h, i: (h, 0, 0)),
                  pl.BlockSpec((1, 1, LANES), lambda b, h, i: (h, 0, 0))],
        out_specs=pl.BlockSpec((ts, LANES), lambda b, h, i: (b * nb + i, h)),
        out_shape=jax.ShapeDtypeStruct((T, WIDTH), F32),
        scratch_shapes=[pltpu.VMEM((LANES, LANES), F32)],
        compiler_params=_params("arbitrary", "arbitrary", "arbitrary"),
        name="retention",
    )(p, p, p, p, cos, sin, gn, dm, kdec, qdec, cdec)


def _conv_kernel(ua_ref, ub_ref, dw_ref, db_ref, lnw_ref, lnb_ref, o_ref, buf, *, ts):
    i = pl.program_id(1)

    @pl.when(i == 0)
    def _():
        buf[0:CONV_HALO, :] = jnp.zeros((CONV_HALO, WIDTH), F32)

    buf[CONV_HALO:, :] = ua_ref[...] * _sigmoid(ub_ref[...])
    acc = jnp.zeros((ts, WIDTH), F32) + db_ref[...]
    for j in range(CONV_WIDTH):
        start = CONV_HALO - (CONV_WIDTH - 1) + j
        acc = acc + dw_ref[j:j + 1, :] * buf[start:start + ts, :]
    buf[0:CONV_HALO, :] = buf[ts:ts + CONV_HALO, :]
    xc = acc - jnp.mean(acc, axis=-1, keepdims=True)
    var = jnp.mean(xc * xc, axis=-1, keepdims=True)
    y = xc * lax.rsqrt(var + CONV_LN_EPS) * lnw_ref[...] + lnb_ref[...]
    o_ref[...] = y * _sigmoid(y)


def _conformer_conv(p, B, S, dw, db, ln_w, ln_b, ts=512):
    ts = min(ts, S)
    nb = S // ts
    T = B * S
    vec = pl.BlockSpec((1, WIDTH), lambda b, i: (0, 0))
    return pl.pallas_call(
        functools.partial(_conv_kernel, ts=ts),
        grid=(B, nb),
        in_specs=[pl.BlockSpec((ts, WIDTH), lambda b, i: (b * nb + i, COL_CONV_A)),
                  pl.BlockSpec((ts, WIDTH), lambda b, i: (b * nb + i, COL_CONV_B)),
                  pl.BlockSpec((CONV_HALO, WIDTH), lambda b, i: (0, 0)), vec, vec, vec],
        out_specs=pl.BlockSpec((ts, WIDTH), lambda b, i: (b * nb + i, 0)),
        out_shape=jax.ShapeDtypeStruct((T, WIDTH), F32),
        scratch_shapes=[pltpu.VMEM((ts + CONV_HALO, WIDTH), F32)],
        compiler_params=_params("arbitrary", "arbitrary"),
        name="conformer_conv",
    )(p, p, dw, db, ln_w, ln_b)


def _merge_kernel(x_ref, g_ref, y0_ref, y1_ref, y2_ref, y3_ref, wg_ref, wb_ref, wo_ref, o_ref):
    x = x_ref[...]
    h = _rmsnorm_rows(x, g_ref[...]).astype(BF16)
    merged = None
    for n, y_ref in enumerate((y0_ref, y1_ref, y2_ref, y3_ref)):
        term = _sigmoid(_nn(h, wg_ref[n])) * _nn(y_ref[...].astype(BF16), wb_ref[n])
        merged = term if merged is None else merged + term
    o_ref[...] = x + _nn(merged.astype(BF16), wo_ref[...])


def _merge(x2, gain, ys, wg, wb, wo, tm=512):
    T, D = x2.shape
    tm = min(tm, T)
    row = lambda n: pl.BlockSpec((tm, n), lambda i: (i, 0))
    return pl.pallas_call(
        _merge_kernel,
        grid=(T // tm,),
        in_specs=[row(D), pl.BlockSpec((1, D), lambda i: (0, 0)), row(WIDTH), row(WIDTH), row(WIDTH), row(WIDTH),
                  pl.BlockSpec(wg.shape, lambda i: (0, 0, 0)),
                  pl.BlockSpec(wb.shape, lambda i: (0, 0, 0)),
                  pl.BlockSpec(wo.shape, lambda i: (0, 0))],
        out_specs=row(D),
        out_shape=jax.ShapeDtypeStruct((T, D), F32),
        compiler_params=_params("parallel"),
        name="merge",
    )(x2, gain, *ys, wg, wb, wo)


def _ffn_kernel(x_ref, g_ref, w1_ref, w3_ref, w2_ref, o_ref, h_s):
    f = pl.program_id(1)

    @pl.when(f == 0)
    def _():
        x = x_ref[...]
        h_s[...] = _rmsnorm_rows(x, g_ref[...]).astype(BF16)
        o_ref[...] = x

    h = h_s[...]
    a = _nn(h, w1_ref[...])
    b = _nn(h, w3_ref[...])
    o_ref[...] += _nn((a * _sigmoid(a) * b).astype(BF16), w2_ref[...])


def _ffn(x2, gain, w1, w3, w2, tm=1024, tf=1408):
    T, D = x2.shape
    tm = min(tm, T)
    nf = w1.shape[1] // tf
    return pl.pallas_call(
        _ffn_kernel,
        grid=(T // tm, nf),
        in_specs=[pl.BlockSpec((tm, D), lambda i, f: (i, 0)),
                  pl.BlockSpec((1, D), lambda i, f: (0, 0)),
                  pl.BlockSpec((D, tf), lambda i, f: (0, f)),
                  pl.BlockSpec((D, tf), lambda i, f: (0, f)),
                  pl.BlockSpec((tf, D), lambda i, f: (f, 0))],
        out_specs=pl.BlockSpec((tm, D), lambda i, f: (i, 0)),
        out_shape=jax.ShapeDtypeStruct((T, D), F32),
        scratch_shapes=[pltpu.VMEM((tm, D), BF16)],
        compiler_params=_params("parallel", "arbitrary"),
        name="ffn",
    )(x2, gain, w1, w3, w2)


def _moe_kernel(x_ref, g_ref, rt_ref, w1_ref, w3_ref, w2_ref, o_ref, h_s, gate_s):
    e = pl.program_id(1)
    lane = _iota((1, LANES), 1)

    @pl.when(e == 0)
    def _():
        x = x_ref[...]
        h = _rmsnorm_rows(x, g_ref[...])
        h_s[...] = h.astype(BF16)
        o_ref[...] = x
        logits = jnp.where(lane < N_EXPERTS, _nn_3pass(h, rt_ref[...]), -jnp.inf)
        m1 = jnp.max(logits, axis=-1, keepdims=True)
        i1 = jnp.min(jnp.where(logits == m1, lane, LANES), axis=-1, keepdims=True)
        rest = jnp.where(lane == i1, -jnp.inf, logits)
        m2 = jnp.max(rest, axis=-1, keepdims=True)
        i2 = jnp.min(jnp.where(rest == m2, lane, LANES), axis=-1, keepdims=True)
        e2 = jnp.exp(m2 - m1)
        den = 1.0 + e2
        gate_s[...] = jnp.where(lane == i1, 1.0 / den, 0.0) + jnp.where(lane == i2, e2 / den, 0.0)

    h = h_s[...]
    a = _nn(h, w1_ref[0])
    b = _nn(h, w3_ref[0])
    y = _nn((a * _sigmoid(a) * b).astype(BF16), w2_ref[0])
    gate = jnp.sum(jnp.where(lane == e, gate_s[...], 0.0), axis=-1, keepdims=True)
    o_ref[...] += gate * y


def _moe(x2, gain, router_pad, w1, w3, w2, tm=1024):
    T, D = x2.shape
    tm = min(tm, T)
    E, _, F = w1.shape
    return pl.pallas_call(
        _moe_kernel,
        grid=(T // tm, E),
        in_specs=[pl.BlockSpec((tm, D), lambda i, e: (i, 0)),
                  pl.BlockSpec((1, D), lambda i, e: (0, 0)),
                  pl.BlockSpec((D, LANES), lambda i, e: (0, 0)),
                  pl.BlockSpec((1, D, F), lambda i, e: (e, 0, 0)),
                  pl.BlockSpec((1, D, F), lambda i, e: (e, 0, 0)),
                  pl.BlockSpec((1, F, D), lambda i, e: (e, 0, 0))],
        out_specs=pl.BlockSpec((tm, D), lambda i, e: (i, 0)),
        out_shape=jax.ShapeDtypeStruct((T, D), F32),
        scratch_shapes=[pltpu.VMEM((tm, D), BF16), pltpu.VMEM((tm, LANES), F32)],
        compiler_params=_params("parallel", "arbitrary"),
        name="moe",
    )(x2, gain, router_pad, w1, w3, w2)


def kernel(x, norm_mix, w_in, rw_mu, rw_w0, rw_w2, rw_a0, rw_a2, rw_g2, rw_k_k, rw_k_a, rw_r_k, rw_ln_w, rw_ln_b, sb_q_norm, sb_k_norm, ret_gn, conv_dw, conv_b, conv_ln_w, conv_ln_b, w_gate, w_branch, w_out, norm_ffn, ffn_w1, ffn_w3, ffn_w2, router, moe_w1, moe_w3, moe_w2):
    B, S, D = x.shape
    depth = norm_mix.shape[0]
    x2 = x.reshape(B * S, D)
    tables = _retention_tables(S)
    row = lambda t: t.reshape(1, -1)
    for l in range(depth):
        p = _inproj(x2, row(norm_mix[l]), w_in[l].astype(BF16))
        zeros = jnp.zeros_like(rw_w2[l])
        wa2 = jnp.concatenate([jnp.concatenate([rw_w2[l], zeros], axis=1),
                               jnp.concatenate([zeros, rw_a2[l]], axis=1)], axis=0).astype(BF16)
        y_rw = _rwkv(p, B, S, row(rw_mu[l]), row(rw_w0[l]), row(rw_a0[l]), wa2, rw_g2[l].astype(BF16),
                     row(rw_k_k[l]), row(rw_k_a[l]), row(rw_r_k[l]), row(rw_ln_w[l]), row(rw_ln_b[l]))
        y_sb = _stick_breaking(p, B, S, row(jnp.tile(sb_q_norm[l], 2)), row(jnp.tile(sb_k_norm[l], 2)))
        y_ret = _retention(p, B, S, row(ret_gn[l]), tables)
        dw = jnp.concatenate([conv_dw[l], jnp.zeros((CONV_HALO - CONV_WIDTH, WIDTH), F32)], axis=0)
        y_conv = _conformer_conv(p, B, S, dw, row(conv_b[l]), row(conv_ln_w[l]), row(conv_ln_b[l]))
        x2 = _merge(x2, row(norm_mix[l]), (y_rw, y_sb, y_ret, y_conv),
                    w_gate[l].astype(BF16), w_branch[l].astype(BF16), w_out[l].astype(BF16))
        if l % 2 == 0:
            x2 = _ffn(x2, row(norm_ffn[l]), ffn_w1[l // 2].astype(BF16), ffn_w3[l // 2].astype(BF16),
                      ffn_w2[l // 2].astype(BF16))
        else:
            rt = jnp.concatenate([router[l // 2], jnp.zeros((D, LANES - N_EXPERTS), F32)], axis=1)
            x2 = _moe(x2, row(norm_ffn[l]), rt, moe_w1[l // 2].astype(BF16), moe_w3[l // 2].astype(BF16),
                      moe_w2[l // 2].astype(BF16))
    return x2.reshape(B, S, D)
```

```python
import functools

import jax
import jax.numpy as jnp
from jax import lax
from jax.experimental import pallas as pl
from jax.experimental.pallas import tpu as pltpu

F32 = jnp.float32
BF16 = jnp.bfloat16

LANES = 128
VMEM_LIMIT = 56 * 1024 * 1024

D_MODEL = 1024
HEAD_DIM = 64
N_HEADS = 4
WIDTH = N_HEADS * HEAD_DIM
NORM_EPS = 1e-6
LOG2_E = 1.4426950408889634
RW_DECAY_SCALE = 0.606531
RW_LN_EPS = 64e-5
RW_CHUNK = 64
RET_CHUNK = 128
RET_GN_EPS = 1e-5
ROPE_BASE = 10000.0
CONV_WIDTH = 31
CONV_HALO = 32
CONV_LN_EPS = 1e-5
N_EXPERTS = 8
N_IN = 3328
COL_RW = 0
COL_SB_Q, COL_SB_K, COL_SB_V = 8, 10, 12
COL_RET_Q, COL_RET_K, COL_RET_V, COL_RET_G = 14, 16, 18, 20
COL_CONV_A, COL_CONV_B = 11, 12


def _nn(a, b):
    return lax.dot_general(a, b, (((1,), (0,)), ((), ())), preferred_element_type=F32)


def _nt(a, b):
    return lax.dot_general(a, b, (((1,), (1,)), ((), ())), preferred_element_type=F32)


def _mm(a, b):
    return _nn(a.astype(BF16), b.astype(BF16))


def _split(x):
    hi = x.astype(BF16)
    lo = (x - hi.astype(F32)).astype(BF16)
    return hi, lo


def _nn_rhs_exact(x, m):
    hi, lo = _split(x)
    return _nn(hi, m) + _nn(lo, m)


def _nn_lhs_exact(m, x):
    hi, lo = _split(x)
    return _nn(m, hi) + _nn(m, lo)


def _nn_3pass(a, b):
    ah, al = _split(a)
    bh, bl = _split(b)
    return _nn(ah, bh) + (_nn(ah, bl) + _nn(al, bh))


def _iota(shape, axis):
    return lax.broadcasted_iota(jnp.int32, shape, axis)


def _head_block_matrix(n, value):
    same = (_iota((n, n), 0) >> 6) == (_iota((n, n), 1) >> 6)
    return jnp.where(same, value, 0.0).astype(BF16)


def _sigmoid(x):
    return 1.0 / (1.0 + jnp.exp(-x))


def _rmsnorm_rows(x, gain):
    return x * lax.rsqrt(jnp.mean(x * x, axis=-1, keepdims=True) + NORM_EPS) * gain


def _params(*sem):
    return pltpu.CompilerParams(dimension_semantics=sem, vmem_limit_bytes=VMEM_LIMIT)


def _inproj_kernel(x_ref, g_ref, w_ref, p_ref):
    h = _rmsnorm_rows(x_ref[...], g_ref[...])
    p_ref[...] = _nn(h.astype(BF16), w_ref[...])


def _inproj(x2, gain, w_bf16, tm=512):
    T, D = x2.shape
    N = w_bf16.shape[1]
    return pl.pallas_call(
        _inproj_kernel,
        grid=(T // tm,),
        in_specs=[pl.BlockSpec((tm, D), lambda i: (i, 0)),
                  pl.BlockSpec((1, D), lambda i: (0, 0)),
                  pl.BlockSpec((D, N), lambda i: (0, 0))],
        out_specs=pl.BlockSpec((tm, N), lambda i: (i, 0)),
        out_shape=jax.ShapeDtypeStruct((T, N), F32),
        compiler_params=_params("parallel"),
        name="inproj",
    )(x2, gain, w_bf16)


def _rw_kernel(p_ref, mu_ref, w0_ref, a0_ref, wa2_ref, g2_ref, kk_ref, ka_ref, rk_ref, lnw_ref, lnb_ref,
               o_ref, prev_ref, zt_ref, r_s, kk_s, kp_s, v_s, a_s, lw_s, cl_s, y_s, *, ts):
    C = RW_CHUNK
    W = WIDTH
    i = pl.program_id(1)

    @pl.when(i == 0)
    def _():
        prev_ref[...] = jnp.zeros_like(prev_ref)
        zt_ref[...] = jnp.zeros_like(zt_ref)

    p = p_ref[...]
    row = _iota((ts, 1), 0)
    shifted = jnp.where(row == 0, prev_ref[0:1, :], pltpu.roll(p, 1, 0))
    prev_ref[0:1, :] = p_ref[ts - 1:ts, :]
    pm = p + (shifted - p) * mu_ref[...]

    r = pm[:, 0:W]
    k = pm[:, W:2 * W]
    v = pm[:, 2 * W:3 * W]
    lora = pm[:, 3 * W:3 * W + LANES]
    pg = pm[:, 3 * W + LANES:]
    lane = _iota((1, LANES), 1)
    wa = _mm(jnp.where(lane < 64, jnp.tanh(lora), lora), wa2_ref[...])
    lw = -RW_DECAY_SCALE * _sigmoid(w0_ref[...] + wa[:, 0:W])
    a = _sigmoid(a0_ref[...] + wa[:, W:])
    g = _mm(_sigmoid(pg), g2_ref[...])

    ones_bd = _head_block_matrix(W, 1.0)
    kk = k * kk_ref[...]
    kk = kk / jnp.maximum(jnp.sqrt(_nn_rhs_exact(kk * kk, ones_bd)), 1e-12)
    kp = k * (1.0 + (a - 1.0) * ka_ref[...])
    bonus = _nn_rhs_exact(r * kp * rk_ref[...], ones_bd) * v

    r_s[...] = r
    kk_s[...] = kk
    kp_s[...] = kp
    v_s[...] = v
    a_s[...] = a
    lw_s[...] = lw

    r4 = _iota((W, W), 0)
    c4 = _iota((W, W), 1)
    tri4 = jnp.where(((r4 >> 6) == (c4 >> 6)) & (c4 <= r4), 1.0, 0.0).astype(BF16)
    for b in range(ts // W):
        cl_s[b * W:(b + 1) * W, :] = _nn_lhs_exact(tri4, lw[b * W:(b + 1) * W, :])

    head = _iota((1, W), 1) >> 6
    rl = r4 & (C - 1)
    cl_ = c4 & (C - 1)
    strict = rl > cl_
    incl = rl >= cl_
    eye = jnp.where(r4 == c4, 1.0, 0.0)

    def stack4(x):
        return jnp.concatenate([jnp.where(head == h, x, 0.0) for h in range(N_HEADS)], axis=0)

    def chunk(c, carry):
        rows = pl.ds(pl.multiple_of(c * C, C), C)
        cl = cl_s[rows, :]
        ecl = jnp.exp(cl)
        eml = jnp.exp(-cl)
        eprev = jnp.exp(cl - lw_s[rows, :])
        kkc = kk_s[rows, :]
        a4 = stack4(-kkc * eprev).astype(BF16)
        b4 = stack4(kkc * a_s[rows, :] * eml).astype(BF16)
        k4 = stack4(kp_s[rows, :] * eml).astype(BF16)
        r4_ = stack4(r_s[rows, :] * ecl).astype(BF16)
        v4 = stack4(v_s[rows, :])
        v4b = v4.astype(BF16)
        g_end = ecl[C - 1:C, :]

        lab = jnp.where(strict, _nt(a4, b4), 0.0)
        lak = jnp.where(strict, _nt(a4, k4), 0.0)
        mrb = jnp.where(incl, _nt(r4_, b4), 0.0)
        mrk = jnp.where(incl, _nt(r4_, k4), 0.0)
        tinv = eye + lab
        m = lab
        for _ in range(5):
            m = _mm(m, m)
            tinv = tinv + _mm(tinv, m)

        zt = zt_ref[...]
        ztb = zt.astype(BF16)
        u4 = _mm(tinv, _nt(a4, ztb) + _mm(lak, v4b))
        y4 = _nt(r4_, ztb) + _mm(mrb, u4) + _mm(mrk, v4b)
        y_s[rows, :] = (y4[0:C] + y4[C:2 * C]) + (y4[2 * C:3 * C] + y4[3 * C:4 * C])
        uvt = jnp.concatenate([u4, v4], axis=0).T.astype(BF16)
        bk = jnp.concatenate([b4, k4], axis=0)
        zt_ref[...] = (zt + _nn(uvt, bk)) * g_end
        return carry

    lax.fori_loop(0, ts // C, chunk, 0)

    mean_bd = _head_block_matrix(W, 1.0 / HEAD_DIM)
    y = y_s[...]
    yc = y - _nn_rhs_exact(y, mean_bd)
    var = _nn_rhs_exact(yc * yc, mean_bd)
    y = yc * lax.rsqrt(var + RW_LN_EPS) * lnw_ref[...] + lnb_ref[...]
    o_ref[...] = (y + bonus) * g


def _rwkv(p, B, S, mu, w0, a0, wa2, g2, k_k, k_a, r_k, ln_w, ln_b, ts=512):
    ts = min(ts, S)
    nb = S // ts
    T = B * S
    W = WIDTH
    vec = lambda n: pl.BlockSpec((1, n), lambda b, i: (0, 0))
    full = lambda a: pl.BlockSpec(a.shape, lambda b, i: (0, 0))
    return pl.pallas_call(
        functools.partial(_rw_kernel, ts=ts),
        grid=(B, nb),
        in_specs=[pl.BlockSpec((ts, 4 * W), lambda b, i: (b * nb + i, COL_RW)),
                  vec(4 * W), vec(W), vec(W), full(wa2), full(g2), vec(W), vec(W), vec(W), vec(W), vec(W)],
        out_specs=pl.BlockSpec((ts, W), lambda b, i: (b * nb + i, 0)),
        out_shape=jax.ShapeDtypeStruct((T, W), F32),
        scratch_shapes=[pltpu.VMEM((8, 4 * W), F32), pltpu.VMEM((W, W), F32)]
                       + [pltpu.VMEM((ts, W), F32)] * 8,
        compiler_params=_params("arbitrary", "arbitrary"),
        name="rwkv7",
    )(p, mu, w0, a0, wa2, g2, k_k, k_a, r_k, ln_w, ln_b)


def _sb_kernel(q_ref, k_ref, v_ref, qg_ref, kg_ref, o_ref, kb, vb, acc_o, acc_l, knorm, *, tq, tk, seq, prep):
    i = pl.program_id(2)
    lane = _iota((1, LANES), 1)
    mean_bd = _head_block_matrix(LANES, 1.0 / HEAD_DIM)
    ones = jnp.ones((LANES, LANES), BF16)

    @pl.when(i == 0)
    def _():
        knorm[...] = jnp.zeros_like(knorm)

        def body(c, carry):
            rows = pl.ds(pl.multiple_of(c * prep, prep), prep)
            kf = k_ref[rows, :]
            ms = _nn_rhs_exact(kf * kf, mean_bd)
            kn = (kf * lax.rsqrt(ms + NORM_EPS) * kg_ref[...]).astype(BF16)
            kb[rows, :] = kn
            vb[rows, :] = v_ref[rows, :].astype(BF16)
            kn = kn.astype(F32)
            sq = _nn_rhs_exact(kn * kn, ones)
            knorm[...] = jnp.maximum(knorm[...], jnp.max(sq.reshape(prep // 8, 8, LANES), axis=0))
            return carry
        lax.fori_loop(0, seq // prep, body, 0)

    q = q_ref[...]
    ms = _nn_rhs_exact(q * q, mean_bd)
    qn = q * lax.rsqrt(ms + NORM_EPS) * qg_ref[...] * (HEAD_DIM ** -0.5 * LOG2_E)
    q2 = jnp.concatenate([jnp.where(lane < 64, qn, 0.0), jnp.where(lane >= 64, qn, 0.0)], axis=0).astype(BF16)

    q2f = q2.astype(F32)
    qsq = jnp.max(_nn_rhs_exact(q2f * q2f, ones))
    zmax = jnp.sqrt(qsq * jnp.max(knorm[...])) * 1.01
    stop_at = 152.0 + zmax * (1.0 / 256.0)

    acc_o[...] = jnp.zeros_like(acc_o)
    acc_l[...] = jnp.zeros_like(acc_l)
    rev_incl = jnp.where(_iota((tk, tk), 0) >= _iota((tk, tk), 1), 1.0, 0.0).astype(BF16)
    qpos = i * tq + (_iota((2 * tq, tk), 0) & (tq - 1))

    def step(j, masked):
        rows = pl.ds(pl.multiple_of(j * tk, tk), tk)
        z2 = _nt(q2, kb[rows, :])
        sp = jnp.maximum(z2, 0.0) + jnp.log2(1.0 + jnp.exp2(-jnp.abs(z2)))
        if masked:
            valid = (j * tk + _iota((2 * tq, tk), 1)) < qpos
            sp = jnp.where(valid, sp, 0.0)
        cum = _nn(sp.astype(BF16), rev_incl)
        seen = acc_l[...]
        attn = jnp.exp2(z2 - cum - jnp.concatenate([seen] * (tk // LANES), axis=1))
        if masked:
            attn = jnp.where(valid, attn, 0.0)
        acc_o[...] += _nn(attn.astype(BF16), vb[rows, :])
        seen = seen + cum[:, 0:1]
        acc_l[...] = seen
        return jnp.min(seen)

    nd = tq // tk
    for d in range(nd):
        low = step(i * nd + (nd - 1 - d), True)

    def cond(carry):
        t, low = carry
        return (t < i * nd) & (low < stop_at)

    def body(carry):
        t, _ = carry
        return t + 1, step(i * nd - 1 - t, False)
    lax.while_loop(cond, body, (0, low))

    o_ref[...] = jnp.where(lane < 64, acc_o[0:tq, :], acc_o[tq:2 * tq, :])


def _stick_breaking(p, B, S, qg, kg, tq=256, tk=256):
    tq = min(tq, S)
    nq = S // tq
    T = B * S
    prep = min(512, S)
    return pl.pallas_call(
        functools.partial(_sb_kernel, tq=tq, tk=tk, seq=S, prep=prep),
        grid=(B, 2, nq),
        in_specs=[pl.BlockSpec((tq, LANES), lambda b, h, i: (b * nq + i, COL_SB_Q + h)),
                  pl.BlockSpec((S, LANES), lambda b, h, i: (b, COL_SB_K + h)),
                  pl.BlockSpec((S, LANES), lambda b, h, i: (b, COL_SB_V + h)),
                  pl.BlockSpec((1, LANES), lambda b, h, i: (0, 0)),
                  pl.BlockSpec((1, LANES), lambda b, h, i: (0, 0))],
        out_specs=pl.BlockSpec((tq, LANES), lambda b, h, i: (b * nq + i, h)),
        out_shape=jax.ShapeDtypeStruct((T, WIDTH), F32),
        scratch_shapes=[pltpu.VMEM((S, LANES), BF16), pltpu.VMEM((S, LANES), BF16),
                        pltpu.VMEM((2 * tq, LANES), F32), pltpu.VMEM((2 * tq, LANES), F32),
                        pltpu.VMEM((8, LANES), F32)],
        compiler_params=_params("arbitrary", "arbitrary", "arbitrary"),
        name="stick_breaking",
    )(p, p, p, qg, kg)


def _ret_kernel(q_ref, k_ref, v_ref, gt_ref, cos_ref, sin_ref, gn_ref, dm_ref, kdec_ref, qdec_ref, cdec_ref,
                o_ref, st_ref, *, ts):
    C = RET_CHUNK
    i = pl.program_id(2)

    @pl.when(i == 0)
    def _():
        st_ref[...] = jnp.zeros_like(st_ref)

    lane = _iota((1, LANES), 1)
    first_half = (lane & 63) < 32
    same_head = (_iota((LANES, LANES), 0) >> 6) == (_iota((LANES, LANES), 1) >> 6)
    mean_bd = _head_block_matrix(LANES, 1.0 / HEAD_DIM)

    def rotary(t, cos, sin_signed):
        swapped = jnp.where(first_half, pltpu.roll(t, LANES - 32, 1), pltpu.roll(t, 32, 1))
        return t * cos + swapped * sin_signed

    for c in range(ts // C):
        rows = slice(c * C, (c + 1) * C)
        cos = cos_ref[rows, :]
        sin = sin_ref[rows, :]
        q = rotary(q_ref[rows, :], cos, sin)
        k = rotary(k_ref[rows, :], cos, sin) * (HEAD_DIM ** -0.5)
        vb = v_ref[rows, :].astype(BF16)
        kb = k.astype(BF16)
        q2 = jnp.concatenate([jnp.where(lane < 64, q, 0.0), jnp.where(lane >= 64, q, 0.0)], axis=0).astype(BF16)
        scores = _nt(q2, kb) * dm_ref[0]
        intra2 = _nn(scores.astype(BF16), vb)
        intra = jnp.where(lane < 64, intra2[0:C], intra2[C:2 * C])
        st = st_ref[...]
        inter = _mm(q * qdec_ref[0], st)
        kv = _nn((k * kdec_ref[0]).T.astype(BF16), vb)
        st_ref[...] = cdec_ref[0] * st + jnp.where(same_head, kv, 0.0)
        o = intra + inter
        oc = o - _nn_rhs_exact(o, mean_bd)
        var = _nn_rhs_exact(oc * oc, mean_bd)
        on = oc * lax.rsqrt(var + RET_GN_EPS) * gn_ref[...]
        gt = gt_ref[rows, :]
        o_ref[rows, :] = on * (gt * _sigmoid(gt))


def _retention_tables(S):
    C = RET_CHUNK
    inv_freq = ROPE_BASE ** (-jnp.arange(0, HEAD_DIM, 2, dtype=F32) / HEAD_DIM)
    ang = jnp.arange(S, dtype=F32)[:, None] * inv_freq[None, :]
    cos = jnp.tile(jnp.cos(ang), (1, 4))
    sin = jnp.tile(jnp.concatenate([-jnp.sin(ang), jnp.sin(ang)], axis=1), (1, 2))
    log_gamma = jnp.log(1.0 - 2.0 ** (-5.0 - jnp.arange(N_HEADS, dtype=F32)))
    idx = jnp.arange(C, dtype=F32)
    rel = idx[:, None] - idx[None, :]
    intra = jnp.where(rel >= 0, jnp.exp(jnp.maximum(rel, 0.0) * log_gamma[:, None, None]), 0.0)
    dm = intra.reshape(2, 2 * C, C)
    lanes = lambda t: jnp.repeat(t.reshape(2, 2, C).transpose(0, 2, 1), HEAD_DIM, axis=2)
    kdec = lanes(jnp.exp((C - 1 - idx)[None, :] * log_gamma[:, None]))
    qdec = lanes(jnp.exp((idx + 1.0)[None, :] * log_gamma[:, None]))
    cdec = jnp.repeat(jnp.exp(C * log_gamma).reshape(2, 1, 2), HEAD_DIM, axis=2)
    return cos, sin, dm, kdec, qdec, cdec


def _retention(p, B, S, gn, tables, ts=512):
    ts = min(ts, S)
    nb = S // ts
    T = B * S
    C = RET_CHUNK
    cos, sin, dm, kdec, qdec, cdec = tables
    col = lambda c0: pl.BlockSpec((ts, LANES), lambda b, h, i: (b * nb + i, c0 + h))
    tab = pl.BlockSpec((ts, LANES), lambda b, h, i: (i, 0))
    return pl.pallas_call(
        functools.partial(_ret_kernel, ts=ts),
        grid=(B, 2, nb),
        in_specs=[col(COL_RET_Q), col(COL_RET_K), col(COL_RET_V), col(COL_RET_G), tab, tab,
                  pl.BlockSpec((1, LANES), lambda b, h, i: (0, h)),
                  pl.BlockSpec((1, 2 * C, C), lambda b, h, i: (h, 0, 0)),
                  pl.BlockSpec((1, C, LANES), lambda b, h, i: (h, 0, 0)),
                  pl.BlockSpec((1, C, LANES), lambda b, h, i: (h, 0, 0)),
                  pl.BlockSpec((1, 1, LANES), lambda b, h, i: (h, 0, 0))],
        out_specs=pl.BlockSpec((ts, LANES), lambda b, h, i: (b * nb + i, h)),
        out_shape=jax.ShapeDtypeStruct((T, WIDTH), F32),
        scratch_shapes=[pltpu.VMEM((LANES, LANES), F32)],
        compiler_params=_params("arbitrary", "arbitrary", "arbitrary"),
        name="retention",
    )(p, p, p, p, cos, sin, gn, dm, kdec, qdec, cdec)


def _conv_kernel(ua_ref, ub_ref, dw_ref, db_ref, lnw_ref, lnb_ref, o_ref, buf, *, ts):
    i = pl.program_id(1)

    @pl.when(i == 0)
    def _():
        buf[0:CONV_HALO, :] = jnp.zeros((CONV_HALO, WIDTH), F32)

    buf[CONV_HALO:, :] = ua_ref[...] * _sigmoid(ub_ref[...])
    acc = jnp.zeros((ts, WIDTH), F32) + db_ref[...]
    for j in range(CONV_WIDTH):
        start = CONV_HALO - (CONV_WIDTH - 1) + j
        acc = acc + dw_ref[j:j + 1, :] * buf[start:start + ts, :]
    buf[0:CONV_HALO, :] = buf[ts:ts + CONV_HALO, :]
    xc = acc - jnp.mean(acc, axis=-1, keepdims=True)
    var = jnp.mean(xc * xc, axis=-1, keepdims=True)
    y = xc * lax.rsqrt(var + CONV_LN_EPS) * lnw_ref[...] + lnb_ref[...]
    o_ref[...] = y * _sigmoid(y)


def _conformer_conv(p, B, S, dw, db, ln_w, ln_b, ts=512):
    ts = min(ts, S)
    nb = S // ts
    T = B * S
    vec = pl.BlockSpec((1, WIDTH), lambda b, i: (0, 0))
    return pl.pallas_call(
        functools.partial(_conv_kernel, ts=ts),
        grid=(B, nb),
        in_specs=[pl.BlockSpec((ts, WIDTH), lambda b, i: (b * nb + i, COL_CONV_A)),
                  pl.BlockSpec((ts, WIDTH), lambda b, i: (b * nb + i, COL_CONV_B)),
                  pl.BlockSpec((CONV_HALO, WIDTH), lambda b, i: (0, 0)), vec, vec, vec],
        out_specs=pl.BlockSpec((ts, WIDTH), lambda b, i: (b * nb + i, 0)),
        out_shape=jax.ShapeDtypeStruct((T, WIDTH), F32),
        scratch_shapes=[pltpu.VMEM((ts + CONV_HALO, WIDTH), F32)],
        compiler_params=_params("arbitrary", "arbitrary"),
        name="conformer_conv",
    )(p, p, dw, db, ln_w, ln_b)


def _merge_kernel(x_ref, g_ref, y0_ref, y1_ref, y2_ref, y3_ref, wg_ref, wb_ref, wo_ref, o_ref):
    x = x_ref[...]
    h = _rmsnorm_rows(x, g_ref[...]).astype(BF16)
    merged = None
    for n, y_ref in enumerate((y0_ref, y1_ref, y2_ref, y3_ref)):
        term = _sigmoid(_nn(h, wg_ref[n])) * _nn(y_ref[...].astype(BF16), wb_ref[n])
        merged = term if merged is None else merged + term
    o_ref[...] = x + _nn(merged.astype(BF16), wo_ref[...])


def _merge(x2, gain, ys, wg, wb, wo, tm=512):
    T, D = x2.shape
    tm = min(tm, T)
    row = lambda n: pl.BlockSpec((tm, n), lambda i: (i, 0))
    return pl.pallas_call(
        _merge_kernel,
        grid=(T // tm,),
        in_specs=[row(D), pl.BlockSpec((1, D), lambda i: (0, 0)), row(WIDTH), row(WIDTH), row(WIDTH), row(WIDTH),
                  pl.BlockSpec(wg.shape, lambda i: (0, 0, 0)),
                  pl.BlockSpec(wb.shape, lambda i: (0, 0, 0)),
                  pl.BlockSpec(wo.shape, lambda i: (0, 0))],
        out_specs=row(D),
        out_shape=jax.ShapeDtypeStruct((T, D), F32),
        compiler_params=_params("parallel"),
        name="merge",
    )(x2, gain, *ys, wg, wb, wo)


def _ffn_kernel(x_ref, g_ref, w1_ref, w3_ref, w2_ref, o_ref, h_s):
    f = pl.program_id(1)

    @pl.when(f == 0)
    def _():
        x = x_ref[...]
        h_s[...] = _rmsnorm_rows(x, g_ref[...]).astype(BF16)
        o_ref[...] = x

    h = h_s[...]
    a = _nn(h, w1_ref[...])
    b = _nn(h, w3_ref[...])
    o_ref[...] += _nn((a * _sigmoid(a) * b).astype(BF16), w2_ref[...])


def _ffn(x2, gain, w1, w3, w2, tm=1024, tf=1408):
    T, D = x2.shape
    tm = min(tm, T)
    nf = w1.shape[1] // tf
    return pl.pallas_call(
        _ffn_kernel,
        grid=(T // tm, nf),
        in_specs=[pl.BlockSpec((tm, D), lambda i, f: (i, 0)),
                  pl.BlockSpec((1, D), lambda i, f: (0, 0)),
                  pl.BlockSpec((D, tf), lambda i, f: (0, f)),
                  pl.BlockSpec((D, tf), lambda i, f: (0, f)),
                  pl.BlockSpec((tf, D), lambda i, f: (f, 0))],
        out_specs=pl.BlockSpec((tm, D), lambda i, f: (i, 0)),
        out_shape=jax.ShapeDtypeStruct((T, D), F32),
        scratch_shapes=[pltpu.VMEM((tm, D), BF16)],
        compiler_params=_params("parallel", "arbitrary"),
        name="ffn",
    )(x2, gain, w1, w3, w2)


def _moe_kernel(x_ref, g_ref, rt_ref, w1_ref, w3_ref, w2_ref, o_ref, h_s, gate_s):
    e = pl.program_id(1)
    lane = _iota((1, LANES), 1)

    @pl.when(e == 0)
    def _():
        x = x_ref[...]
        h = _rmsnorm_rows(x, g_ref[...])
        h_s[...] = h.astype(BF16)
        o_ref[...] = x
        logits = jnp.where(lane < N_EXPERTS, _nn_3pass(h, rt_ref[...]), -jnp.inf)
        m1 = jnp.max(logits, axis=-1, keepdims=True)
        i1 = jnp.min(jnp.where(logits == m1, lane, LANES), axis=-1, keepdims=True)
        rest = jnp.where(lane == i1, -jnp.inf, logits)
        m2 = jnp.max(rest, axis=-1, keepdims=True)
        i2 = jnp.min(jnp.where(rest == m2, lane, LANES), axis=-1, keepdims=True)
        e2 = jnp.exp(m2 - m1)
        den = 1.0 + e2
        gate_s[...] = jnp.where(lane == i1, 1.0 / den, 0.0) + jnp.where(lane == i2, e2 / den, 0.0)

    h = h_s[...]
    a = _nn(h, w1_ref[0])
    b = _nn(h, w3_ref[0])
    y = _nn((a * _sigmoid(a) * b).astype(BF16), w2_ref[0])
    gate = jnp.sum(jnp.where(lane == e, gate_s[...], 0.0), axis=-1, keepdims=True)
    o_ref[...] += gate * y


def _moe(x2, gain, router_pad, w1, w3, w2, tm=1024):
    T, D = x2.shape
    tm = min(tm, T)
    E, _, F = w1.shape
    return pl.pallas_call(
        _moe_kernel,
        grid=(T // tm, E),
        in_specs=[pl.BlockSpec((tm, D), lambda i, e: (i, 0)),
                  pl.BlockSpec((1, D), lambda i, e: (0, 0)),
                  pl.BlockSpec((D, LANES), lambda i, e: (0, 0)),
                  pl.BlockSpec((1, D, F), lambda i, e: (e, 0, 0)),
                  pl.BlockSpec((1, D, F), lambda i, e: (e, 0, 0)),
                  pl.BlockSpec((1, F, D), lambda i, e: (e, 0, 0))],
        out_specs=pl.BlockSpec((tm, D), lambda i, e: (i, 0)),
        out_shape=jax.ShapeDtypeStruct((T, D), F32),
        scratch_shapes=[pltpu.VMEM((tm, D), BF16), pltpu.VMEM((tm, LANES), F32)],
        compiler_params=_params("parallel", "arbitrary"),
        name="moe",
    )(x2, gain, router_pad, w1, w3, w2)


def kernel(x, norm_mix, w_in, rw_mu, rw_w0, rw_w2, rw_a0, rw_a2, rw_g2, rw_k_k, rw_k_a, rw_r_k, rw_ln_w, rw_ln_b, sb_q_norm, sb_k_norm, ret_gn, conv_dw, conv_b, conv_ln_w, conv_ln_b, w_gate, w_branch, w_out, norm_ffn, ffn_w1, ffn_w3, ffn_w2, router, moe_w1, moe_w3, moe_w2):
    B, S, D = x.shape
    depth = norm_mix.shape[0]
    x2 = x.reshape(B * S, D)
    tables = _retention_tables(S)
    row = lambda t: t.reshape(1, -1)
    for l in range(depth):
        p = _inproj(x2, row(norm_mix[l]), w_in[l].astype(BF16))
        zeros = jnp.zeros_like(rw_w2[l])
        wa2 = jnp.concatenate([jnp.concatenate([rw_w2[l], zeros], axis=1),
                               jnp.concatenate([zeros, rw_a2[l]], axis=1)], axis=0).astype(BF16)
        y_rw = _rwkv(p, B, S, row(rw_mu[l]), row(rw_w0[l]), row(rw_a0[l]), wa2, rw_g2[l].astype(BF16),
                     row(rw_k_k[l]), row(rw_k_a[l]), row(rw_r_k[l]), row(rw_ln_w[l]), row(rw_ln_b[l]))
        y_sb = _stick_breaking(p, B, S, row(jnp.tile(sb_q_norm[l], 2)), row(jnp.tile(sb_k_norm[l], 2)))
        y_ret = _retention(p, B, S, row(ret_gn[l]), tables)
        dw = jnp.concatenate([conv_dw[l], jnp.zeros((CONV_HALO - CONV_WIDTH, WIDTH), F32)], axis=0)
        y_conv = _conformer_conv(p, B, S, dw, row(conv_b[l]), row(conv_ln_w[l]), row(conv_ln_b[l]))
        x2 = _merge(x2, row(norm_mix[l]), (y_rw, y_sb, y_ret, y_conv),
                    w_gate[l].astype(BF16), w_branch[l].astype(BF16), w_out[l].astype(BF16))
        if l % 2 == 0:
            x2 = _ffn(x2, row(norm_ffn[l]), ffn_w1[l // 2].astype(BF16), ffn_w3[l // 2].astype(BF16),
                      ffn_w2[l // 2].astype(BF16))
        else:
            rt = jnp.concatenate([router[l // 2], jnp.zeros((D, LANES - N_EXPERTS), F32)], axis=1)
            x2 = _moe(x2, row(norm_ffn[l]), rt, moe_w1[l // 2].astype(BF16), moe_w3[l // 2].astype(BF16),
                      moe_w2[l // 2].astype(BF16))
    return x2.reshape(B, S, D)
```

```python
import functools

import jax
import jax.numpy as jnp
from jax import lax
from jax.experimental import pallas as pl
from jax.experimental.pallas import tpu as pltpu

F32 = jnp.float32
BF16 = jnp.bfloat16

LANES = 128
VMEM_LIMIT = 56 * 1024 * 1024

D_MODEL = 1024
HEAD_DIM = 64
N_HEADS = 4
WIDTH = N_HEADS * HEAD_DIM
NORM_EPS = 1e-6
LOG2_E = 1.4426950408889634
RW_DECAY_SCALE = 0.606531
RW_LN_EPS = 64e-5
RW_CHUNK = 64
RET_CHUNK = 128
RET_GN_EPS = 1e-5
ROPE_BASE = 10000.0
CONV_WIDTH = 31
CONV_HALO = 32
CONV_LN_EPS = 1e-5
N_EXPERTS = 8
N_IN = 3328
COL_RW = 0
COL_SB_Q, COL_SB_K, COL_SB_V = 8, 10, 12
COL_RET_Q, COL_RET_K, COL_RET_V, COL_RET_G = 14, 16, 18, 20
COL_CONV_A, COL_CONV_B = 11, 12


def _nn(a, b):
    return lax.dot_general(a, b, (((1,), (0,)), ((), ())), preferred_element_type=F32)


def _nt(a, b):
    return lax.dot_general(a, b, (((1,), (1,)), ((), ())), preferred_element_type=F32)


def _mm(a, b):
    return _nn(a.astype(BF16), b.astype(BF16))


def _split(x):
    hi = x.astype(BF16)
    lo = (x - hi.astype(F32)).astype(BF16)
    return hi, lo


def _nn_rhs_exact(x, m):
    hi, lo = _split(x)
    return _nn(hi, m) + _nn(lo, m)


def _nn_lhs_exact(m, x):
    hi, lo = _split(x)
    return _nn(m, hi) + _nn(m, lo)


def _nn_3pass(a, b):
    ah, al = _split(a)
    bh, bl = _split(b)
    return _nn(ah, bh) + (_nn(ah, bl) + _nn(al, bh))


def _iota(shape, axis):
    return lax.broadcasted_iota(jnp.int32, shape, axis)


def _head_block_matrix(n, value):
    same = (_iota((n, n), 0) >> 6) == (_iota((n, n), 1) >> 6)
    return jnp.where(same, value, 0.0).astype(BF16)


def _sigmoid(x):
    return 0.5 * jnp.tanh(0.5 * x) + 0.5


def _rmsnorm_rows(x, gain):
    return x * lax.rsqrt(jnp.mean(x * x, axis=-1, keepdims=True) + NORM_EPS) * gain


def _params(*sem):
    return pltpu.CompilerParams(dimension_semantics=sem, vmem_limit_bytes=VMEM_LIMIT)


def _inproj_kernel(x_ref, g_ref, w_ref, p_ref):
    h = _rmsnorm_rows(x_ref[...], g_ref[...])
    p_ref[...] = _nn(h.astype(BF16), w_ref[...])


def _inproj(x2, gain, w_bf16, tm=512):
    T, D = x2.shape
    N = w_bf16.shape[1]
    return pl.pallas_call(
        _inproj_kernel,
        grid=(T // tm,),
        in_specs=[pl.BlockSpec((tm, D), lambda i: (i, 0)),
                  pl.BlockSpec((1, D), lambda i: (0, 0)),
                  pl.BlockSpec((D, N), lambda i: (0, 0))],
        out_specs=pl.BlockSpec((tm, N), lambda i: (i, 0)),
        out_shape=jax.ShapeDtypeStruct((T, N), F32),
        compiler_params=_params("parallel"),
        name="inproj",
    )(x2, gain, w_bf16)


def _rw_kernel(p_ref, mu_ref, w0_ref, a0_ref, wa2_ref, g2_ref, kk_ref, ka_ref, rk_ref, lnw_ref, lnb_ref,
               o_ref, prev_ref, zt_ref, at_s, bt_s, kt_s, rt_s, tw_s, mrb_s, v_s, ecl_s, x_s, mv_s, y_s, bonus_s, g_s,
               *, ts, nbatch, group):
    C = RW_CHUNK
    W = WIDTH
    i = pl.program_id(0)

    @pl.when(i == 0)
    def _():
        prev_ref[...] = jnp.zeros_like(prev_ref)
        zt_ref[...] = jnp.zeros_like(zt_ref)

    lane = _iota((1, LANES), 1)
    row = _iota((ts, 1), 0)
    r4 = _iota((W, W), 0)
    c4 = _iota((W, W), 1)
    same_head = (r4 >> 6) == (c4 >> 6)
    tri4 = jnp.where(same_head & (c4 <= r4), 1.0, 0.0).astype(BF16)
    ones_bd = _head_block_matrix(W, 1.0)

    for b in range(nbatch):
        p = p_ref[b]
        shifted = jnp.where(row == 0, prev_ref[b, 0:1, :], pltpu.roll(p, 1, 0))
        prev_ref[b, 0:1, :] = p_ref[b, ts - 1:ts, :]
        pm = p + (shifted - p) * mu_ref[...]
        r = pm[:, 0:W]
        k = pm[:, W:2 * W]
        v = pm[:, 2 * W:3 * W]
        lora = pm[:, 3 * W:3 * W + LANES]
        pg = pm[:, 3 * W + LANES:]
        wa = _mm(jnp.where(lane < 64, jnp.tanh(lora), lora), wa2_ref[...])
        lw = -RW_DECAY_SCALE * _sigmoid(w0_ref[...] + wa[:, 0:W])
        a = _sigmoid(a0_ref[...] + wa[:, W:])
        g_s[b] = _mm(_sigmoid(pg), g2_ref[...])
        kk = k * kk_ref[...]
        kk = kk * jnp.minimum(lax.rsqrt(_nn_rhs_exact(kk * kk, ones_bd)), 1e12)
        kp = k * (1.0 + (a - 1.0) * ka_ref[...])
        bonus_s[b] = _nn_rhs_exact(r * kp * rk_ref[...], ones_bd) * v
        cl = jnp.concatenate([_nn_lhs_exact(tri4, lw[n * W:(n + 1) * W, :]) for n in range(ts // W)], axis=0)
        ecl = jnp.exp(cl)
        eml = jnp.exp(-cl)
        at_s[b] = (-kk * jnp.exp(cl - lw)).astype(BF16)
        bt_s[b] = (kk * a * eml).astype(BF16)
        kt_s[b] = (kp * eml).astype(BF16)
        rt_s[b] = (r * ecl).astype(BF16)
        v_s[b] = v
        ecl_s[b] = ecl

    head = _iota((1, W), 1) >> 6
    tw = _iota((C, W), 0)
    sw = _iota((C, W), 1) & (C - 1)
    strict = tw > sw
    incl = tw >= sw
    eye = jnp.where(tw == sw, 1.0, 0.0)

    def stack4(x):
        return jnp.concatenate([jnp.where(head == h, x, jnp.zeros_like(x)) for h in range(N_HEADS)], axis=0)

    def block_diag(x):
        return jnp.concatenate([x.astype(BF16)] * N_HEADS, axis=0) * ones_bd

    chunks = [(b, slice(c * C, (c + 1) * C)) for c in range(ts // C) for b in range(nbatch)]
    for g0 in range(0, len(chunks), group):
        grp = chunks[g0:g0 + group]
        n = range(len(grp))
        at = [at_s[b, rows, :] for b, rows in grp]
        rt = [rt_s[b, rows, :] for b, rows in grp]
        b4 = [stack4(bt_s[b, rows, :]) for b, rows in grp]
        k4 = [stack4(kt_s[b, rows, :]) for b, rows in grp]
        v4 = [stack4(v_s[b, rows, :].astype(BF16)) for b, rows in grp]
        lab = [jnp.where(strict, _nt(at[j], b4[j]), 0.0) for j in n]
        lak = [jnp.where(strict, _nt(at[j], k4[j]), 0.0) for j in n]
        mrb = [jnp.where(incl, _nt(rt[j], b4[j]), 0.0) for j in n]
        mrk = [jnp.where(incl, _nt(rt[j], k4[j]), 0.0) for j in n]
        tinv = [eye + lab[j] for j in n]
        m = lab
        m_bd = [block_diag(m[j]) for j in n]
        for _ in range(5):
            m = [_nn(m[j].astype(BF16), m_bd[j]) for j in n]
            m_bd = [block_diag(m[j]) for j in n]
            tinv = [tinv[j] + _nn(tinv[j].astype(BF16), m_bd[j]) for j in n]
        for j, (b, rows) in enumerate(grp):
            tw_s[b, rows, :] = tinv[j].astype(BF16)
            mrb_s[b, rows, :] = mrb[j].astype(BF16)
            x_s[b, rows, :] = _nn(lak[j].astype(BF16), v4[j])
            mv_s[b, rows, :] = _nn(mrk[j].astype(BF16), v4[j])

    def chunk(c, carry):
        rows = pl.ds(pl.multiple_of(c * C, C), C)
        nb = range(nbatch)
        zt = [zt_ref[b] for b in nb]
        ztb = [zt[b].astype(BF16) for b in nb]
        az = [_nt(at_s[b, rows, :], ztb[b]) + x_s[b, rows, :] for b in nb]
        u = [_nn(tw_s[b, rows, :], stack4(az[b]).astype(BF16)) for b in nb]
        uvt = [jnp.concatenate([u[b], v_s[b, rows, :]], axis=0).T.astype(BF16) for b in nb]
        bk = [jnp.concatenate([bt_s[b, rows, :], kt_s[b, rows, :]], axis=0) for b in nb]
        for b in nb:
            g_end = ecl_s[b, pl.ds(c * C + C - 1, 1), :]
            zt_ref[b] = (zt[b] + jnp.where(same_head, _nn(uvt[b], bk[b]), 0.0)) * g_end
        for b in nb:
            y_s[b, rows, :] = (_nt(rt_s[b, rows, :], ztb[b]) + _nn(mrb_s[b, rows, :], stack4(u[b]).astype(BF16))
                               + mv_s[b, rows, :])
        return carry

    lax.fori_loop(0, ts // C, chunk, 0)

    mean_bd = _head_block_matrix(W, 1.0 / HEAD_DIM)
    for b in range(nbatch):
        y = y_s[b]
        yc = y - _nn_rhs_exact(y, mean_bd)
        var = _nn_rhs_exact(yc * yc, mean_bd)
        y = yc * lax.rsqrt(var + RW_LN_EPS) * lnw_ref[...] + lnb_ref[...]
        o_ref[b] = (y + bonus_s[b]) * g_s[b]


def _rwkv(p, B, S, mu, w0, a0, wa2, g2, k_k, k_a, r_k, ln_w, ln_b, ts=512, group=8):
    ts = min(ts, S)
    W = WIDTH
    vec = lambda n: pl.BlockSpec((1, n), lambda i: (0, 0))
    full = lambda a: pl.BlockSpec(a.shape, lambda i: (0, 0))
    out = pl.pallas_call(
        functools.partial(_rw_kernel, ts=ts, nbatch=B, group=group),
        grid=(S // ts,),
        in_specs=[pl.BlockSpec((B, ts, 4 * W), lambda i: (0, i, COL_RW)),
                  vec(4 * W), vec(W), vec(W), full(wa2), full(g2), vec(W), vec(W), vec(W), vec(W), vec(W)],
        out_specs=pl.BlockSpec((B, ts, W), lambda i: (0, i, 0)),
        out_shape=jax.ShapeDtypeStruct((B, S, W), F32),
        scratch_shapes=[pltpu.VMEM((B, 8, 4 * W), F32), pltpu.VMEM((B, W, W), F32)]
                       + [pltpu.VMEM((B, ts, W), BF16)] * 6 + [pltpu.VMEM((B, ts, W), F32)] * 7,
        compiler_params=_params("arbitrary"),
        name="rwkv7",
    )(p.reshape(B, S, -1), mu, w0, a0, wa2, g2, k_k, k_a, r_k, ln_w, ln_b)
    return out.reshape(B * S, W)


def _sb_kernel(q_ref, k_ref, v_ref, qg_ref, kg_ref, o_ref, kb, vb, acc_o, acc_l, knorm, *, tq, tk, seq, prep):
    i = pl.program_id(2)
    lane = _iota((1, LANES), 1)
    mean_bd = _head_block_matrix(LANES, 1.0 / HEAD_DIM)
    ones = jnp.ones((LANES, LANES), BF16)

    @pl.when(i == 0)
    def _():
        knorm[...] = jnp.zeros_like(knorm)

        def body(c, carry):
            rows = pl.ds(pl.multiple_of(c * prep, prep), prep)
            kf = k_ref[rows, :]
            ms = _nn_rhs_exact(kf * kf, mean_bd)
            kn = (kf * lax.rsqrt(ms + NORM_EPS) * kg_ref[...]).astype(BF16)
            kb[rows, :] = kn
            vb[rows, :] = v_ref[rows, :].astype(BF16)
            kn = kn.astype(F32)
            sq = _nn_rhs_exact(kn * kn, ones)
            knorm[...] = jnp.maximum(knorm[...], jnp.max(sq.reshape(prep // 8, 8, LANES), axis=0))
            return carry
        lax.fori_loop(0, seq // prep, body, 0)

    q = q_ref[...]
    ms = _nn_rhs_exact(q * q, mean_bd)
    qn = q * lax.rsqrt(ms + NORM_EPS) * qg_ref[...] * (HEAD_DIM ** -0.5 * LOG2_E)
    q2 = jnp.concatenate([jnp.where(lane < 64, qn, 0.0), jnp.where(lane >= 64, qn, 0.0)], axis=0).astype(BF16)

    q2f = q2.astype(F32)
    qsq = jnp.max(_nn_rhs_exact(q2f * q2f, ones))
    zmax = jnp.sqrt(qsq * jnp.max(knorm[...])) * 1.01
    stop_at = 152.0 + zmax * (1.0 / 256.0)

    def softplus2(z2):
        return jnp.maximum(z2, 0.0) + jnp.log2(1.0 + jnp.exp2(-jnp.abs(z2)))

    def rev_incl(n):
        return jnp.where(_iota((n, n), 0) >= _iota((n, n), 1), 1.0, 0.0).astype(BF16)

    prev = pl.ds(pl.multiple_of(jnp.maximum(i - 1, 0) * tk, tk), tk)
    diag = pl.ds(pl.multiple_of(i * tk, tk), tk)
    z2 = _nt(q2, jnp.concatenate([kb[prev, :], kb[diag, :]], axis=0))
    kpos = (i - 1) * tk + _iota((2 * tq, 2 * tk), 1)
    qpos = i * tq + (_iota((2 * tq, 2 * tk), 0) & (tq - 1))
    valid = (kpos < qpos) & (kpos >= 0)
    cum = _nn(jnp.where(valid, softplus2(z2), 0.0).astype(BF16), rev_incl(2 * tk))
    attn = jnp.where(valid, jnp.exp2(z2 - cum), 0.0)
    acc_o[...] = _nn(attn.astype(BF16), jnp.concatenate([vb[prev, :], vb[diag, :]], axis=0))
    seen = jnp.broadcast_to(cum[:, 0:1], (2 * tq, LANES))
    acc_l[...] = seen
    rev = rev_incl(tk)

    def step(j):
        rows = pl.ds(pl.multiple_of(j * tk, tk), tk)
        z2 = _nt(q2, kb[rows, :])
        cum = _nn(softplus2(z2).astype(BF16), rev)
        seen = acc_l[...]
        attn = jnp.exp2(z2 - cum - jnp.concatenate([seen] * (tk // LANES), axis=1))
        acc_o[...] += _nn(attn.astype(BF16), vb[rows, :])
        seen = seen + cum[:, 0:1]
        acc_l[...] = seen
        return jnp.min(seen)

    def cond(carry):
        t, low = carry
        return (t < i) & (low < stop_at)

    def body(carry):
        t, _ = carry
        return t + 1, step(i - 1 - t)
    lax.while_loop(cond, body, (1, jnp.min(seen)))

    o_ref[...] = jnp.where(lane < 64, acc_o[0:tq, :], acc_o[tq:2 * tq, :])


def _stick_breaking(p, B, S, qg, kg, tq=256, tk=256):
    assert tq == tk and S % tq == 0
    nq = S // tq
    T = B * S
    prep = min(512, S)
    return pl.pallas_call(
        functools.partial(_sb_kernel, tq=tq, tk=tk, seq=S, prep=prep),
        grid=(B, 2, nq),
        in_specs=[pl.BlockSpec((tq, LANES), lambda b, h, i: (b * nq + i, COL_SB_Q + h)),
                  pl.BlockSpec((S, LANES), lambda b, h, i: (b, COL_SB_K + h)),
                  pl.BlockSpec((S, LANES), lambda b, h, i: (b, COL_SB_V + h)),
                  pl.BlockSpec((1, LANES), lambda b, h, i: (0, 0)),
                  pl.BlockSpec((1, LANES), lambda b, h, i: (0, 0))],
        out_specs=pl.BlockSpec((tq, LANES), lambda b, h, i: (b * nq + i, h)),
        out_shape=jax.ShapeDtypeStruct((T, WIDTH), F32),
        scratch_shapes=[pltpu.VMEM((S, LANES), BF16), pltpu.VMEM((S, LANES), BF16),
                        pltpu.VMEM((2 * tq, LANES), F32), pltpu.VMEM((2 * tq, LANES), F32),
                        pltpu.VMEM((8, LANES), F32)],
        compiler_params=_params("arbitrary", "arbitrary", "arbitrary"),
        name="stick_breaking",
    )(p, p, p, qg, kg)


def _ret_kernel(q_ref, k_ref, v_ref, gt_ref, cos_ref, sin_ref, gn_ref, dm_ref, kdec_ref, qdec_ref, cdec_ref,
                o_ref, st_ref, *, ts):
    C = RET_CHUNK
    i = pl.program_id(2)

    @pl.when(i == 0)
    def _():
        st_ref[...] = jnp.zeros_like(st_ref)

    lane = _iota((1, LANES), 1)
    first_half = (lane & 63) < 32
    same_head = (_iota((LANES, LANES), 0) >> 6) == (_iota((LANES, LANES), 1) >> 6)
    mean_bd = _head_block_matrix(LANES, 1.0 / HEAD_DIM)

    def rotary(t, cos, sin_signed):
        swapped = jnp.where(first_half, pltpu.roll(t, LANES - 32, 1), pltpu.roll(t, 32, 1))
        return t * cos + swapped * sin_signed

    for c in range(ts // C):
        rows = slice(c * C, (c + 1) * C)
        cos = cos_ref[rows, :]
        sin = sin_ref[rows, :]
        q = rotary(q_ref[rows, :], cos, sin)
        k = rotary(k_ref[rows, :], cos, sin) * (HEAD_DIM ** -0.5)
        vb = v_ref[rows, :].astype(BF16)
        kb = k.astype(BF16)
        q2 = jnp.concatenate([jnp.where(lane < 64, q, 0.0), jnp.where(lane >= 64, q, 0.0)], axis=0).astype(BF16)
        scores = _nt(q2, kb) * dm_ref[0]
        intra2 = _nn(scores.astype(BF16), vb)
        intra = jnp.where(lane < 64, intra2[0:C], intra2[C:2 * C])
        st = st_ref[...]
        inter = _mm(q * qdec_ref[0], st)
        kv = _nn((k * kdec_ref[0]).T.astype(BF16), vb)
        st_ref[...] = cdec_ref[0] * st + jnp.where(same_head, kv, 0.0)
        o = intra + inter
        oc = o - _nn_rhs_exact(o, mean_bd)
        var = _nn_rhs_exact(oc * oc, mean_bd)
        on = oc * lax.rsqrt(var + RET_GN_EPS) * gn_ref[...]
        gt = gt_ref[rows, :]
        o_ref[rows, :] = on * (gt * _sigmoid(gt))


def _retention_tables(S):
    C = RET_CHUNK
    inv_freq = ROPE_BASE ** (-jnp.arange(0, HEAD_DIM, 2, dtype=F32) / HEAD_DIM)
    ang = jnp.arange(S, dtype=F32)[:, None] * inv_freq[None, :]
    cos = jnp.tile(jnp.cos(ang), (1, 4))
    sin = jnp.tile(jnp.concatenate([-jnp.sin(ang), jnp.sin(ang)], axis=1), (1, 2))
    log_gamma = jnp.log(1.0 - 2.0 ** (-5.0 - jnp.arange(N_HEADS, dtype=F32)))
    idx = jnp.arange(C, dtype=F32)
    rel = idx[:, None] - idx[None, :]
    intra = jnp.where(rel >= 0, jnp.exp(jnp.maximum(rel, 0.0) * log_gamma[:, None, None]), 0.0)
    dm = intra.reshape(2, 2 * C, C)
    lanes = lambda t: jnp.repeat(t.reshape(2, 2, C).transpose(0, 2, 1), HEAD_DIM, axis=2)
    kdec = lanes(jnp.exp((C - 1 - idx)[None, :] * log_gamma[:, None]))
    qdec = lanes(jnp.exp((idx + 1.0)[None, :] * log_gamma[:, None]))
    cdec = jnp.repeat(jnp.exp(C * log_gamma).reshape(2, 1, 2), HEAD_DIM, axis=2)
    return cos, sin, dm, kdec, qdec, cdec


def _retention(p, B, S, gn, tables, ts=512):
    ts = min(ts, S)
    nb = S // ts
    T = B * S
    C = RET_CHUNK
    cos, sin, dm, kdec, qdec, cdec = tables
    col = lambda c0: pl.BlockSpec((ts, LANES), lambda b, h, i: (b * nb + i, c0 + h))
    tab = pl.BlockSpec((ts, LANES), lambda b, h, i: (i, 0))
    return pl.pallas_call(
        functools.partial(_ret_kernel, ts=ts),
        grid=(B, 2, nb),
        in_specs=[col(COL_RET_Q), col(COL_RET_K), col(COL_RET_V), col(COL_RET_G), tab, tab,
                  pl.BlockSpec((1, LANES), lambda b, h, i: (0, h)),
                  pl.BlockSpec((1, 2 * C, C), lambda b, h, i: (h, 0, 0)),
                  pl.BlockSpec((1, C, LANES), lambda b, h, i: (h, 0, 0)),
                  pl.BlockSpec((1, C, LANES), lambda b, h, i: (h, 0, 0)),
                  pl.BlockSpec((1, 1, LANES), lambda b, h, i: (h, 0, 0))],
        out_specs=pl.BlockSpec((ts, LANES), lambda b, h, i: (b * nb + i, h)),
        out_shape=jax.ShapeDtypeStruct((T, WIDTH), F32),
        scratch_shapes=[pltpu.VMEM((LANES, LANES), F32)],
        compiler_params=_params("arbitrary", "arbitrary", "arbitrary"),
        name="retention",
    )(p, p, p, p, cos, sin, gn, dm, kdec, qdec, cdec)


def _conv_kernel(ua_ref, ub_ref, dw_ref, db_ref, lnw_ref, lnb_ref, o_ref, buf, *, ts):
    i = pl.program_id(1)

    @pl.when(i == 0)
    def _():
        buf[0:CONV_HALO, :] = jnp.zeros((CONV_HALO, WIDTH), F32)

    buf[CONV_HALO:, :] = ua_ref[...] * _sigmoid(ub_ref[...])
    acc = jnp.zeros((ts, WIDTH), F32) + db_ref[...]
    for j in range(CONV_WIDTH):
        start = CONV_HALO - (CONV_WIDTH - 1) + j
        acc = acc + dw_ref[j:j + 1, :] * buf[start:start + ts, :]
    buf[0:CONV_HALO, :] = buf[ts:ts + CONV_HALO, :]
    xc = acc - jnp.mean(acc, axis=-1, keepdims=True)
    var = jnp.mean(xc * xc, axis=-1, keepdims=True)
    y = xc * lax.rsqrt(var + CONV_LN_EPS) * lnw_ref[...] + lnb_ref[...]
    o_ref[...] = y * _sigmoid(y)


def _conformer_conv(p, B, S, dw, db, ln_w, ln_b, ts=512):
    ts = min(ts, S)
    nb = S // ts
    T = B * S
    vec = pl.BlockSpec((1, WIDTH), lambda b, i: (0, 0))
    return pl.pallas_call(
        functools.partial(_conv_kernel, ts=ts),
        grid=(B, nb),
        in_specs=[pl.BlockSpec((ts, WIDTH), lambda b, i: (b * nb + i, COL_CONV_A)),
                  pl.BlockSpec((ts, WIDTH), lambda b, i: (b * nb + i, COL_CONV_B)),
                  pl.BlockSpec((CONV_HALO, WIDTH), lambda b, i: (0, 0)), vec, vec, vec],
        out_specs=pl.BlockSpec((ts, WIDTH), lambda b, i: (b * nb + i, 0)),
        out_shape=jax.ShapeDtypeStruct((T, WIDTH), F32),
        scratch_shapes=[pltpu.VMEM((ts + CONV_HALO, WIDTH), F32)],
        compiler_params=_params("arbitrary", "arbitrary"),
        name="conformer_conv",
    )(p, p, dw, db, ln_w, ln_b)


def _merge_kernel(x_ref, g_ref, y0_ref, y1_ref, y2_ref, y3_ref, wg_ref, wb_ref, wo_ref, o_ref):
    x = x_ref[...]
    h = _rmsnorm_rows(x, g_ref[...]).astype(BF16)
    merged = None
    for n, y_ref in enumerate((y0_ref, y1_ref, y2_ref, y3_ref)):
        term = _sigmoid(_nn(h, wg_ref[n])) * _nn(y_ref[...].astype(BF16), wb_ref[n])
        merged = term if merged is None else merged + term
    o_ref[...] = x + _nn(merged.astype(BF16), wo_ref[...])


def _merge(x2, gain, ys, wg, wb, wo, tm=512):
    T, D = x2.shape
    tm = min(tm, T)
    row = lambda n: pl.BlockSpec((tm, n), lambda i: (i, 0))
    return pl.pallas_call(
        _merge_kernel,
        grid=(T // tm,),
        in_specs=[row(D), pl.BlockSpec((1, D), lambda i: (0, 0)), row(WIDTH), row(WIDTH), row(WIDTH), row(WIDTH),
                  pl.BlockSpec(wg.shape, lambda i: (0, 0, 0)),
                  pl.BlockSpec(wb.shape, lambda i: (0, 0, 0)),
                  pl.BlockSpec(wo.shape, lambda i: (0, 0))],
        out_specs=row(D),
        out_shape=jax.ShapeDtypeStruct((T, D), F32),
        compiler_params=_params("parallel"),
        name="merge",
    )(x2, gain, *ys, wg, wb, wo)


def _ffn_kernel(x_ref, g_ref, w1_ref, w3_ref, w2_ref, o_ref, h_s):
    f = pl.program_id(1)

    @pl.when(f == 0)
    def _():
        x = x_ref[...]
        h_s[...] = _rmsnorm_rows(x, g_ref[...]).astype(BF16)
        o_ref[...] = x

    h = h_s[...]
    a = _nn(h, w1_ref[...])
    b = _nn(h, w3_ref[...])
    o_ref[...] += _nn((a * _sigmoid(a) * b).astype(BF16), w2_ref[...])


def _ffn(x2, gain, w1, w3, w2, tm=1024, tf=1408):
    T, D = x2.shape
    tm = min(tm, T)
    nf = w1.shape[1] // tf
    return pl.pallas_call(
        _ffn_kernel,
        grid=(T // tm, nf),
        in_specs=[pl.BlockSpec((tm, D), lambda i, f: (i, 0)),
                  pl.BlockSpec((1, D), lambda i, f: (0, 0)),
                  pl.BlockSpec((D, tf), lambda i, f: (0, f)),
                  pl.BlockSpec((D, tf), lambda i, f: (0, f)),
                  pl.BlockSpec((tf, D), lambda i, f: (f, 0))],
        out_specs=pl.BlockSpec((tm, D), lambda i, f: (i, 0)),
        out_shape=jax.ShapeDtypeStruct((T, D), F32),
        scratch_shapes=[pltpu.VMEM((tm, D), BF16)],
        compiler_params=_params("parallel", "arbitrary"),
        name="ffn",
    )(x2, gain, w1, w3, w2)


def _moe_kernel(x_ref, g_ref, rt_ref, w1_ref, w3_ref, w2_ref, o_ref, h_s, gate_s, rank_s, rank_t_s, count_s,
                *, tm, cap):
    e = pl.program_id(1)
    lane = _iota((1, LANES), 1)

    @pl.when(e == 0)
    def _():
        x = x_ref[...]
        h = _rmsnorm_rows(x, g_ref[...])
        h_s[...] = h.astype(BF16)
        o_ref[...] = x
        logits = jnp.where(lane < N_EXPERTS, _nn_3pass(h, rt_ref[...]), -jnp.inf)
        m1 = jnp.max(logits, axis=-1, keepdims=True)
        i1 = jnp.min(jnp.where(logits == m1, lane, LANES), axis=-1, keepdims=True)
        rest = jnp.where(lane == i1, -jnp.inf, logits)
        m2 = jnp.max(rest, axis=-1, keepdims=True)
        i2 = jnp.min(jnp.where(rest == m2, lane, LANES), axis=-1, keepdims=True)
        e2 = jnp.exp(m2 - m1)
        den = 1.0 + e2
        gate_s[...] = jnp.where(lane == i1, 1.0 / den, 0.0) + jnp.where(lane == i2, e2 / den, 0.0)
        chosen = jnp.where((lane == i1) | (lane == i2), 1.0, 0.0)
        before = jnp.where(_iota((tm, tm), 1) < _iota((tm, tm), 0), 1.0, 0.0).astype(BF16)
        rank = jnp.where(chosen > 0.0, _nn(before, chosen.astype(BF16)), -1.0)
        rank_s[...] = rank
        rank_t_s[...] = rank.T
        count_s[...] = jnp.sum(chosen, axis=0, keepdims=True)

    mine = lane == e
    gate = jnp.sum(jnp.where(mine, gate_s[...], 0.0), axis=-1, keepdims=True)
    rank_col = jnp.sum(jnp.where(mine, rank_s[...], 0.0), axis=-1, keepdims=True)
    rank_row = rank_t_s[pl.ds(e, 1), :]
    count = jnp.sum(jnp.where(mine, count_s[...], 0.0))
    slot_rows = _iota((cap, tm), 0).astype(F32)
    slot_cols = _iota((tm, cap), 1).astype(F32)

    def one_pass(first):
        take = jnp.where(rank_row - first == slot_rows, 1.0, 0.0).astype(BF16)
        xg = _nn(take, h_s[...]).astype(BF16)
        a = _nn(xg, w1_ref[0])
        b = _nn(xg, w3_ref[0])
        y = _nn((a * _sigmoid(a) * b).astype(BF16), w2_ref[0])
        put = jnp.where(rank_col - first == slot_cols, 1.0, 0.0).astype(BF16)
        o_ref[...] += gate * _nn(put, y.astype(BF16))
        return first + cap

    lax.while_loop(lambda first: first < count, one_pass, jnp.float32(0.0))


def _moe(x2, gain, router_pad, w1, w3, w2, tm=1024, cap=384):
    T, D = x2.shape
    tm = min(tm, T)
    cap = min(cap, tm)
    E, _, F = w1.shape
    return pl.pallas_call(
        functools.partial(_moe_kernel, tm=tm, cap=cap),
        grid=(T // tm, E),
        in_specs=[pl.BlockSpec((tm, D), lambda i, e: (i, 0)),
                  pl.BlockSpec((1, D), lambda i, e: (0, 0)),
                  pl.BlockSpec((D, LANES), lambda i, e: (0, 0)),
                  pl.BlockSpec((1, D, F), lambda i, e: (e, 0, 0)),
                  pl.BlockSpec((1, D, F), lambda i, e: (e, 0, 0)),
                  pl.BlockSpec((1, F, D), lambda i, e: (e, 0, 0))],
        out_specs=pl.BlockSpec((tm, D), lambda i, e: (i, 0)),
        out_shape=jax.ShapeDtypeStruct((T, D), F32),
        scratch_shapes=[pltpu.VMEM((tm, D), BF16), pltpu.VMEM((tm, LANES), F32), pltpu.VMEM((tm, LANES), F32),
                        pltpu.VMEM((LANES, tm), F32), pltpu.VMEM((1, LANES), F32)],
        compiler_params=_params("parallel", "arbitrary"),
        name="moe",
    )(x2, gain, router_pad, w1, w3, w2)


def kernel(x, norm_mix, w_in, rw_mu, rw_w0, rw_w2, rw_a0, rw_a2, rw_g2, rw_k_k, rw_k_a, rw_r_k, rw_ln_w, rw_ln_b, sb_q_norm, sb_k_norm, ret_gn, conv_dw, conv_b, conv_ln_w, conv_ln_b, w_gate, w_branch, w_out, norm_ffn, ffn_w1, ffn_w3, ffn_w2, router, moe_w1, moe_w3, moe_w2):
    B, S, D = x.shape
    depth = norm_mix.shape[0]
    x2 = x.reshape(B * S, D)
    tables = _retention_tables(S)
    row = lambda t: t.reshape(1, -1)
    for l in range(depth):
        p = _inproj(x2, row(norm_mix[l]), w_in[l].astype(BF16))
        zeros = jnp.zeros_like(rw_w2[l])
        wa2 = jnp.concatenate([jnp.concatenate([rw_w2[l], zeros], axis=1),
                               jnp.concatenate([zeros, rw_a2[l]], axis=1)], axis=0).astype(BF16)
        y_rw = _rwkv(p, B, S, row(rw_mu[l]), row(rw_w0[l]), row(rw_a0[l]), wa2, rw_g2[l].astype(BF16),
                     row(rw_k_k[l]), row(rw_k_a[l]), row(rw_r_k[l]), row(rw_ln_w[l]), row(rw_ln_b[l]))
        y_sb = _stick_breaking(p, B, S, row(jnp.tile(sb_q_norm[l], 2)), row(jnp.tile(sb_k_norm[l], 2)))
        y_ret = _retention(p, B, S, row(ret_gn[l]), tables)
        dw = jnp.concatenate([conv_dw[l], jnp.zeros((CONV_HALO - CONV_WIDTH, WIDTH), F32)], axis=0)
        y_conv = _conformer_conv(p, B, S, dw, row(conv_b[l]), row(conv_ln_w[l]), row(conv_ln_b[l]))
        x2 = _merge(x2, row(norm_mix[l]), (y_rw, y_sb, y_ret, y_conv),
                    w_gate[l].astype(BF16), w_branch[l].astype(BF16), w_out[l].astype(BF16))
        if l % 2 == 0:
            x2 = _ffn(x2, row(norm_ffn[l]), ffn_w1[l // 2].astype(BF16), ffn_w3[l // 2].astype(BF16),
                      ffn_w2[l // 2].astype(BF16))
        else:
            rt = jnp.concatenate([router[l // 2], jnp.zeros((D, LANES - N_EXPERTS), F32)], axis=1)
            x2 = _moe(x2, row(norm_ffn[l]), rt, moe_w1[l // 2].astype(BF16), moe_w3[l // 2].astype(BF16),
                      moe_w2[l // 2].astype(BF16))
    return x2.reshape(B, S, D)
```

```python
import functools

import jax
import jax.numpy as jnp
from jax import lax
from jax.experimental import pallas as pl
from jax.experimental.pallas import tpu as pltpu

F32 = jnp.float32
BF16 = jnp.bfloat16

LANES = 128
SUBLANES = 8
VMEM_LIMIT = 56 * 1024 * 1024

D_MODEL = 1024
HEAD_DIM = 64
N_HEADS = 4
WIDTH = N_HEADS * HEAD_DIM
NORM_EPS = 1e-6
LOG2_E = 1.4426950408889634
RW_DECAY_SCALE = 0.606531
RW_LN_EPS = 64e-5
RW_CHUNK = 64
RET_CHUNK = 128
RET_GN_EPS = 1e-5
ROPE_BASE = 10000.0
CONV_WIDTH = 31
CONV_HALO = 32
CONV_LN_EPS = 1e-5
N_EXPERTS = 8
N_IN = 3328
COL_RW = 0
COL_SB_Q, COL_SB_K, COL_SB_V = 8, 10, 12
COL_RET_Q, COL_RET_K, COL_RET_V, COL_RET_G = 14, 16, 18, 20
COL_CONV_A, COL_CONV_B = 11, 12


def _nn(a, b):
    return lax.dot_general(a, b, (((1,), (0,)), ((), ())), preferred_element_type=F32)


def _nt(a, b):
    return lax.dot_general(a, b, (((1,), (1,)), ((), ())), preferred_element_type=F32)


def _mm(a, b):
    return _nn(a.astype(BF16), b.astype(BF16))


def _split(x):
    hi = x.astype(BF16)
    lo = (x - hi.astype(F32)).astype(BF16)
    return hi, lo


def _nn_rhs_exact(x, m):
    hi, lo = _split(x)
    return _nn(hi, m) + _nn(lo, m)


def _nn_lhs_exact(m, x):
    hi, lo = _split(x)
    return _nn(m, hi) + _nn(m, lo)


def _nn_3pass(a, b):
    ah, al = _split(a)
    bh, bl = _split(b)
    return _nn(ah, bh) + (_nn(ah, bl) + _nn(al, bh))


def _iota(shape, axis):
    return lax.broadcasted_iota(jnp.int32, shape, axis)


def _head_block_matrix(n, value):
    same = (_iota((n, n), 0) >> 6) == (_iota((n, n), 1) >> 6)
    return jnp.where(same, value, 0.0).astype(BF16)


def _sigmoid(x):
    return 0.5 * jnp.tanh(0.5 * x) + 0.5


def _rmsnorm_rows(x, gain):
    return x * lax.rsqrt(jnp.mean(x * x, axis=-1, keepdims=True) + NORM_EPS) * gain


def _params(*sem):
    return pltpu.CompilerParams(dimension_semantics=sem, vmem_limit_bytes=VMEM_LIMIT)


def _inproj_kernel(x_ref, g_ref, w_ref, p_ref):
    h = _rmsnorm_rows(x_ref[...], g_ref[...])
    p_ref[...] = _nn(h.astype(BF16), w_ref[...])


def _inproj(x2, gain, w_bf16, tm=512):
    T, D = x2.shape
    N = w_bf16.shape[1]
    return pl.pallas_call(
        _inproj_kernel,
        grid=(T // tm,),
        in_specs=[pl.BlockSpec((tm, D), lambda i: (i, 0)),
                  pl.BlockSpec((1, D), lambda i: (0, 0)),
                  pl.BlockSpec((D, N), lambda i: (0, 0))],
        out_specs=pl.BlockSpec((tm, N), lambda i: (i, 0)),
        out_shape=jax.ShapeDtypeStruct((T, N), F32),
        compiler_params=_params("parallel"),
        name="inproj",
    )(x2, gain, w_bf16)


def _rw_kernel(p_ref, mu_ref, w0_ref, a0_ref, wa2_ref, g2_ref, kk_ref, ka_ref, rk_ref, lnw_ref, lnb_ref,
               o_ref, prev_ref, zt_ref, at_s, bt_s, kt_s, rt_s, tw_s, mrb_s, v_s, ecl_s, x_s, mv_s, y_s, bonus_s, g_s,
               *, ts, nbatch, group):
    C = RW_CHUNK
    W = WIDTH
    i = pl.program_id(0)

    @pl.when(i == 0)
    def _():
        prev_ref[...] = jnp.zeros_like(prev_ref)
        zt_ref[...] = jnp.zeros_like(zt_ref)

    lane = _iota((1, LANES), 1)
    row = _iota((ts, 1), 0)
    r4 = _iota((W, W), 0)
    c4 = _iota((W, W), 1)
    same_head = (r4 >> 6) == (c4 >> 6)
    tri4 = jnp.where(same_head & (c4 <= r4), 1.0, 0.0).astype(BF16)
    ones_bd = _head_block_matrix(W, 1.0)

    for b in range(nbatch):
        p = p_ref[b]
        shifted = jnp.where(row == 0, prev_ref[b, 0:1, :], pltpu.roll(p, 1, 0))
        prev_ref[b, 0:1, :] = p_ref[b, ts - 1:ts, :]
        pm = p + (shifted - p) * mu_ref[...]
        r = pm[:, 0:W]
        k = pm[:, W:2 * W]
        v = pm[:, 2 * W:3 * W]
        lora = pm[:, 3 * W:3 * W + LANES]
        pg = pm[:, 3 * W + LANES:]
        wa = _mm(jnp.where(lane < 64, jnp.tanh(lora), lora), wa2_ref[...])
        lw = -RW_DECAY_SCALE * _sigmoid(w0_ref[...] + wa[:, 0:W])
        a = _sigmoid(a0_ref[...] + wa[:, W:])
        g_s[b] = _mm(_sigmoid(pg), g2_ref[...])
        kk = k * kk_ref[...]
        kk = kk * jnp.minimum(lax.rsqrt(_nn_rhs_exact(kk * kk, ones_bd)), 1e12)
        kp = k * (1.0 + (a - 1.0) * ka_ref[...])
        bonus_s[b] = _nn_rhs_exact(r * kp * rk_ref[...], ones_bd) * v
        cl = jnp.concatenate([_nn_lhs_exact(tri4, lw[n * W:(n + 1) * W, :]) for n in range(ts // W)], axis=0)
        ecl = jnp.exp(cl)
        eml = jnp.exp(-cl)
        at_s[b] = (-kk * jnp.exp(cl - lw)).astype(BF16)
        bt_s[b] = (kk * a * eml).astype(BF16)
        kt_s[b] = (kp * eml).astype(BF16)
        rt_s[b] = (r * ecl).astype(BF16)
        v_s[b] = v
        ecl_s[b] = ecl

    head = _iota((1, W), 1) >> 6
    tw = _iota((C, W), 0)
    sw = _iota((C, W), 1) & (C - 1)
    strict = tw > sw
    incl = tw >= sw
    eye = jnp.where(tw == sw, 1.0, 0.0)

    def stack4(x):
        return jnp.concatenate([jnp.where(head == h, x, jnp.zeros_like(x)) for h in range(N_HEADS)], axis=0)

    def block_diag(x):
        return jnp.concatenate([x.astype(BF16)] * N_HEADS, axis=0) * ones_bd

    chunks = [(b, slice(c * C, (c + 1) * C)) for c in range(ts // C) for b in range(nbatch)]
    for g0 in range(0, len(chunks), group):
        grp = chunks[g0:g0 + group]
        n = range(len(grp))
        at = [at_s[b, rows, :] for b, rows in grp]
        rt = [rt_s[b, rows, :] for b, rows in grp]
        b4 = [stack4(bt_s[b, rows, :]) for b, rows in grp]
        k4 = [stack4(kt_s[b, rows, :]) for b, rows in grp]
        v4 = [stack4(v_s[b, rows, :].astype(BF16)) for b, rows in grp]
        lab = [jnp.where(strict, _nt(at[j], b4[j]), 0.0) for j in n]
        lak = [jnp.where(strict, _nt(at[j], k4[j]), 0.0) for j in n]
        mrb = [jnp.where(incl, _nt(rt[j], b4[j]), 0.0) for j in n]
        mrk = [jnp.where(incl, _nt(rt[j], k4[j]), 0.0) for j in n]
        tinv = [eye + lab[j] for j in n]
        m = lab
        m_bd = [block_diag(m[j]) for j in n]
        for _ in range(5):
            m = [_nn(m[j].astype(BF16), m_bd[j]) for j in n]
            m_bd = [block_diag(m[j]) for j in n]
            tinv = [tinv[j] + _nn(tinv[j].astype(BF16), m_bd[j]) for j in n]
        for j, (b, rows) in enumerate(grp):
            tw_s[b, rows, :] = tinv[j].astype(BF16)
            mrb_s[b, rows, :] = mrb[j].astype(BF16)
            x_s[b, rows, :] = _nn(lak[j].astype(BF16), v4[j])
            mv_s[b, rows, :] = _nn(mrk[j].astype(BF16), v4[j])

    def chunk(c, carry):
        rows = pl.ds(pl.multiple_of(c * C, C), C)
        nb = range(nbatch)
        zt = [zt_ref[b] for b in nb]
        ztb = [zt[b].astype(BF16) for b in nb]
        az = [_nt(at_s[b, rows, :], ztb[b]) + x_s[b, rows, :] for b in nb]
        u = [_nn(tw_s[b, rows, :], stack4(az[b]).astype(BF16)) for b in nb]
        uvt = [jnp.concatenate([u[b], v_s[b, rows, :]], axis=0).T.astype(BF16) for b in nb]
        bk = [jnp.concatenate([bt_s[b, rows, :], kt_s[b, rows, :]], axis=0) for b in nb]
        for b in nb:
            g_end = ecl_s[b, pl.ds(c * C + C - 1, 1), :]
            zt_ref[b] = (zt[b] + jnp.where(same_head, _nn(uvt[b], bk[b]), 0.0)) * g_end
        for b in nb:
            y_s[b, rows, :] = (_nt(rt_s[b, rows, :], ztb[b]) + _nn(mrb_s[b, rows, :], stack4(u[b]).astype(BF16))
                               + mv_s[b, rows, :])
        return carry

    lax.fori_loop(0, ts // C, chunk, 0)

    mean_bd = _head_block_matrix(W, 1.0 / HEAD_DIM)
    for b in range(nbatch):
        y = y_s[b]
        yc = y - _nn_rhs_exact(y, mean_bd)
        var = _nn_rhs_exact(yc * yc, mean_bd)
        y = yc * lax.rsqrt(var + RW_LN_EPS) * lnw_ref[...] + lnb_ref[...]
        o_ref[b] = (y + bonus_s[b]) * g_s[b]


def _rwkv(p, B, S, mu, w0, a0, wa2, g2, k_k, k_a, r_k, ln_w, ln_b, ts=512, group=8):
    ts = min(ts, S)
    W = WIDTH
    vec = lambda n: pl.BlockSpec((1, n), lambda i: (0, 0))
    full = lambda a: pl.BlockSpec(a.shape, lambda i: (0, 0))
    out = pl.pallas_call(
        functools.partial(_rw_kernel, ts=ts, nbatch=B, group=group),
        grid=(S // ts,),
        in_specs=[pl.BlockSpec((B, ts, 4 * W), lambda i: (0, i, COL_RW)),
                  vec(4 * W), vec(W), vec(W), full(wa2), full(g2), vec(W), vec(W), vec(W), vec(W), vec(W)],
        out_specs=pl.BlockSpec((B, ts, W), lambda i: (0, i, 0)),
        out_shape=jax.ShapeDtypeStruct((B, S, W), F32),
        scratch_shapes=[pltpu.VMEM((B, 8, 4 * W), F32), pltpu.VMEM((B, W, W), F32)]
                       + [pltpu.VMEM((B, ts, W), BF16)] * 6 + [pltpu.VMEM((B, ts, W), F32)] * 7,
        compiler_params=_params("arbitrary"),
        name="rwkv7",
    )(p.reshape(B, S, -1), mu, w0, a0, wa2, g2, k_k, k_a, r_k, ln_w, ln_b)
    return out.reshape(B * S, W)


def _sb_kernel(q_ref, k_ref, v_ref, qg_ref, kg_ref, o_ref, kb, vb, acc_o, acc_l, knorm, *, tq, tk, seq, prep):
    i = pl.program_id(2)
    lane = _iota((1, LANES), 1)
    mean_bd = _head_block_matrix(LANES, 1.0 / HEAD_DIM)
    ones = jnp.ones((LANES, LANES), BF16)

    @pl.when(i == 0)
    def _():
        knorm[...] = jnp.zeros_like(knorm)

        def body(c, carry):
            rows = pl.ds(pl.multiple_of(c * prep, prep), prep)
            kf = k_ref[rows, :]
            ms = _nn_rhs_exact(kf * kf, mean_bd)
            kn = (kf * lax.rsqrt(ms + NORM_EPS) * kg_ref[...]).astype(BF16)
            kb[rows, :] = kn
            vb[rows, :] = v_ref[rows, :].astype(BF16)
            kn = kn.astype(F32)
            sq = _nn_rhs_exact(kn * kn, ones)
            knorm[...] = jnp.maximum(knorm[...], jnp.max(sq.reshape(prep // 8, 8, LANES), axis=0))
            return carry
        lax.fori_loop(0, seq // prep, body, 0)

    q = q_ref[...]
    ms = _nn_rhs_exact(q * q, mean_bd)
    qn = q * lax.rsqrt(ms + NORM_EPS) * qg_ref[...] * (HEAD_DIM ** -0.5 * LOG2_E)
    q2 = jnp.concatenate([jnp.where(lane < 64, qn, 0.0), jnp.where(lane >= 64, qn, 0.0)], axis=0).astype(BF16)

    q2f = q2.astype(F32)
    qsq = jnp.max(_nn_rhs_exact(q2f * q2f, ones))
    zmax = jnp.sqrt(qsq * jnp.max(knorm[...])) * 1.01
    stop_at = 152.0 + zmax * (1.0 / 256.0)

    def softplus2(z2):
        return jnp.maximum(z2, 0.0) + jnp.log2(1.0 + jnp.exp2(-jnp.abs(z2)))

    def rev_incl(n):
        return jnp.where(_iota((n, n), 0) >= _iota((n, n), 1), 1.0, 0.0).astype(BF16)

    prev = pl.ds(pl.multiple_of(jnp.maximum(i - 1, 0) * tk, tk), tk)
    diag = pl.ds(pl.multiple_of(i * tk, tk), tk)
    z2 = _nt(q2, jnp.concatenate([kb[prev, :], kb[diag, :]], axis=0))
    kpos = (i - 1) * tk + _iota((2 * tq, 2 * tk), 1)
    qpos = i * tq + (_iota((2 * tq, 2 * tk), 0) & (tq - 1))
    valid = (kpos < qpos) & (kpos >= 0)
    cum = _nn(jnp.where(valid, softplus2(z2), 0.0).astype(BF16), rev_incl(2 * tk))
    attn = jnp.where(valid, jnp.exp2(z2 - cum), 0.0)
    acc_o[...] = _nn(attn.astype(BF16), jnp.concatenate([vb[prev, :], vb[diag, :]], axis=0))
    seen = jnp.broadcast_to(cum[:, 0:1], (2 * tq, LANES))
    acc_l[...] = seen
    rev = rev_incl(tk)

    def step(j):
        rows = pl.ds(pl.multiple_of(j * tk, tk), tk)
        z2 = _nt(q2, kb[rows, :])
        cum = _nn(softplus2(z2).astype(BF16), rev)
        seen = acc_l[...]
        attn = jnp.exp2(z2 - cum - jnp.concatenate([seen] * (tk // LANES), axis=1))
        acc_o[...] += _nn(attn.astype(BF16), vb[rows, :])
        seen = seen + cum[:, 0:1]
        acc_l[...] = seen
        return jnp.min(seen)

    def cond(carry):
        t, low = carry
        return (t < i) & (low < stop_at)

    def body(carry):
        t, _ = carry
        return t + 1, step(i - 1 - t)
    lax.while_loop(cond, body, (1, jnp.min(seen)))

    o_ref[...] = jnp.where(lane < 64, acc_o[0:tq, :], acc_o[tq:2 * tq, :])


def _stick_breaking(p, B, S, qg, kg, tq=256, tk=256):
    assert tq == tk and S % tq == 0
    nq = S // tq
    T = B * S
    prep = min(512, S)
    return pl.pallas_call(
        functools.partial(_sb_kernel, tq=tq, tk=tk, seq=S, prep=prep),
        grid=(B, 2, nq),
        in_specs=[pl.BlockSpec((tq, LANES), lambda b, h, i: (b * nq + i, COL_SB_Q + h)),
                  pl.BlockSpec((S, LANES), lambda b, h, i: (b, COL_SB_K + h)),
                  pl.BlockSpec((S, LANES), lambda b, h, i: (b, COL_SB_V + h)),
                  pl.BlockSpec((1, LANES), lambda b, h, i: (0, 0)),
                  pl.BlockSpec((1, LANES), lambda b, h, i: (0, 0))],
        out_specs=pl.BlockSpec((tq, LANES), lambda b, h, i: (b * nq + i, h)),
        out_shape=jax.ShapeDtypeStruct((T, WIDTH), F32),
        scratch_shapes=[pltpu.VMEM((S, LANES), BF16), pltpu.VMEM((S, LANES), BF16),
                        pltpu.VMEM((2 * tq, LANES), F32), pltpu.VMEM((2 * tq, LANES), F32),
                        pltpu.VMEM((8, LANES), F32)],
        compiler_params=_params("arbitrary", "arbitrary", "arbitrary"),
        name="stick_breaking",
    )(p, p, p, qg, kg)


def _ret_kernel(q_ref, k_ref, v_ref, gt_ref, cos_ref, sin_ref, gn_ref, dm_ref, kdec_ref, qdec_ref, cdec_ref,
                o_ref, st_ref, *, ts):
    C = RET_CHUNK
    nch = ts // C
    pairs = range(2)
    chunks = range(nch)
    i = pl.program_id(1)

    @pl.when(i == 0)
    def _():
        st_ref[...] = jnp.zeros_like(st_ref)

    lane = _iota((1, LANES), 1)
    first_half = (lane & 63) < 32
    same_head = (_iota((LANES, LANES), 0) >> 6) == (_iota((LANES, LANES), 1) >> 6)
    mean_bd = _head_block_matrix(LANES, 1.0 / HEAD_DIM)
    cos = cos_ref[...]
    sin = sin_ref[...]

    def rotary(t):
        swapped = jnp.where(first_half, pltpu.roll(t, LANES - 32, 1), pltpu.roll(t, 32, 1))
        return t * cos + swapped * sin

    def rows(t, c):
        return t[c * C:(c + 1) * C]

    cols = [slice(h * LANES, (h + 1) * LANES) for h in pairs]
    q = [rotary(q_ref[:, cols[h]]) for h in pairs]
    k = [rotary(k_ref[:, cols[h]]) * (HEAD_DIM ** -0.5) for h in pairs]
    vb = [v_ref[:, cols[h]].astype(BF16) for h in pairs]
    kb = [k[h].astype(BF16) for h in pairs]
    q_lo = [jnp.where(lane < 64, q[h], 0.0).astype(BF16) for h in pairs]
    q_hi = [jnp.where(lane >= 64, q[h], 0.0).astype(BF16) for h in pairs]
    qd = [(q[h] * jnp.concatenate([qdec_ref[h]] * nch, axis=0)).astype(BF16) for h in pairs]
    kd = [k[h] * jnp.concatenate([kdec_ref[h]] * nch, axis=0) for h in pairs]

    hc = [(h, c) for h in pairs for c in chunks]
    scores = {(h, c): _nt(jnp.concatenate([rows(q_lo[h], c), rows(q_hi[h], c)], axis=0), rows(kb[h], c)) * dm_ref[h]
              for h, c in hc}
    kv = {(h, c): _nn(rows(kd[h], c).T.astype(BF16), rows(vb[h], c)) for h, c in hc}
    intra2 = {(h, c): _nn(scores[h, c].astype(BF16), rows(vb[h], c)) for h, c in hc}
    state = {}
    for h in pairs:
        st = st_ref[h]
        for c in chunks:
            state[h, c] = st
            st = cdec_ref[h] * st + jnp.where(same_head, kv[h, c], 0.0)
        st_ref[h] = st
    inter = {(h, c): _nn(rows(qd[h], c), state[h, c].astype(BF16)) for h, c in hc}
    for h in pairs:
        o = jnp.concatenate([jnp.where(lane < 64, intra2[h, c][0:C], intra2[h, c][C:2 * C]) + inter[h, c]
                             for c in chunks], axis=0)
        oc = o - _nn_rhs_exact(o, mean_bd)
        var = _nn_rhs_exact(oc * oc, mean_bd)
        on = oc * lax.rsqrt(var + RET_GN_EPS) * gn_ref[:, cols[h]]
        gt = gt_ref[:, cols[h]]
        o_ref[:, cols[h]] = on * (gt * _sigmoid(gt))


def _retention_tables(S):
    C = RET_CHUNK
    inv_freq = ROPE_BASE ** (-jnp.arange(0, HEAD_DIM, 2, dtype=F32) / HEAD_DIM)
    ang = jnp.arange(S, dtype=F32)[:, None] * inv_freq[None, :]
    cos = jnp.tile(jnp.cos(ang), (1, 4))
    sin = jnp.tile(jnp.concatenate([-jnp.sin(ang), jnp.sin(ang)], axis=1), (1, 2))
    log_gamma = jnp.log(1.0 - 2.0 ** (-5.0 - jnp.arange(N_HEADS, dtype=F32)))
    idx = jnp.arange(C, dtype=F32)
    rel = idx[:, None] - idx[None, :]
    intra = jnp.where(rel >= 0, jnp.exp(jnp.maximum(rel, 0.0) * log_gamma[:, None, None]), 0.0)
    dm = intra.reshape(2, 2 * C, C)
    lanes = lambda t: jnp.repeat(t.reshape(2, 2, C).transpose(0, 2, 1), HEAD_DIM, axis=2)
    kdec = lanes(jnp.exp((C - 1 - idx)[None, :] * log_gamma[:, None]))
    qdec = lanes(jnp.exp((idx + 1.0)[None, :] * log_gamma[:, None]))
    cdec = jnp.repeat(jnp.exp(C * log_gamma).reshape(2, 1, 2), HEAD_DIM, axis=2)
    return cos, sin, dm, kdec, qdec, cdec


def _retention(p, B, S, gn, tables, ts=512):
    ts = min(ts, S)
    nb = S // ts
    T = B * S
    C = RET_CHUNK
    cos, sin, dm, kdec, qdec, cdec = tables
    col = lambda c0: pl.BlockSpec((ts, WIDTH), lambda b, i: (b * nb + i, c0 // 2))
    tab = pl.BlockSpec((ts, LANES), lambda b, i: (i, 0))
    full = lambda a: pl.BlockSpec(a.shape, lambda b, i: (0,) * a.ndim)
    return pl.pallas_call(
        functools.partial(_ret_kernel, ts=ts),
        grid=(B, nb),
        in_specs=[col(COL_RET_Q), col(COL_RET_K), col(COL_RET_V), col(COL_RET_G), tab, tab,
                  full(gn), full(dm), full(kdec), full(qdec), full(cdec)],
        out_specs=pl.BlockSpec((ts, WIDTH), lambda b, i: (b * nb + i, 0)),
        out_shape=jax.ShapeDtypeStruct((T, WIDTH), F32),
        scratch_shapes=[pltpu.VMEM((2, LANES, LANES), F32)],
        compiler_params=_params("arbitrary", "arbitrary"),
        name="retention",
    )(p, p, p, p, cos, sin, gn, dm, kdec, qdec, cdec)


def _conv_kernel(ua_ref, ub_ref, dw_ref, db_ref, lnw_ref, lnb_ref, o_ref, buf, shifted, *, ts):
    i = pl.program_id(1)
    span = ts + CONV_HALO - SUBLANES

    @pl.when(i == 0)
    def _():
        buf[0:CONV_HALO, :] = jnp.zeros((CONV_HALO, WIDTH), F32)

    buf[CONV_HALO:, :] = ua_ref[...] * _sigmoid(ub_ref[...])
    for s in range(1, SUBLANES):
        shifted[s] = buf[s:s + span, :]
    acc = jnp.zeros((ts, WIDTH), F32) + db_ref[...]
    for j in range(CONV_WIDTH):
        start = CONV_HALO - (CONV_WIDTH - 1) + j
        s = start % SUBLANES
        window = buf[start:start + ts, :] if s == 0 else shifted[s, start - s:start - s + ts, :]
        acc = acc + dw_ref[j:j + 1, :] * window
    buf[0:CONV_HALO, :] = buf[ts:ts + CONV_HALO, :]
    xc = acc - jnp.mean(acc, axis=-1, keepdims=True)
    var = jnp.mean(xc * xc, axis=-1, keepdims=True)
    y = xc * lax.rsqrt(var + CONV_LN_EPS) * lnw_ref[...] + lnb_ref[...]
    o_ref[...] = y * _sigmoid(y)


def _conformer_conv(p, B, S, dw, db, ln_w, ln_b, ts=512):
    ts = min(ts, S)
    nb = S // ts
    T = B * S
    vec = pl.BlockSpec((1, WIDTH), lambda b, i: (0, 0))
    return pl.pallas_call(
        functools.partial(_conv_kernel, ts=ts),
        grid=(B, nb),
        in_specs=[pl.BlockSpec((ts, WIDTH), lambda b, i: (b * nb + i, COL_CONV_A)),
                  pl.BlockSpec((ts, WIDTH), lambda b, i: (b * nb + i, COL_CONV_B)),
                  pl.BlockSpec((CONV_HALO, WIDTH), lambda b, i: (0, 0)), vec, vec, vec],
        out_specs=pl.BlockSpec((ts, WIDTH), lambda b, i: (b * nb + i, 0)),
        out_shape=jax.ShapeDtypeStruct((T, WIDTH), F32),
        scratch_shapes=[pltpu.VMEM((ts + CONV_HALO, WIDTH), F32),
                        pltpu.VMEM((SUBLANES, ts + CONV_HALO - SUBLANES, WIDTH), F32)],
        compiler_params=_params("arbitrary", "arbitrary"),
        name="conformer_conv",
    )(p, p, dw, db, ln_w, ln_b)


def _merge_kernel(x_ref, g_ref, y0_ref, y1_ref, y2_ref, y3_ref, wg_ref, wb_ref, wo_ref, o_ref):
    x = x_ref[...]
    h = _rmsnorm_rows(x, g_ref[...]).astype(BF16)
    merged = None
    for n, y_ref in enumerate((y0_ref, y1_ref, y2_ref, y3_ref)):
        term = _sigmoid(_nn(h, wg_ref[n])) * _nn(y_ref[...].astype(BF16), wb_ref[n])
        merged = term if merged is None else merged + term
    o_ref[...] = x + _nn(merged.astype(BF16), wo_ref[...])


def _merge(x2, gain, ys, wg, wb, wo, tm=512):
    T, D = x2.shape
    tm = min(tm, T)
    row = lambda n: pl.BlockSpec((tm, n), lambda i: (i, 0))
    return pl.pallas_call(
        _merge_kernel,
        grid=(T // tm,),
        in_specs=[row(D), pl.BlockSpec((1, D), lambda i: (0, 0)), row(WIDTH), row(WIDTH), row(WIDTH), row(WIDTH),
                  pl.BlockSpec(wg.shape, lambda i: (0, 0, 0)),
                  pl.BlockSpec(wb.shape, lambda i: (0, 0, 0)),
                  pl.BlockSpec(wo.shape, lambda i: (0, 0))],
        out_specs=row(D),
        out_shape=jax.ShapeDtypeStruct((T, D), F32),
        compiler_params=_params("parallel"),
        name="merge",
    )(x2, gain, *ys, wg, wb, wo)


def _ffn_kernel(x_ref, g_ref, w1_ref, w3_ref, w2_ref, o_ref, h_s):
    f = pl.program_id(1)

    @pl.when(f == 0)
    def _():
        x = x_ref[...]
        h_s[...] = _rmsnorm_rows(x, g_ref[...]).astype(BF16)
        o_ref[...] = x

    h = h_s[...]
    a = _nn(h, w1_ref[...])
    b = _nn(h, w3_ref[...])
    o_ref[...] += _nn((a * _sigmoid(a) * b).astype(BF16), w2_ref[...])


def _ffn(x2, gain, w1, w3, w2, tm=1024, tf=1408):
    T, D = x2.shape
    tm = min(tm, T)
    nf = w1.shape[1] // tf
    return pl.pallas_call(
        _ffn_kernel,
        grid=(T // tm, nf),
        in_specs=[pl.BlockSpec((tm, D), lambda i, f: (i, 0)),
                  pl.BlockSpec((1, D), lambda i, f: (0, 0)),
                  pl.BlockSpec((D, tf), lambda i, f: (0, f)),
                  pl.BlockSpec((D, tf), lambda i, f: (0, f)),
                  pl.BlockSpec((tf, D), lambda i, f: (f, 0))],
        out_specs=pl.BlockSpec((tm, D), lambda i, f: (i, 0)),
        out_shape=jax.ShapeDtypeStruct((T, D), F32),
        scratch_shapes=[pltpu.VMEM((tm, D), BF16)],
        compiler_params=_params("parallel", "arbitrary"),
        name="ffn",
    )(x2, gain, w1, w3, w2)


def _moe_kernel(x_ref, g_ref, rt_ref, w1_ref, w3_ref, w2_ref, o_ref, h_s, gate_s, rank_s, rank_t_s, count_s,
                *, tm, cap):
    e = pl.program_id(1)
    lane = _iota((1, LANES), 1)

    @pl.when(e == 0)
    def _():
        x = x_ref[...]
        h = _rmsnorm_rows(x, g_ref[...])
        h_s[...] = h.astype(BF16)
        o_ref[...] = x
        logits = jnp.where(lane < N_EXPERTS, _nn_3pass(h, rt_ref[...]), -jnp.inf)
        m1 = jnp.max(logits, axis=-1, keepdims=True)
        i1 = jnp.min(jnp.where(logits == m1, lane, LANES), axis=-1, keepdims=True)
        rest = jnp.where(lane == i1, -jnp.inf, logits)
        m2 = jnp.max(rest, axis=-1, keepdims=True)
        i2 = jnp.min(jnp.where(rest == m2, lane, LANES), axis=-1, keepdims=True)
        e2 = jnp.exp(m2 - m1)
        den = 1.0 + e2
        gate_s[...] = jnp.where(lane == i1, 1.0 / den, 0.0) + jnp.where(lane == i2, e2 / den, 0.0)
        chosen = jnp.where((lane == i1) | (lane == i2), 1.0, 0.0)
        before = jnp.where(_iota((tm, tm), 1) < _iota((tm, tm), 0), 1.0, 0.0).astype(BF16)
        rank = jnp.where(chosen > 0.0, _nn(before, chosen.astype(BF16)), -1.0)
        rank_s[...] = rank
        rank_t_s[...] = rank.T
        count_s[...] = jnp.sum(chosen, axis=0, keepdims=True)

    mine = lane == e
    gate = jnp.sum(jnp.where(mine, gate_s[...], 0.0), axis=-1, keepdims=True)
    rank_col = jnp.sum(jnp.where(mine, rank_s[...], 0.0), axis=-1, keepdims=True)
    rank_row = rank_t_s[pl.ds(e, 1), :]
    count = jnp.sum(jnp.where(mine, count_s[...], 0.0))
    cap_lanes = -(-cap // LANES) * LANES
    slot_rows = _iota((cap, tm), 0).astype(F32)
    slot_cols = _iota((tm, cap_lanes), 1).astype(F32)

    def one_pass(first):
        take = jnp.where(rank_row - first == slot_rows, 1.0, 0.0).astype(BF16)
        xg = _nn(take, h_s[...]).astype(BF16)
        a = _nn(xg, w1_ref[0])
        b = _nn(xg, w3_ref[0])
        y = _nn((a * _sigmoid(a) * b).astype(BF16), w2_ref[0]).astype(BF16)
        if cap_lanes > cap:
            y = jnp.concatenate([y, jnp.zeros((cap_lanes - cap, y.shape[1]), BF16)], axis=0)
        put = jnp.where((rank_col - first == slot_cols) & (slot_cols < cap), 1.0, 0.0).astype(BF16)
        o_ref[...] += gate * _nn(put, y)
        return first + cap

    lax.while_loop(lambda first: first < count, one_pass, jnp.float32(0.0))


def _moe(x2, gain, router_pad, w1, w3, w2, tm=1024, cap=320):
    T, D = x2.shape
    tm = min(tm, T)
    cap = min(cap, tm)
    E, _, F = w1.shape
    return pl.pallas_call(
        functools.partial(_moe_kernel, tm=tm, cap=cap),
        grid=(T // tm, E),
        in_specs=[pl.BlockSpec((tm, D), lambda i, e: (i, 0)),
                  pl.BlockSpec((1, D), lambda i, e: (0, 0)),
                  pl.BlockSpec((D, LANES), lambda i, e: (0, 0)),
                  pl.BlockSpec((1, D, F), lambda i, e: (e, 0, 0)),
                  pl.BlockSpec((1, D, F), lambda i, e: (e, 0, 0)),
                  pl.BlockSpec((1, F, D), lambda i, e: (e, 0, 0))],
        out_specs=pl.BlockSpec((tm, D), lambda i, e: (i, 0)),
        out_shape=jax.ShapeDtypeStruct((T, D), F32),
        scratch_shapes=[pltpu.VMEM((tm, D), BF16), pltpu.VMEM((tm, LANES), F32), pltpu.VMEM((tm, LANES), F32),
                        pltpu.VMEM((LANES, tm), F32), pltpu.VMEM((1, LANES), F32)],
        compiler_params=_params("parallel", "arbitrary"),
        name="moe",
    )(x2, gain, router_pad, w1, w3, w2)


def kernel(x, norm_mix, w_in, rw_mu, rw_w0, rw_w2, rw_a0, rw_a2, rw_g2, rw_k_k, rw_k_a, rw_r_k, rw_ln_w, rw_ln_b, sb_q_norm, sb_k_norm, ret_gn, conv_dw, conv_b, conv_ln_w, conv_ln_b, w_gate, w_branch, w_out, norm_ffn, ffn_w1, ffn_w3, ffn_w2, router, moe_w1, moe_w3, moe_w2):
    B, S, D = x.shape
    depth = norm_mix.shape[0]
    x2 = x.reshape(B * S, D)
    tables = _retention_tables(S)
    row = lambda t: t.reshape(1, -1)
    for l in range(depth):
        p = _inproj(x2, row(norm_mix[l]), w_in[l].astype(BF16))
        zeros = jnp.zeros_like(rw_w2[l])
        wa2 = jnp.concatenate([jnp.concatenate([rw_w2[l], zeros], axis=1),
                               jnp.concatenate([zeros, rw_a2[l]], axis=1)], axis=0).astype(BF16)
        y_rw = _rwkv(p, B, S, row(rw_mu[l]), row(rw_w0[l]), row(rw_a0[l]), wa2, rw_g2[l].astype(BF16),
                     row(rw_k_k[l]), row(rw_k_a[l]), row(rw_r_k[l]), row(rw_ln_w[l]), row(rw_ln_b[l]))
        y_sb = _stick_breaking(p, B, S, row(jnp.tile(sb_q_norm[l], 2)), row(jnp.tile(sb_k_norm[l], 2)))
        y_ret = _retention(p, B, S, row(ret_gn[l]), tables)
        dw = jnp.concatenate([conv_dw[l], jnp.zeros((CONV_HALO - CONV_WIDTH, WIDTH), F32)], axis=0)
        y_conv = _conformer_conv(p, B, S, dw, row(conv_b[l]), row(conv_ln_w[l]), row(conv_ln_b[l]))
        x2 = _merge(x2, row(norm_mix[l]), (y_rw, y_sb, y_ret, y_conv),
                    w_gate[l].astype(BF16), w_branch[l].astype(BF16), w_out[l].astype(BF16))
        if l % 2 == 0:
            x2 = _ffn(x2, row(norm_ffn[l]), ffn_w1[l // 2].astype(BF16), ffn_w3[l // 2].astype(BF16),
                      ffn_w2[l // 2].astype(BF16))
        else:
            rt = jnp.concatenate([router[l // 2], jnp.zeros((D, LANES - N_EXPERTS), F32)], axis=1)
            x2 = _moe(x2, row(norm_ffn[l]), rt, moe_w1[l // 2].astype(BF16), moe_w3[l // 2].astype(BF16),
                      moe_w2[l // 2].astype(BF16))
    return x2.reshape(B, S, D)
```

```python
import functools

import jax
import jax.numpy as jnp
from jax import lax
from jax.experimental import pallas as pl
from jax.experimental.pallas import tpu as pltpu

F32 = jnp.float32
BF16 = jnp.bfloat16

LANES = 128
SUBLANES = 8
VMEM_LIMIT = 56 * 1024 * 1024

D_MODEL = 1024
HEAD_DIM = 64
N_HEADS = 4
WIDTH = N_HEADS * HEAD_DIM
NORM_EPS = 1e-6
LOG2_E = 1.4426950408889634
RW_DECAY_SCALE = 0.606531
RW_LN_EPS = 64e-5
RW_CHUNK = 64
RET_CHUNK = 128
RET_GN_EPS = 1e-5
ROPE_BASE = 10000.0
CONV_WIDTH = 31
CONV_HALO = 32
CONV_LN_EPS = 1e-5
N_EXPERTS = 8
N_IN = 3328
COL_RW = 0
COL_SB_Q, COL_SB_K, COL_SB_V = 8, 10, 12
COL_RET_Q, COL_RET_K, COL_RET_V, COL_RET_G = 14, 16, 18, 20
COL_CONV_A, COL_CONV_B = 11, 12


def _nn(a, b):
    return lax.dot_general(a, b, (((1,), (0,)), ((), ())), preferred_element_type=F32)


def _nt(a, b):
    return lax.dot_general(a, b, (((1,), (1,)), ((), ())), preferred_element_type=F32)


def _mm(a, b):
    return _nn(a.astype(BF16), b.astype(BF16))


def _split(x):
    hi = x.astype(BF16)
    lo = (x - hi.astype(F32)).astype(BF16)
    return hi, lo


def _nn_rhs_exact(x, m):
    hi, lo = _split(x)
    return _nn(hi, m) + _nn(lo, m)


def _nn_lhs_exact(m, x):
    hi, lo = _split(x)
    return _nn(m, hi) + _nn(m, lo)


def _nn_3pass(a, b):
    ah, al = _split(a)
    bh, bl = _split(b)
    return _nn(ah, bh) + (_nn(ah, bl) + _nn(al, bh))


def _iota(shape, axis):
    return lax.broadcasted_iota(jnp.int32, shape, axis)


def _head_block_matrix(n, value):
    same = (_iota((n, n), 0) >> 6) == (_iota((n, n), 1) >> 6)
    return jnp.where(same, value, 0.0).astype(BF16)


def _sigmoid(x):
    return 0.5 * jnp.tanh(0.5 * x) + 0.5


def _rmsnorm_rows(x, gain):
    return x * lax.rsqrt(jnp.mean(x * x, axis=-1, keepdims=True) + NORM_EPS) * gain


def _params(*sem):
    return pltpu.CompilerParams(dimension_semantics=sem, vmem_limit_bytes=VMEM_LIMIT)


def _inproj_kernel(x_ref, g_ref, w_ref, p_ref):
    h = _rmsnorm_rows(x_ref[...], g_ref[...])
    p_ref[...] = _nn(h.astype(BF16), w_ref[...])


def _inproj(x2, gain, w_bf16, tm=512):
    T, D = x2.shape
    N = w_bf16.shape[1]
    return pl.pallas_call(
        _inproj_kernel,
        grid=(T // tm,),
        in_specs=[pl.BlockSpec((tm, D), lambda i: (i, 0)),
                  pl.BlockSpec((1, D), lambda i: (0, 0)),
                  pl.BlockSpec((D, N), lambda i: (0, 0))],
        out_specs=pl.BlockSpec((tm, N), lambda i: (i, 0)),
        out_shape=jax.ShapeDtypeStruct((T, N), F32),
        compiler_params=_params("parallel"),
        name="inproj",
    )(x2, gain, w_bf16)


def _rw_kernel(p_ref, mu_ref, w0_ref, a0_ref, wa2_ref, g2_ref, kk_ref, ka_ref, rk_ref, lnw_ref, lnb_ref,
               o_ref, prev_ref, zt_ref, at_s, bt_s, kt_s, rt_s, qh_s, gmat_s, v_s, ecl_s, y0_s, y_s, bonus_s, g_s, hmat_s,
               *, ts, nbatch, group):
    C = RW_CHUNK
    W = WIDTH
    i = pl.program_id(0)

    @pl.when(i == 0)
    def _():
        prev_ref[...] = jnp.zeros_like(prev_ref)
        zt_ref[...] = jnp.zeros_like(zt_ref)

    lane = _iota((1, LANES), 1)
    row = _iota((ts, 1), 0)
    r4 = _iota((W, W), 0)
    c4 = _iota((W, W), 1)
    same_head = (r4 >> 6) == (c4 >> 6)
    tri4 = jnp.where(same_head & (c4 <= r4), 1.0, 0.0).astype(BF16)
    ones_bd = _head_block_matrix(W, 1.0)

    for b in range(nbatch):
        p = p_ref[b]
        shifted = jnp.where(row == 0, prev_ref[b, 0:1, :], pltpu.roll(p, 1, 0))
        prev_ref[b, 0:1, :] = p_ref[b, ts - 1:ts, :]
        pm = p + (shifted - p) * mu_ref[...]
        r = pm[:, 0:W]
        k = pm[:, W:2 * W]
        v = pm[:, 2 * W:3 * W]
        lora = pm[:, 3 * W:3 * W + LANES]
        pg = pm[:, 3 * W + LANES:]
        wa = _mm(jnp.where(lane < 64, jnp.tanh(lora), lora), wa2_ref[...])
        lw = -RW_DECAY_SCALE * _sigmoid(w0_ref[...] + wa[:, 0:W])
        a = _sigmoid(a0_ref[...] + wa[:, W:])
        g_s[b] = _mm(_sigmoid(pg), g2_ref[...])
        kk = k * kk_ref[...]
        kk = kk * jnp.minimum(lax.rsqrt(_nn_rhs_exact(kk * kk, ones_bd)), 1e12)
        kp = k * (1.0 + (a - 1.0) * ka_ref[...])
        bonus_s[b] = _nn_rhs_exact(r * kp * rk_ref[...], ones_bd) * v
        cl = jnp.concatenate([_nn_lhs_exact(tri4, lw[n * W:(n + 1) * W, :]) for n in range(ts // W)], axis=0)
        ecl = jnp.exp(cl)
        eml = jnp.exp(-cl)
        at_s[b] = (-kk * jnp.exp(cl - lw)).astype(BF16)
        bt_s[b] = (kk * a * eml).astype(BF16)
        kt_s[b] = (kp * eml).astype(BF16)
        rt_s[b] = (r * ecl).astype(BF16)
        v_s[b] = v
        ecl_s[b] = ecl

    head = _iota((1, W), 1) >> 6
    tw = _iota((C, W), 0)
    sw = _iota((C, W), 1) & (C - 1)
    strict = tw > sw
    incl = tw >= sw
    eye = jnp.where(tw == sw, 1.0, 0.0)

    def stack4(x):
        return jnp.concatenate([jnp.where(head == h, x, jnp.zeros_like(x)) for h in range(N_HEADS)], axis=0)

    def block_diag(x):
        return jnp.concatenate([x.astype(BF16)] * N_HEADS, axis=0) * ones_bd

    chunks = [(b, slice(c * C, (c + 1) * C)) for c in range(ts // C) for b in range(nbatch)]
    for g0 in range(0, len(chunks), group):
        grp = chunks[g0:g0 + group]
        n = range(len(grp))
        at = [at_s[b, rows, :] for b, rows in grp]
        rt = [rt_s[b, rows, :] for b, rows in grp]
        b4 = [stack4(bt_s[b, rows, :]) for b, rows in grp]
        k4 = [stack4(kt_s[b, rows, :]) for b, rows in grp]
        v4 = [stack4(v_s[b, rows, :].astype(BF16)) for b, rows in grp]
        lab = [jnp.where(strict, _nt(at[j], b4[j]), 0.0) for j in n]
        lak = [jnp.where(strict, _nt(at[j], k4[j]), 0.0) for j in n]
        mrb = [jnp.where(incl, _nt(rt[j], b4[j]), 0.0) for j in n]
        mrk = [jnp.where(incl, _nt(rt[j], k4[j]), 0.0) for j in n]
        tinv = [eye + lab[j] for j in n]
        m = lab
        m_bd = [block_diag(m[j]) for j in n]
        for _ in range(5):
            m = [_nn(m[j].astype(BF16), m_bd[j]) for j in n]
            m_bd = [block_diag(m[j]) for j in n]
            tinv = [tinv[j] + _nn(tinv[j].astype(BF16), m_bd[j]) for j in n]
        tb = [tinv[j].astype(BF16) for j in n]
        mb = [mrb[j].astype(BF16) for j in n]
        w = [_nn(tb[j], stack4(at[j])) for j in n]
        u0 = [_nn(tb[j], stack4(_nn(lak[j].astype(BF16), v4[j])).astype(BF16)) for j in n]
        qh = [rt[j].astype(F32) + _nn(mb[j], stack4(w[j]).astype(BF16)) for j in n]
        y0 = [_nn(mb[j], stack4(u0[j]).astype(BF16)) + _nn(mrk[j].astype(BF16), v4[j]) for j in n]
        pad = jnp.zeros((C, W), F32)
        for j, (b, rows) in enumerate(grp):
            c = rows.start // C
            wt = jnp.concatenate([w[j], pad], axis=0).T.astype(BF16)
            uvt = jnp.concatenate([u0[j], v_s[b, rows, :]], axis=0).T.astype(BF16)
            bt = bt_s[b, rows, :]
            gmat = _nn(wt, jnp.concatenate([bt, jnp.zeros_like(bt)], axis=0))
            hmat = _nn(uvt, jnp.concatenate([bt, kt_s[b, rows, :]], axis=0))
            gmat_s[b, c] = jnp.where(same_head, gmat, 0.0).astype(BF16)
            hmat_s[b, c] = jnp.where(same_head, hmat, 0.0)
            qh_s[b, rows, :] = qh[j].astype(BF16)
            y0_s[b, rows, :] = y0[j]

    def chunk(c, carry):
        rows = pl.ds(pl.multiple_of(c * C, C), C)
        for b in range(nbatch):
            zt = zt_ref[b]
            ztb = zt.astype(BF16)
            y_s[b, rows, :] = _nt(qh_s[b, rows, :], ztb) + y0_s[b, rows, :]
            g_end = ecl_s[b, pl.ds(c * C + C - 1, 1), :]
            zt_ref[b] = (zt + _nn(ztb, gmat_s[b, c]) + hmat_s[b, c]) * g_end
        return carry

    lax.fori_loop(0, ts // C, chunk, 0)

    mean_bd = _head_block_matrix(W, 1.0 / HEAD_DIM)
    for b in range(nbatch):
        y = y_s[b]
        yc = y - _nn_rhs_exact(y, mean_bd)
        var = _nn_rhs_exact(yc * yc, mean_bd)
        y = yc * lax.rsqrt(var + RW_LN_EPS) * lnw_ref[...] + lnb_ref[...]
        o_ref[b] = (y + bonus_s[b]) * g_s[b]


def _rwkv(p, B, S, mu, w0, a0, wa2, g2, k_k, k_a, r_k, ln_w, ln_b, ts=512, group=8):
    ts = min(ts, S)
    W = WIDTH
    vec = lambda n: pl.BlockSpec((1, n), lambda i: (0, 0))
    full = lambda a: pl.BlockSpec(a.shape, lambda i: (0, 0))
    out = pl.pallas_call(
        functools.partial(_rw_kernel, ts=ts, nbatch=B, group=group),
        grid=(S // ts,),
        in_specs=[pl.BlockSpec((B, ts, 4 * W), lambda i: (0, i, COL_RW)),
                  vec(4 * W), vec(W), vec(W), full(wa2), full(g2), vec(W), vec(W), vec(W), vec(W), vec(W)],
        out_specs=pl.BlockSpec((B, ts, W), lambda i: (0, i, 0)),
        out_shape=jax.ShapeDtypeStruct((B, S, W), F32),
        scratch_shapes=[pltpu.VMEM((B, 8, 4 * W), F32), pltpu.VMEM((B, W, W), F32)]
                       + [pltpu.VMEM((B, ts, W), BF16)] * 5 + [pltpu.VMEM((B, ts // RW_CHUNK, W, W), BF16)]
                       + [pltpu.VMEM((B, ts, W), F32)] * 6 + [pltpu.VMEM((B, ts // RW_CHUNK, W, W), F32)],
        compiler_params=_params("arbitrary"),
        name="rwkv7",
    )(p.reshape(B, S, -1), mu, w0, a0, wa2, g2, k_k, k_a, r_k, ln_w, ln_b)
    return out.reshape(B * S, W)


def _sb_kernel(q_ref, k_ref, v_ref, qg_ref, kg_ref, o_ref, kb, vb, acc_o, acc_l, knorm,
               *, tq, tk, seq, prep, nsub):
    i = pl.program_id(2)
    lane = _iota((1, LANES), 1)
    mean_bd = _head_block_matrix(LANES, 1.0 / HEAD_DIM)
    ones = jnp.ones((LANES, LANES), BF16)

    @pl.when(i == 0)
    def _():
        knorm[...] = jnp.zeros_like(knorm)

        def body(c, carry):
            rows = pl.ds(pl.multiple_of(c * prep, prep), prep)
            kf = k_ref[rows, :]
            ms = _nn_rhs_exact(kf * kf, mean_bd)
            kn = (kf * lax.rsqrt(ms + NORM_EPS) * kg_ref[...]).astype(BF16)
            kb[rows, :] = kn
            vb[rows, :] = v_ref[rows, :].astype(BF16)
            kn = kn.astype(F32)
            sq = _nn_rhs_exact(kn * kn, ones)
            knorm[...] = jnp.maximum(knorm[...], jnp.max(sq.reshape(prep // 8, 8, LANES), axis=0))
            return carry
        lax.fori_loop(0, seq // prep, body, 0)

    subs = range(nsub)
    tile = [i * nsub + s for s in subs]
    q = q_ref[...]
    ms = _nn_rhs_exact(q * q, mean_bd)
    qn = q * lax.rsqrt(ms + NORM_EPS) * qg_ref[...] * (HEAD_DIM ** -0.5 * LOG2_E)
    q_lo = jnp.where(lane < 64, qn, 0.0).astype(BF16)
    q_hi = jnp.where(lane >= 64, qn, 0.0).astype(BF16)
    q2 = [jnp.concatenate([q_lo[s * tq:(s + 1) * tq], q_hi[s * tq:(s + 1) * tq]], axis=0) for s in subs]

    qf = (q_lo + q_hi).astype(F32)
    qsq = jnp.max(_nn_rhs_exact(qf * qf, ones))
    zmax = jnp.sqrt(qsq * jnp.max(knorm[...])) * 1.01
    stop_at = 152.0 + zmax * (1.0 / 256.0)

    def softplus2(z2):
        return jnp.maximum(z2, 0.0) + jnp.log2(1.0 + jnp.exp2(-jnp.abs(z2)))

    def rev_incl(n):
        return jnp.where(_iota((n, n), 0) >= _iota((n, n), 1), 1.0, 0.0).astype(BF16)

    prev = [pl.ds(pl.multiple_of(jnp.maximum(tile[s] - 1, 0) * tk, tk), tk) for s in subs]
    diag = [pl.ds(pl.multiple_of(tile[s] * tk, tk), tk) for s in subs]
    col = _iota((2 * tq, 2 * tk), 1)
    causal = (col - tk) < (_iota((2 * tq, 2 * tk), 0) & (tq - 1))
    valid = [causal & (col >= jnp.where(tile[s] > 0, 0, tk)) for s in subs]
    rev2 = rev_incl(2 * tk)
    z2 = [_nt(q2[s], jnp.concatenate([kb[prev[s], :], kb[diag[s], :]], axis=0)) for s in subs]
    cum = [_nn(jnp.where(valid[s], softplus2(z2[s]), 0.0).astype(BF16), rev2) for s in subs]
    attn = [jnp.where(valid[s], jnp.exp2(z2[s] - cum[s]), 0.0).astype(BF16) for s in subs]
    for s in subs:
        acc_o[s] = _nn(attn[s], jnp.concatenate([vb[prev[s], :], vb[diag[s], :]], axis=0))
        acc_l[s] = jnp.broadcast_to(cum[s][:, 0:1], (2 * tq, LANES))
    rev = rev_incl(tk)

    for s in subs:
        def step(j, s=s):
            rows = pl.ds(pl.multiple_of(j * tk, tk), tk)
            z2 = _nt(q2[s], kb[rows, :])
            cum = _nn(softplus2(z2).astype(BF16), rev)
            seen = acc_l[s]
            attn = jnp.exp2(z2 - cum - jnp.concatenate([seen] * (tk // LANES), axis=1))
            acc_o[s] += _nn(attn.astype(BF16), vb[rows, :])
            seen = seen + cum[:, 0:1]
            acc_l[s] = seen
            return jnp.min(seen)

        def cond(carry, s=s):
            t, low = carry
            return (t < tile[s]) & (low < stop_at)

        def body(carry, s=s, step=step):
            t, _ = carry
            return t + 1, step(tile[s] - 1 - t)
        lax.while_loop(cond, body, (1, jnp.min(cum[s][:, 0:1])))

    for s in subs:
        o_ref[s * tq:(s + 1) * tq, :] = jnp.where(lane < 64, acc_o[s, 0:tq, :], acc_o[s, tq:2 * tq, :])


def _stick_breaking(p, B, S, qg, kg, tq=256, tk=256, nsub=2):
    assert tq == tk and S % (nsub * tq) == 0
    nq = S // (nsub * tq)
    T = B * S
    prep = min(512, S)
    return pl.pallas_call(
        functools.partial(_sb_kernel, tq=tq, tk=tk, seq=S, prep=prep, nsub=nsub),
        grid=(B, 2, nq),
        in_specs=[pl.BlockSpec((nsub * tq, LANES), lambda b, h, i: (b * nq + i, COL_SB_Q + h)),
                  pl.BlockSpec((S, LANES), lambda b, h, i: (b, COL_SB_K + h)),
                  pl.BlockSpec((S, LANES), lambda b, h, i: (b, COL_SB_V + h)),
                  pl.BlockSpec((1, LANES), lambda b, h, i: (0, 0)),
                  pl.BlockSpec((1, LANES), lambda b, h, i: (0, 0))],
        out_specs=pl.BlockSpec((nsub * tq, LANES), lambda b, h, i: (b * nq + i, h)),
        out_shape=jax.ShapeDtypeStruct((T, WIDTH), F32),
        scratch_shapes=[pltpu.VMEM((S, LANES), BF16), pltpu.VMEM((S, LANES), BF16),
                        pltpu.VMEM((nsub, 2 * tq, LANES), F32), pltpu.VMEM((nsub, 2 * tq, LANES), F32),
                        pltpu.VMEM((8, LANES), F32)],
        compiler_params=_params("arbitrary", "arbitrary", "arbitrary"),
        name="stick_breaking",
    )(p, p, p, qg, kg)


def _ret_kernel(q_ref, k_ref, v_ref, gt_ref, cos_ref, sin_ref, gn_ref, dm_ref, kdec_ref, qdec_ref, cdec_ref,
                o_ref, st_ref, *, ts):
    C = RET_CHUNK
    nch = ts // C
    pairs = range(2)
    chunks = range(nch)
    i = pl.program_id(1)

    @pl.when(i == 0)
    def _():
        st_ref[...] = jnp.zeros_like(st_ref)

    lane = _iota((1, LANES), 1)
    first_half = (lane & 63) < 32
    same_head = (_iota((LANES, LANES), 0) >> 6) == (_iota((LANES, LANES), 1) >> 6)
    mean_bd = _head_block_matrix(LANES, 1.0 / HEAD_DIM)
    cos = cos_ref[...]
    sin = sin_ref[...]

    def rotary(t):
        swapped = jnp.where(first_half, pltpu.roll(t, LANES - 32, 1), pltpu.roll(t, 32, 1))
        return t * cos + swapped * sin

    def rows(t, c):
        return t[c * C:(c + 1) * C]

    cols = [slice(h * LANES, (h + 1) * LANES) for h in pairs]
    q = [rotary(q_ref[:, cols[h]]) for h in pairs]
    k = [rotary(k_ref[:, cols[h]]) * (HEAD_DIM ** -0.5) for h in pairs]
    vb = [v_ref[:, cols[h]].astype(BF16) for h in pairs]
    kb = [k[h].astype(BF16) for h in pairs]
    q_lo = [jnp.where(lane < 64, q[h], 0.0).astype(BF16) for h in pairs]
    q_hi = [jnp.where(lane >= 64, q[h], 0.0).astype(BF16) for h in pairs]
    qd = [(q[h] * jnp.concatenate([qdec_ref[h]] * nch, axis=0)).astype(BF16) for h in pairs]
    kd = [k[h] * jnp.concatenate([kdec_ref[h]] * nch, axis=0) for h in pairs]

    hc = [(h, c) for h in pairs for c in chunks]
    scores = {(h, c): _nt(jnp.concatenate([rows(q_lo[h], c), rows(q_hi[h], c)], axis=0), rows(kb[h], c)) * dm_ref[h]
              for h, c in hc}
    kv = {(h, c): _nn(rows(kd[h], c).T.astype(BF16), rows(vb[h], c)) for h, c in hc}
    intra2 = {(h, c): _nn(scores[h, c].astype(BF16), rows(vb[h], c)) for h, c in hc}
    state = {}
    for h in pairs:
        st = st_ref[h]
        for c in chunks:
            state[h, c] = st
            st = cdec_ref[h] * st + jnp.where(same_head, kv[h, c], 0.0)
        st_ref[h] = st
    inter = {(h, c): _nn(rows(qd[h], c), state[h, c].astype(BF16)) for h, c in hc}
    for h in pairs:
        o = jnp.concatenate([jnp.where(lane < 64, intra2[h, c][0:C], intra2[h, c][C:2 * C]) + inter[h, c]
                             for c in chunks], axis=0)
        oc = o - _nn_rhs_exact(o, mean_bd)
        var = _nn_rhs_exact(oc * oc, mean_bd)
        on = oc * lax.rsqrt(var + RET_GN_EPS) * gn_ref[:, cols[h]]
        gt = gt_ref[:, cols[h]]
        o_ref[:, cols[h]] = on * (gt * _sigmoid(gt))


def _retention_tables(S):
    C = RET_CHUNK
    inv_freq = ROPE_BASE ** (-jnp.arange(0, HEAD_DIM, 2, dtype=F32) / HEAD_DIM)
    ang = jnp.arange(S, dtype=F32)[:, None] * inv_freq[None, :]
    cos = jnp.tile(jnp.cos(ang), (1, 4))
    sin = jnp.tile(jnp.concatenate([-jnp.sin(ang), jnp.sin(ang)], axis=1), (1, 2))
    log_gamma = jnp.log(1.0 - 2.0 ** (-5.0 - jnp.arange(N_HEADS, dtype=F32)))
    idx = jnp.arange(C, dtype=F32)
    rel = idx[:, None] - idx[None, :]
    intra = jnp.where(rel >= 0, jnp.exp(jnp.maximum(rel, 0.0) * log_gamma[:, None, None]), 0.0)
    dm = intra.reshape(2, 2 * C, C)
    lanes = lambda t: jnp.repeat(t.reshape(2, 2, C).transpose(0, 2, 1), HEAD_DIM, axis=2)
    kdec = lanes(jnp.exp((C - 1 - idx)[None, :] * log_gamma[:, None]))
    qdec = lanes(jnp.exp((idx + 1.0)[None, :] * log_gamma[:, None]))
    cdec = jnp.repeat(jnp.exp(C * log_gamma).reshape(2, 1, 2), HEAD_DIM, axis=2)
    return cos, sin, dm, kdec, qdec, cdec


def _retention(p, B, S, gn, tables, ts=512):
    ts = min(ts, S)
    nb = S // ts
    T = B * S
    C = RET_CHUNK
    cos, sin, dm, kdec, qdec, cdec = tables
    col = lambda c0: pl.BlockSpec((ts, WIDTH), lambda b, i: (b * nb + i, c0 // 2))
    tab = pl.BlockSpec((ts, LANES), lambda b, i: (i, 0))
    full = lambda a: pl.BlockSpec(a.shape, lambda b, i: (0,) * a.ndim)
    return pl.pallas_call(
        functools.partial(_ret_kernel, ts=ts),
        grid=(B, nb),
        in_specs=[col(COL_RET_Q), col(COL_RET_K), col(COL_RET_V), col(COL_RET_G), tab, tab,
                  full(gn), full(dm), full(kdec), full(qdec), full(cdec)],
        out_specs=pl.BlockSpec((ts, WIDTH), lambda b, i: (b * nb + i, 0)),
        out_shape=jax.ShapeDtypeStruct((T, WIDTH), F32),
        scratch_shapes=[pltpu.VMEM((2, LANES, LANES), F32)],
        compiler_params=_params("arbitrary", "arbitrary"),
        name="retention",
    )(p, p, p, p, cos, sin, gn, dm, kdec, qdec, cdec)


def _conv_kernel(ua_ref, ub_ref, dw_ref, db_ref, lnw_ref, lnb_ref, o_ref, buf, shifted, *, ts):
    i = pl.program_id(1)
    span = ts + CONV_HALO - SUBLANES

    @pl.when(i == 0)
    def _():
        buf[0:CONV_HALO, :] = jnp.zeros((CONV_HALO, WIDTH), F32)

    buf[CONV_HALO:, :] = ua_ref[...] * _sigmoid(ub_ref[...])
    for s in range(1, SUBLANES):
        shifted[s] = buf[s:s + span, :]
    acc = jnp.zeros((ts, WIDTH), F32) + db_ref[...]
    for j in range(CONV_WIDTH):
        start = CONV_HALO - (CONV_WIDTH - 1) + j
        s = start % SUBLANES
        window = buf[start:start + ts, :] if s == 0 else shifted[s, start - s:start - s + ts, :]
        acc = acc + dw_ref[j:j + 1, :] * window
    buf[0:CONV_HALO, :] = buf[ts:ts + CONV_HALO, :]
    xc = acc - jnp.mean(acc, axis=-1, keepdims=True)
    var = jnp.mean(xc * xc, axis=-1, keepdims=True)
    y = xc * lax.rsqrt(var + CONV_LN_EPS) * lnw_ref[...] + lnb_ref[...]
    o_ref[...] = y * _sigmoid(y)


def _conformer_conv(p, B, S, dw, db, ln_w, ln_b, ts=512):
    ts = min(ts, S)
    nb = S // ts
    T = B * S
    vec = pl.BlockSpec((1, WIDTH), lambda b, i: (0, 0))
    return pl.pallas_call(
        functools.partial(_conv_kernel, ts=ts),
        grid=(B, nb),
        in_specs=[pl.BlockSpec((ts, WIDTH), lambda b, i: (b * nb + i, COL_CONV_A)),
                  pl.BlockSpec((ts, WIDTH), lambda b, i: (b * nb + i, COL_CONV_B)),
                  pl.BlockSpec((CONV_HALO, WIDTH), lambda b, i: (0, 0)), vec, vec, vec],
        out_specs=pl.BlockSpec((ts, WIDTH), lambda b, i: (b * nb + i, 0)),
        out_shape=jax.ShapeDtypeStruct((T, WIDTH), F32),
        scratch_shapes=[pltpu.VMEM((ts + CONV_HALO, WIDTH), F32),
                        pltpu.VMEM((SUBLANES, ts + CONV_HALO - SUBLANES, WIDTH), F32)],
        compiler_params=_params("arbitrary", "arbitrary"),
        name="conformer_conv",
    )(p, p, dw, db, ln_w, ln_b)


def _merge_kernel(x_ref, g_ref, y0_ref, y1_ref, y2_ref, y3_ref, wg_ref, wb_ref, wo_ref, o_ref):
    x = x_ref[...]
    h = _rmsnorm_rows(x, g_ref[...]).astype(BF16)
    merged = None
    for n, y_ref in enumerate((y0_ref, y1_ref, y2_ref, y3_ref)):
        term = _sigmoid(_nn(h, wg_ref[n])) * _nn(y_ref[...].astype(BF16), wb_ref[n])
        merged = term if merged is None else merged + term
    o_ref[...] = x + _nn(merged.astype(BF16), wo_ref[...])


def _merge(x2, gain, ys, wg, wb, wo, tm=512):
    T, D = x2.shape
    tm = min(tm, T)
    row = lambda n: pl.BlockSpec((tm, n), lambda i: (i, 0))
    return pl.pallas_call(
        _merge_kernel,
        grid=(T // tm,),
        in_specs=[row(D), pl.BlockSpec((1, D), lambda i: (0, 0)), row(WIDTH), row(WIDTH), row(WIDTH), row(WIDTH),
                  pl.BlockSpec(wg.shape, lambda i: (0, 0, 0)),
                  pl.BlockSpec(wb.shape, lambda i: (0, 0, 0)),
                  pl.BlockSpec(wo.shape, lambda i: (0, 0))],
        out_specs=row(D),
        out_shape=jax.ShapeDtypeStruct((T, D), F32),
        compiler_params=_params("parallel"),
        name="merge",
    )(x2, gain, *ys, wg, wb, wo)


def _ffn_kernel(x_ref, g_ref, w1_ref, w3_ref, w2_ref, o_ref, h_s):
    f = pl.program_id(1)

    @pl.when(f == 0)
    def _():
        x = x_ref[...]
        h_s[...] = _rmsnorm_rows(x, g_ref[...]).astype(BF16)
        o_ref[...] = x

    h = h_s[...]
    a = _nn(h, w1_ref[...])
    b = _nn(h, w3_ref[...])
    o_ref[...] += _nn((a * _sigmoid(a) * b).astype(BF16), w2_ref[...])


def _ffn(x2, gain, w1, w3, w2, tm=1024, tf=1408):
    T, D = x2.shape
    tm = min(tm, T)
    nf = w1.shape[1] // tf
    return pl.pallas_call(
        _ffn_kernel,
        grid=(T // tm, nf),
        in_specs=[pl.BlockSpec((tm, D), lambda i, f: (i, 0)),
                  pl.BlockSpec((1, D), lambda i, f: (0, 0)),
                  pl.BlockSpec((D, tf), lambda i, f: (0, f)),
                  pl.BlockSpec((D, tf), lambda i, f: (0, f)),
                  pl.BlockSpec((tf, D), lambda i, f: (f, 0))],
        out_specs=pl.BlockSpec((tm, D), lambda i, f: (i, 0)),
        out_shape=jax.ShapeDtypeStruct((T, D), F32),
        scratch_shapes=[pltpu.VMEM((tm, D), BF16)],
        compiler_params=_params("parallel", "arbitrary"),
        name="ffn",
    )(x2, gain, w1, w3, w2)


def _moe_kernel(x_ref, g_ref, rt_ref, w1_ref, w3_ref, w2_ref, o_ref, h_s, gate_s, rank_s, rank_t_s, count_s,
                *, tm, cap):
    e = pl.program_id(1)
    lane = _iota((1, LANES), 1)

    @pl.when(e == 0)
    def _():
        x = x_ref[...]
        h = _rmsnorm_rows(x, g_ref[...])
        h_s[...] = h.astype(BF16)
        o_ref[...] = x
        logits = jnp.where(lane < N_EXPERTS, _nn_3pass(h, rt_ref[...]), -jnp.inf)
        m1 = jnp.max(logits, axis=-1, keepdims=True)
        i1 = jnp.min(jnp.where(logits == m1, lane, LANES), axis=-1, keepdims=True)
        rest = jnp.where(lane == i1, -jnp.inf, logits)
        m2 = jnp.max(rest, axis=-1, keepdims=True)
        i2 = jnp.min(jnp.where(rest == m2, lane, LANES), axis=-1, keepdims=True)
        e2 = jnp.exp(m2 - m1)
        den = 1.0 + e2
        gate_s[...] = jnp.where(lane == i1, 1.0 / den, 0.0) + jnp.where(lane == i2, e2 / den, 0.0)
        chosen = jnp.where((lane == i1) | (lane == i2), 1.0, 0.0)
        before = jnp.where(_iota((tm, tm), 1) < _iota((tm, tm), 0), 1.0, 0.0).astype(BF16)
        rank = jnp.where(chosen > 0.0, _nn(before, chosen.astype(BF16)), -1.0)
        rank_s[...] = rank
        rank_t_s[...] = rank.T
        count_s[...] = jnp.sum(chosen, axis=0, keepdims=True)

    mine = lane == e
    gate = jnp.sum(jnp.where(mine, gate_s[...], 0.0), axis=-1, keepdims=True)
    rank_col = jnp.sum(jnp.where(mine, rank_s[...], 0.0), axis=-1, keepdims=True)
    rank_row = rank_t_s[pl.ds(e, 1), :]
    count = jnp.sum(jnp.where(mine, count_s[...], 0.0))
    cap_lanes = -(-cap // LANES) * LANES
    slot_rows = _iota((cap, tm), 0).astype(F32)
    slot_cols = _iota((tm, cap_lanes), 1).astype(F32)

    def one_pass(first):
        take = jnp.where(rank_row - first == slot_rows, 1.0, 0.0).astype(BF16)
        xg = _nn(take, h_s[...]).astype(BF16)
        a = _nn(xg, w1_ref[0])
        b = _nn(xg, w3_ref[0])
        y = _nn((a * _sigmoid(a) * b).astype(BF16), w2_ref[0]).astype(BF16)
        if cap_lanes > cap:
            y = jnp.concatenate([y, jnp.zeros((cap_lanes - cap, y.shape[1]), BF16)], axis=0)
        put = jnp.where((rank_col - first == slot_cols) & (slot_cols < cap), 1.0, 0.0).astype(BF16)
        o_ref[...] += gate * _nn(put, y)
        return first + cap

    lax.while_loop(lambda first: first < count, one_pass, jnp.float32(0.0))


def _moe(x2, gain, router_pad, w1, w3, w2, tm=1024, cap=320):
    T, D = x2.shape
    tm = min(tm, T)
    cap = min(cap, tm)
    E, _, F = w1.shape
    return pl.pallas_call(
        functools.partial(_moe_kernel, tm=tm, cap=cap),
        grid=(T // tm, E),
        in_specs=[pl.BlockSpec((tm, D), lambda i, e: (i, 0)),
                  pl.BlockSpec((1, D), lambda i, e: (0, 0)),
                  pl.BlockSpec((D, LANES), lambda i, e: (0, 0)),
                  pl.BlockSpec((1, D, F), lambda i, e: (e, 0, 0)),
                  pl.BlockSpec((1, D, F), lambda i, e: (e, 0, 0)),
                  pl.BlockSpec((1, F, D), lambda i, e: (e, 0, 0))],
        out_specs=pl.BlockSpec((tm, D), lambda i, e: (i, 0)),
        out_shape=jax.ShapeDtypeStruct((T, D), F32),
        scratch_shapes=[pltpu.VMEM((tm, D), BF16), pltpu.VMEM((tm, LANES), F32), pltpu.VMEM((tm, LANES), F32),
                        pltpu.VMEM((LANES, tm), F32), pltpu.VMEM((1, LANES), F32)],
        compiler_params=_params("parallel", "arbitrary"),
        name="moe",
    )(x2, gain, router_pad, w1, w3, w2)


def kernel(x, norm_mix, w_in, rw_mu, rw_w0, rw_w2, rw_a0, rw_a2, rw_g2, rw_k_k, rw_k_a, rw_r_k, rw_ln_w, rw_ln_b, sb_q_norm, sb_k_norm, ret_gn, conv_dw, conv_b, conv_ln_w, conv_ln_b, w_gate, w_branch, w_out, norm_ffn, ffn_w1, ffn_w3, ffn_w2, router, moe_w1, moe_w3, moe_w2):
    B, S, D = x.shape
    depth = norm_mix.shape[0]
    x2 = x.reshape(B * S, D)
    tables = _retention_tables(S)
    row = lambda t: t.reshape(1, -1)
    for l in range(depth):
        p = _inproj(x2, row(norm_mix[l]), w_in[l].astype(BF16))
        zeros = jnp.zeros_like(rw_w2[l])
        wa2 = jnp.concatenate([jnp.concatenate([rw_w2[l], zeros], axis=1),
                               jnp.concatenate([zeros, rw_a2[l]], axis=1)], axis=0).astype(BF16)
        y_rw = _rwkv(p, B, S, row(rw_mu[l]), row(rw_w0[l]), row(rw_a0[l]), wa2, rw_g2[l].astype(BF16),
                     row(rw_k_k[l]), row(rw_k_a[l]), row(rw_r_k[l]), row(rw_ln_w[l]), row(rw_ln_b[l]))
        y_sb = _stick_breaking(p, B, S, row(jnp.tile(sb_q_norm[l], 2)), row(jnp.tile(sb_k_norm[l], 2)))
        y_ret = _retention(p, B, S, row(ret_gn[l]), tables)
        dw = jnp.concatenate([conv_dw[l], jnp.zeros((CONV_HALO - CONV_WIDTH, WIDTH), F32)], axis=0)
        y_conv = _conformer_conv(p, B, S, dw, row(conv_b[l]), row(conv_ln_w[l]), row(conv_ln_b[l]))
        x2 = _merge(x2, row(norm_mix[l]), (y_rw, y_sb, y_ret, y_conv),
                    w_gate[l].astype(BF16), w_branch[l].astype(BF16), w_out[l].astype(BF16))
        if l % 2 == 0:
            x2 = _ffn(x2, row(norm_ffn[l]), ffn_w1[l // 2].astype(BF16), ffn_w3[l // 2].astype(BF16),
                      ffn_w2[l // 2].astype(BF16))
        else:
            rt = jnp.concatenate([router[l // 2], jnp.zeros((D, LANES - N_EXPERTS), F32)], axis=1)
            x2 = _moe(x2, row(norm_ffn[l]), rt, moe_w1[l // 2].astype(BF16), moe_w3[l // 2].astype(BF16),
                      moe_w2[l // 2].astype(BF16))
    return x2.reshape(B, S, D)
```

```python
import functools

import jax
import jax.numpy as jnp
from jax import lax
from jax.experimental import pallas as pl
from jax.experimental.pallas import tpu as pltpu

F32 = jnp.float32
BF16 = jnp.bfloat16

LANES = 128
SUBLANES = 8
VMEM_LIMIT = 56 * 1024 * 1024

D_MODEL = 1024
HEAD_DIM = 64
N_HEADS = 4
WIDTH = N_HEADS * HEAD_DIM
NORM_EPS = 1e-6
LOG2_E = 1.4426950408889634
RW_DECAY_SCALE = 0.606531
RW_LN_EPS = 64e-5
RW_CHUNK = 64
RET_CHUNK = 128
RET_GN_EPS = 1e-5
ROPE_BASE = 10000.0
CONV_WIDTH = 31
CONV_HALO = 32
CONV_LN_EPS = 1e-5
N_EXPERTS = 8
N_IN = 3328
COL_RW = 0
COL_SB_Q, COL_SB_K, COL_SB_V = 8, 10, 12
COL_RET_Q, COL_RET_K, COL_RET_V, COL_RET_G = 14, 16, 18, 20
COL_CONV_A, COL_CONV_B = 11, 12


def _nn(a, b):
    return lax.dot_general(a, b, (((1,), (0,)), ((), ())), preferred_element_type=F32)


def _nt(a, b):
    return lax.dot_general(a, b, (((1,), (1,)), ((), ())), preferred_element_type=F32)


def _mm(a, b):
    return _nn(a.astype(BF16), b.astype(BF16))


def _split(x):
    hi = x.astype(BF16)
    lo = (x - hi.astype(F32)).astype(BF16)
    return hi, lo


def _nn_rhs_exact(x, m):
    hi, lo = _split(x)
    return _nn(hi, m) + _nn(lo, m)


def _nn_lhs_exact(m, x):
    hi, lo = _split(x)
    return _nn(m, hi) + _nn(m, lo)


def _nn_3pass(a, b):
    ah, al = _split(a)
    bh, bl = _split(b)
    return _nn(ah, bh) + (_nn(ah, bl) + _nn(al, bh))


def _iota(shape, axis):
    return lax.broadcasted_iota(jnp.int32, shape, axis)


def _head_block_matrix(n, value):
    same = (_iota((n, n), 0) >> 6) == (_iota((n, n), 1) >> 6)
    return jnp.where(same, value, 0.0).astype(BF16)


def _sigmoid(x):
    return 0.5 * jnp.tanh(0.5 * x) + 0.5


def _rmsnorm_rows(x, gain):
    return x * lax.rsqrt(jnp.mean(x * x, axis=-1, keepdims=True) + NORM_EPS) * gain


def _params(*sem):
    return pltpu.CompilerParams(dimension_semantics=sem, vmem_limit_bytes=VMEM_LIMIT)


def _inproj_kernel(x_ref, g_ref, w_ref, p_ref):
    h = _rmsnorm_rows(x_ref[...], g_ref[...])
    p_ref[...] = _nn(h.astype(BF16), w_ref[...])


def _inproj(x2, gain, w_bf16, tm=512):
    T, D = x2.shape
    N = w_bf16.shape[1]
    return pl.pallas_call(
        _inproj_kernel,
        grid=(T // tm,),
        in_specs=[pl.BlockSpec((tm, D), lambda i: (i, 0)),
                  pl.BlockSpec((1, D), lambda i: (0, 0)),
                  pl.BlockSpec((D, N), lambda i: (0, 0))],
        out_specs=pl.BlockSpec((tm, N), lambda i: (i, 0)),
        out_shape=jax.ShapeDtypeStruct((T, N), F32),
        compiler_params=_params("parallel"),
        name="inproj",
    )(x2, gain, w_bf16)


def _rw_kernel(p_ref, mu_ref, w0_ref, a0_ref, wa2_ref, g2_ref, kk_ref, ka_ref, rk_ref, lnw_ref, lnb_ref,
               o_ref, prev_ref, zt_ref, at_s, bt_s, kt_s, rt_s, qh_s, gmat_s, v_s, ecl_s, y0_s, y_s, bonus_s, g_s, hmat_s,
               *, ts, nbatch, group):
    C = RW_CHUNK
    W = WIDTH
    i = pl.program_id(0)

    @pl.when(i == 0)
    def _():
        prev_ref[...] = jnp.zeros_like(prev_ref)
        zt_ref[...] = jnp.zeros_like(zt_ref)

    lane = _iota((1, LANES), 1)
    row = _iota((ts, 1), 0)
    r4 = _iota((W, W), 0)
    c4 = _iota((W, W), 1)
    same_head = (r4 >> 6) == (c4 >> 6)
    tri4 = jnp.where(same_head & (c4 <= r4), 1.0, 0.0).astype(BF16)
    ones_bd = _head_block_matrix(W, 1.0)

    for b in range(nbatch):
        p = p_ref[b]
        shifted = jnp.where(row == 0, prev_ref[b, 0:1, :], pltpu.roll(p, 1, 0))
        prev_ref[b, 0:1, :] = p_ref[b, ts - 1:ts, :]
        pm = p + (shifted - p) * mu_ref[...]
        r = pm[:, 0:W]
        k = pm[:, W:2 * W]
        v = pm[:, 2 * W:3 * W]
        lora = pm[:, 3 * W:3 * W + LANES]
        pg = pm[:, 3 * W + LANES:]
        wa = _mm(jnp.where(lane < 64, jnp.tanh(lora), lora), wa2_ref[...])
        lw = -RW_DECAY_SCALE * _sigmoid(w0_ref[...] + wa[:, 0:W])
        a = _sigmoid(a0_ref[...] + wa[:, W:])
        g_s[b] = _mm(_sigmoid(pg), g2_ref[...])
        kk = k * kk_ref[...]
        kk = kk * jnp.minimum(lax.rsqrt(_nn_rhs_exact(kk * kk, ones_bd)), 1e12)
        kp = k * (1.0 + (a - 1.0) * ka_ref[...])
        bonus_s[b] = _nn_rhs_exact(r * kp * rk_ref[...], ones_bd) * v
        cl = jnp.concatenate([_nn_lhs_exact(tri4, lw[n * W:(n + 1) * W, :]) for n in range(ts // W)], axis=0)
        ecl = jnp.exp(cl)
        eml = jnp.exp(-cl)
        at_s[b] = (-kk * jnp.exp(cl - lw)).astype(BF16)
        bt_s[b] = (kk * a * eml).astype(BF16)
        kt_s[b] = (kp * eml).astype(BF16)
        rt_s[b] = (r * ecl).astype(BF16)
        v_s[b] = v
        ecl_s[b] = ecl

    head = _iota((1, W), 1) >> 6
    tw = _iota((C, W), 0)
    sw = _iota((C, W), 1) & (C - 1)
    strict = tw > sw
    incl = tw >= sw
    eye = jnp.where(tw == sw, 1.0, 0.0)

    def stack4(x):
        return jnp.concatenate([jnp.where(head == h, x, jnp.zeros_like(x)) for h in range(N_HEADS)], axis=0)

    def block_diag(x):
        return jnp.concatenate([x.astype(BF16)] * N_HEADS, axis=0) * ones_bd

    chunks = [(b, slice(c * C, (c + 1) * C)) for c in range(ts // C) for b in range(nbatch)]
    for g0 in range(0, len(chunks), group):
        grp = chunks[g0:g0 + group]
        n = range(len(grp))
        at = [at_s[b, rows, :] for b, rows in grp]
        rt = [rt_s[b, rows, :] for b, rows in grp]
        b4 = [stack4(bt_s[b, rows, :]) for b, rows in grp]
        k4 = [stack4(kt_s[b, rows, :]) for b, rows in grp]
        v4 = [stack4(v_s[b, rows, :].astype(BF16)) for b, rows in grp]
        lab = [jnp.where(strict, _nt(at[j], b4[j]), 0.0) for j in n]
        lak = [jnp.where(strict, _nt(at[j], k4[j]), 0.0) for j in n]
        mrb = [jnp.where(incl, _nt(rt[j], b4[j]), 0.0) for j in n]
        mrk = [jnp.where(incl, _nt(rt[j], k4[j]), 0.0) for j in n]
        tinv = [eye + lab[j] for j in n]
        m = lab
        m_bd = [block_diag(m[j]) for j in n]
        for _ in range(5):
            m = [_nn(m[j].astype(BF16), m_bd[j]) for j in n]
            m_bd = [block_diag(m[j]) for j in n]
            tinv = [tinv[j] + _nn(tinv[j].astype(BF16), m_bd[j]) for j in n]
        tb = [tinv[j].astype(BF16) for j in n]
        mb = [mrb[j].astype(BF16) for j in n]
        w = [_nn(tb[j], stack4(at[j])) for j in n]
        u0 = [_nn(tb[j], stack4(_nn(lak[j].astype(BF16), v4[j])).astype(BF16)) for j in n]
        qh = [rt[j].astype(F32) + _nn(mb[j], stack4(w[j]).astype(BF16)) for j in n]
        y0 = [_nn(mb[j], stack4(u0[j]).astype(BF16)) + _nn(mrk[j].astype(BF16), v4[j]) for j in n]
        pad = jnp.zeros((C, W), F32)
        for j, (b, rows) in enumerate(grp):
            c = rows.start // C
            wt = jnp.concatenate([w[j], pad], axis=0).T.astype(BF16)
            uvt = jnp.concatenate([u0[j], v_s[b, rows, :]], axis=0).T.astype(BF16)
            bt = bt_s[b, rows, :]
            gmat = _nn(wt, jnp.concatenate([bt, jnp.zeros_like(bt)], axis=0))
            hmat = _nn(uvt, jnp.concatenate([bt, kt_s[b, rows, :]], axis=0))
            gmat_s[b, c] = jnp.where(same_head, gmat, 0.0).astype(BF16)
            hmat_s[b, c] = jnp.where(same_head, hmat, 0.0)
            qh_s[b, rows, :] = qh[j].astype(BF16)
            y0_s[b, rows, :] = y0[j]

    def chunk(c, carry):
        rows = pl.ds(pl.multiple_of(c * C, C), C)
        for b in range(nbatch):
            zt = zt_ref[b]
            ztb = zt.astype(BF16)
            y_s[b, rows, :] = _nt(qh_s[b, rows, :], ztb) + y0_s[b, rows, :]
            g_end = ecl_s[b, pl.ds(c * C + C - 1, 1), :]
            zt_ref[b] = (zt + _nn(ztb, gmat_s[b, c]) + hmat_s[b, c]) * g_end
        return carry

    lax.fori_loop(0, ts // C, chunk, 0)

    mean_bd = _head_block_matrix(W, 1.0 / HEAD_DIM)
    for b in range(nbatch):
        y = y_s[b]
        yc = y - _nn_rhs_exact(y, mean_bd)
        var = _nn_rhs_exact(yc * yc, mean_bd)
        y = yc * lax.rsqrt(var + RW_LN_EPS) * lnw_ref[...] + lnb_ref[...]
        o_ref[b] = (y + bonus_s[b]) * g_s[b]


def _rwkv(p, B, S, mu, w0, a0, wa2, g2, k_k, k_a, r_k, ln_w, ln_b, ts=512, group=8):
    ts = min(ts, S)
    W = WIDTH
    vec = lambda n: pl.BlockSpec((1, n), lambda i: (0, 0))
    full = lambda a: pl.BlockSpec(a.shape, lambda i: (0, 0))
    out = pl.pallas_call(
        functools.partial(_rw_kernel, ts=ts, nbatch=B, group=group),
        grid=(S // ts,),
        in_specs=[pl.BlockSpec((B, ts, 4 * W), lambda i: (0, i, COL_RW)),
                  vec(4 * W), vec(W), vec(W), full(wa2), full(g2), vec(W), vec(W), vec(W), vec(W), vec(W)],
        out_specs=pl.BlockSpec((B, ts, W), lambda i: (0, i, 0)),
        out_shape=jax.ShapeDtypeStruct((B, S, W), F32),
        scratch_shapes=[pltpu.VMEM((B, 8, 4 * W), F32), pltpu.VMEM((B, W, W), F32)]
                       + [pltpu.VMEM((B, ts, W), BF16)] * 5 + [pltpu.VMEM((B, ts // RW_CHUNK, W, W), BF16)]
                       + [pltpu.VMEM((B, ts, W), F32)] * 6 + [pltpu.VMEM((B, ts // RW_CHUNK, W, W), F32)],
        compiler_params=_params("arbitrary"),
        name="rwkv7",
    )(p.reshape(B, S, -1), mu, w0, a0, wa2, g2, k_k, k_a, r_k, ln_w, ln_b)
    return out.reshape(B * S, W)


def _sb_kernel(q_ref, k_ref, v_ref, qg_ref, kg_ref, o_ref, kb, vb, acc_o, acc_l, knorm,
               *, tq, tk, seq, prep, nsub):
    i = pl.program_id(2)
    lane = _iota((1, LANES), 1)
    mean_bd = _head_block_matrix(LANES, 1.0 / HEAD_DIM)
    ones = jnp.ones((LANES, LANES), BF16)

    @pl.when(i == 0)
    def _():
        knorm[...] = jnp.zeros_like(knorm)

        def body(c, carry):
            rows = pl.ds(pl.multiple_of(c * prep, prep), prep)
            kf = k_ref[rows, :]
            ms = _nn_rhs_exact(kf * kf, mean_bd)
            kn = (kf * lax.rsqrt(ms + NORM_EPS) * kg_ref[...]).astype(BF16)
            kb[rows, :] = kn
            vb[rows, :] = v_ref[rows, :].astype(BF16)
            kn = kn.astype(F32)
            sq = _nn_rhs_exact(kn * kn, ones)
            knorm[...] = jnp.maximum(knorm[...], jnp.max(sq.reshape(prep // 8, 8, LANES), axis=0))
            return carry
        lax.fori_loop(0, seq // prep, body, 0)

    subs = range(nsub)
    tile = [i * nsub + s for s in subs]
    q = q_ref[...]
    ms = _nn_rhs_exact(q * q, mean_bd)
    qn = q * lax.rsqrt(ms + NORM_EPS) * qg_ref[...] * (HEAD_DIM ** -0.5 * LOG2_E)
    q_lo = jnp.where(lane < 64, qn, 0.0).astype(BF16)
    q_hi = jnp.where(lane >= 64, qn, 0.0).astype(BF16)
    q2 = [jnp.concatenate([q_lo[s * tq:(s + 1) * tq], q_hi[s * tq:(s + 1) * tq]], axis=0) for s in subs]

    qf = (q_lo + q_hi).astype(F32)
    qsq = jnp.max(_nn_rhs_exact(qf * qf, ones))
    zmax = jnp.sqrt(qsq * jnp.max(knorm[...])) * 1.01
    stop_at = 152.0 + zmax * (1.0 / 256.0)

    def softplus2(z2):
        return jnp.maximum(z2, 0.0) + jnp.log2(1.0 + jnp.exp2(-jnp.abs(z2)))

    def rev_incl(n):
        return jnp.where(_iota((n, n), 0) >= _iota((n, n), 1), 1.0, 0.0).astype(BF16)

    prev = [pl.ds(pl.multiple_of(jnp.maximum(tile[s] - 1, 0) * tk, tk), tk) for s in subs]
    diag = [pl.ds(pl.multiple_of(tile[s] * tk, tk), tk) for s in subs]
    col = _iota((2 * tq, 2 * tk), 1)
    causal = (col - tk) < (_iota((2 * tq, 2 * tk), 0) & (tq - 1))
    valid = [causal & (col >= jnp.where(i > 0, 0, tk))] + [causal] * (nsub - 1)
    rev = rev_incl(tk)
    z2 = [_nt(q2[s], jnp.concatenate([kb[prev[s], :], kb[diag[s], :]], axis=0)) for s in subs]
    sp = [jnp.where(valid[s], softplus2(z2[s]), 0.0).astype(BF16) for s in subs]
    cum_diag = [_nn(sp[s][:, tk:], rev) for s in subs]
    cum = [jnp.concatenate([_nn(sp[s][:, :tk], rev) + cum_diag[s][:, 0:1], cum_diag[s]], axis=1) for s in subs]
    attn = [jnp.where(valid[s], jnp.exp2(z2[s] - cum[s]), 0.0).astype(BF16) for s in subs]
    for s in subs:
        acc_o[s] = _nn(attn[s], jnp.concatenate([vb[prev[s], :], vb[diag[s], :]], axis=0))
        acc_l[s] = jnp.broadcast_to(cum[s][:, 0:1], (2 * tq, LANES))

    for s in subs:
        def step(j, s=s):
            rows = pl.ds(pl.multiple_of(j * tk, tk), tk)
            z2 = _nt(q2[s], kb[rows, :])
            cum = _nn(softplus2(z2).astype(BF16), rev)
            seen = acc_l[s]
            attn = jnp.exp2(z2 - cum - jnp.concatenate([seen] * (tk // LANES), axis=1))
            acc_o[s] += _nn(attn.astype(BF16), vb[rows, :])
            seen = seen + cum[:, 0:1]
            acc_l[s] = seen
            return jnp.min(seen)

        def cond(carry, s=s):
            t, low = carry
            return (t < tile[s]) & (low < stop_at)

        def body(carry, s=s, step=step):
            t, _ = carry
            return t + 1, step(tile[s] - 1 - t)
        lax.while_loop(cond, body, (1, jnp.min(cum[s][:, 0:1])))

    for s in subs:
        o_ref[s * tq:(s + 1) * tq, :] = jnp.where(lane < 64, acc_o[s, 0:tq, :], acc_o[s, tq:2 * tq, :])


def _stick_breaking(p, B, S, qg, kg, tq=256, tk=256, nsub=2):
    assert tq == tk and S % (nsub * tq) == 0
    nq = S // (nsub * tq)
    T = B * S
    prep = min(512, S)
    return pl.pallas_call(
        functools.partial(_sb_kernel, tq=tq, tk=tk, seq=S, prep=prep, nsub=nsub),
        grid=(B, 2, nq),
        in_specs=[pl.BlockSpec((nsub * tq, LANES), lambda b, h, i: (b * nq + i, COL_SB_Q + h)),
                  pl.BlockSpec((S, LANES), lambda b, h, i: (b, COL_SB_K + h)),
                  pl.BlockSpec((S, LANES), lambda b, h, i: (b, COL_SB_V + h)),
                  pl.BlockSpec((1, LANES), lambda b, h, i: (0, 0)),
                  pl.BlockSpec((1, LANES), lambda b, h, i: (0, 0))],
        out_specs=pl.BlockSpec((nsub * tq, LANES), lambda b, h, i: (b * nq + i, h)),
        out_shape=jax.ShapeDtypeStruct((T, WIDTH), F32),
        scratch_shapes=[pltpu.VMEM((S, LANES), BF16), pltpu.VMEM((S, LANES), BF16),
                        pltpu.VMEM((nsub, 2 * tq, LANES), F32), pltpu.VMEM((nsub, 2 * tq, LANES), F32),
                        pltpu.VMEM((8, LANES), F32)],
        compiler_params=_params("arbitrary", "arbitrary", "arbitrary"),
        name="stick_breaking",
    )(p, p, p, qg, kg)


def _ret_kernel(q_ref, k_ref, v_ref, gt_ref, cos_ref, sin_ref, gn_ref, dm_ref, kdec_ref, qdec_ref, cdec_ref,
                o_ref, st_ref, *, ts):
    C = RET_CHUNK
    nch = ts // C
    pairs = range(2)
    chunks = range(nch)
    i = pl.program_id(1)

    @pl.when(i == 0)
    def _():
        st_ref[...] = jnp.zeros_like(st_ref)

    lane = _iota((1, LANES), 1)
    first_half = (lane & 63) < 32
    same_head = (_iota((LANES, LANES), 0) >> 6) == (_iota((LANES, LANES), 1) >> 6)
    mean_bd = _head_block_matrix(LANES, 1.0 / HEAD_DIM)
    cos = cos_ref[...]
    sin = sin_ref[...]

    def rotary(t):
        swapped = jnp.where(first_half, pltpu.roll(t, LANES - 32, 1), pltpu.roll(t, 32, 1))
        return t * cos + swapped * sin

    def rows(t, c):
        return t[c * C:(c + 1) * C]

    cols = [slice(h * LANES, (h + 1) * LANES) for h in pairs]
    q = [rotary(q_ref[:, cols[h]]) for h in pairs]
    k = [rotary(k_ref[:, cols[h]]) * (HEAD_DIM ** -0.5) for h in pairs]
    vb = [v_ref[:, cols[h]].astype(BF16) for h in pairs]
    kb = [k[h].astype(BF16) for h in pairs]
    q_lo = [jnp.where(lane < 64, q[h], 0.0).astype(BF16) for h in pairs]
    q_hi = [jnp.where(lane >= 64, q[h], 0.0).astype(BF16) for h in pairs]
    qd = [(q[h] * jnp.concatenate([qdec_ref[h]] * nch, axis=0)).astype(BF16) for h in pairs]
    kd = [k[h] * jnp.concatenate([kdec_ref[h]] * nch, axis=0) for h in pairs]

    hc = [(h, c) for h in pairs for c in chunks]
    scores = {(h, c): _nt(jnp.concatenate([rows(q_lo[h], c), rows(q_hi[h], c)], axis=0), rows(kb[h], c)) * dm_ref[h]
              for h, c in hc}
    kv = {(h, c): _nn(rows(kd[h], c).T.astype(BF16), rows(vb[h], c)) for h, c in hc}
    intra2 = {(h, c): _nn(scores[h, c].astype(BF16), rows(vb[h], c)) for h, c in hc}
    state = {}
    for h in pairs:
        st = st_ref[h]
        for c in chunks:
            state[h, c] = st
            st = cdec_ref[h] * st + jnp.where(same_head, kv[h, c], 0.0)
        st_ref[h] = st
    inter = {(h, c): _nn(rows(qd[h], c), state[h, c].astype(BF16)) for h, c in hc}
    for h in pairs:
        o = jnp.concatenate([jnp.where(lane < 64, intra2[h, c][0:C], intra2[h, c][C:2 * C]) + inter[h, c]
                             for c in chunks], axis=0)
        oc = o - _nn_rhs_exact(o, mean_bd)
        var = _nn_rhs_exact(oc * oc, mean_bd)
        on = oc * lax.rsqrt(var + RET_GN_EPS) * gn_ref[:, cols[h]]
        gt = gt_ref[:, cols[h]]
        o_ref[:, cols[h]] = on * (gt * _sigmoid(gt))


def _retention_tables(S):
    C = RET_CHUNK
    inv_freq = ROPE_BASE ** (-jnp.arange(0, HEAD_DIM, 2, dtype=F32) / HEAD_DIM)
    ang = jnp.arange(S, dtype=F32)[:, None] * inv_freq[None, :]
    cos = jnp.tile(jnp.cos(ang), (1, 4))
    sin = jnp.tile(jnp.concatenate([-jnp.sin(ang), jnp.sin(ang)], axis=1), (1, 2))
    log_gamma = jnp.log(1.0 - 2.0 ** (-5.0 - jnp.arange(N_HEADS, dtype=F32)))
    idx = jnp.arange(C, dtype=F32)
    rel = idx[:, None] - idx[None, :]
    intra = jnp.where(rel >= 0, jnp.exp(jnp.maximum(rel, 0.0) * log_gamma[:, None, None]), 0.0)
    dm = intra.reshape(2, 2 * C, C)
    lanes = lambda t: jnp.repeat(t.reshape(2, 2, C).transpose(0, 2, 1), HEAD_DIM, axis=2)
    kdec = lanes(jnp.exp((C - 1 - idx)[None, :] * log_gamma[:, None]))
    qdec = lanes(jnp.exp((idx + 1.0)[None, :] * log_gamma[:, None]))
    cdec = jnp.repeat(jnp.exp(C * log_gamma).reshape(2, 1, 2), HEAD_DIM, axis=2)
    return cos, sin, dm, kdec, qdec, cdec


def _retention(p, B, S, gn, tables, ts=512):
    ts = min(ts, S)
    nb = S // ts
    T = B * S
    C = RET_CHUNK
    cos, sin, dm, kdec, qdec, cdec = tables
    col = lambda c0: pl.BlockSpec((ts, WIDTH), lambda b, i: (b * nb + i, c0 // 2))
    tab = pl.BlockSpec((ts, LANES), lambda b, i: (i, 0))
    full = lambda a: pl.BlockSpec(a.shape, lambda b, i: (0,) * a.ndim)
    return pl.pallas_call(
        functools.partial(_ret_kernel, ts=ts),
        grid=(B, nb),
        in_specs=[col(COL_RET_Q), col(COL_RET_K), col(COL_RET_V), col(COL_RET_G), tab, tab,
                  full(gn), full(dm), full(kdec), full(qdec), full(cdec)],
        out_specs=pl.BlockSpec((ts, WIDTH), lambda b, i: (b * nb + i, 0)),
        out_shape=jax.ShapeDtypeStruct((T, WIDTH), F32),
        scratch_shapes=[pltpu.VMEM((2, LANES, LANES), F32)],
        compiler_params=_params("arbitrary", "arbitrary"),
        name="retention",
    )(p, p, p, p, cos, sin, gn, dm, kdec, qdec, cdec)


def _conv_kernel(ua_ref, ub_ref, dw_ref, db_ref, lnw_ref, lnb_ref, o_ref, buf, shifted, *, ts):
    i = pl.program_id(1)
    span = ts + CONV_HALO - SUBLANES

    @pl.when(i == 0)
    def _():
        buf[0:CONV_HALO, :] = jnp.zeros((CONV_HALO, WIDTH), F32)

    buf[CONV_HALO:, :] = ua_ref[...] * _sigmoid(ub_ref[...])
    for s in range(1, SUBLANES):
        shifted[s] = buf[s:s + span, :]
    acc = jnp.zeros((ts, WIDTH), F32) + db_ref[...]
    for j in range(CONV_WIDTH):
        start = CONV_HALO - (CONV_WIDTH - 1) + j
        s = start % SUBLANES
        window = buf[start:start + ts, :] if s == 0 else shifted[s, start - s:start - s + ts, :]
        acc = acc + dw_ref[j:j + 1, :] * window
    buf[0:CONV_HALO, :] = buf[ts:ts + CONV_HALO, :]
    xc = acc - jnp.mean(acc, axis=-1, keepdims=True)
    var = jnp.mean(xc * xc, axis=-1, keepdims=True)
    y = xc * lax.rsqrt(var + CONV_LN_EPS) * lnw_ref[...] + lnb_ref[...]
    o_ref[...] = y * _sigmoid(y)


def _conformer_conv(p, B, S, dw, db, ln_w, ln_b, ts=512):
    ts = min(ts, S)
    nb = S // ts
    T = B * S
    vec = pl.BlockSpec((1, WIDTH), lambda b, i: (0, 0))
    return pl.pallas_call(
        functools.partial(_conv_kernel, ts=ts),
        grid=(B, nb),
        in_specs=[pl.BlockSpec((ts, WIDTH), lambda b, i: (b * nb + i, COL_CONV_A)),
                  pl.BlockSpec((ts, WIDTH), lambda b, i: (b * nb + i, COL_CONV_B)),
                  pl.BlockSpec((CONV_HALO, WIDTH), lambda b, i: (0, 0)), vec, vec, vec],
        out_specs=pl.BlockSpec((ts, WIDTH), lambda b, i: (b * nb + i, 0)),
        out_shape=jax.ShapeDtypeStruct((T, WIDTH), F32),
        scratch_shapes=[pltpu.VMEM((ts + CONV_HALO, WIDTH), F32),
                        pltpu.VMEM((SUBLANES, ts + CONV_HALO - SUBLANES, WIDTH), F32)],
        compiler_params=_params("arbitrary", "arbitrary"),
        name="conformer_conv",
    )(p, p, dw, db, ln_w, ln_b)


def _merge_kernel(x_ref, g_ref, y0_ref, y1_ref, y2_ref, y3_ref, wg_ref, wb_ref, wo_ref, o_ref):
    x = x_ref[...]
    h = _rmsnorm_rows(x, g_ref[...]).astype(BF16)
    merged = None
    for n, y_ref in enumerate((y0_ref, y1_ref, y2_ref, y3_ref)):
        term = _sigmoid(_nn(h, wg_ref[n])) * _nn(y_ref[...].astype(BF16), wb_ref[n])
        merged = term if merged is None else merged + term
    o_ref[...] = x + _nn(merged.astype(BF16), wo_ref[...])


def _merge(x2, gain, ys, wg, wb, wo, tm=512):
    T, D = x2.shape
    tm = min(tm, T)
    row = lambda n: pl.BlockSpec((tm, n), lambda i: (i, 0))
    return pl.pallas_call(
        _merge_kernel,
        grid=(T // tm,),
        in_specs=[row(D), pl.BlockSpec((1, D), lambda i: (0, 0)), row(WIDTH), row(WIDTH), row(WIDTH), row(WIDTH),
                  pl.BlockSpec(wg.shape, lambda i: (0, 0, 0)),
                  pl.BlockSpec(wb.shape, lambda i: (0, 0, 0)),
                  pl.BlockSpec(wo.shape, lambda i: (0, 0))],
        out_specs=row(D),
        out_shape=jax.ShapeDtypeStruct((T, D), F32),
        compiler_params=_params("parallel"),
        name="merge",
    )(x2, gain, *ys, wg, wb, wo)


def _ffn_kernel(x_ref, g_ref, w1_ref, w3_ref, w2_ref, o_ref, h_s):
    f = pl.program_id(1)

    @pl.when(f == 0)
    def _():
        x = x_ref[...]
        h_s[...] = _rmsnorm_rows(x, g_ref[...]).astype(BF16)
        o_ref[...] = x

    h = h_s[...]
    a = _nn(h, w1_ref[...])
    b = _nn(h, w3_ref[...])
    o_ref[...] += _nn((a * _sigmoid(a) * b).astype(BF16), w2_ref[...])


def _ffn(x2, gain, w1, w3, w2, tm=1024, tf=1408):
    T, D = x2.shape
    tm = min(tm, T)
    nf = w1.shape[1] // tf
    return pl.pallas_call(
        _ffn_kernel,
        grid=(T // tm, nf),
        in_specs=[pl.BlockSpec((tm, D), lambda i, f: (i, 0)),
                  pl.BlockSpec((1, D), lambda i, f: (0, 0)),
                  pl.BlockSpec((D, tf), lambda i, f: (0, f)),
                  pl.BlockSpec((D, tf), lambda i, f: (0, f)),
                  pl.BlockSpec((tf, D), lambda i, f: (f, 0))],
        out_specs=pl.BlockSpec((tm, D), lambda i, f: (i, 0)),
        out_shape=jax.ShapeDtypeStruct((T, D), F32),
        scratch_shapes=[pltpu.VMEM((tm, D), BF16)],
        compiler_params=_params("parallel", "arbitrary"),
        name="ffn",
    )(x2, gain, w1, w3, w2)


def _moe_kernel(x_ref, g_ref, rt_ref, w1_ref, w3_ref, w2_ref, o_ref, h_s, gate_s, rank_s, rank_t_s, count_s,
                *, tm, cap, small):
    e = pl.program_id(1)
    lane = _iota((1, LANES), 1)

    @pl.when(e == 0)
    def _():
        x = x_ref[...]
        h = _rmsnorm_rows(x, g_ref[...])
        h_s[...] = h.astype(BF16)
        o_ref[...] = x
        logits = jnp.where(lane < N_EXPERTS, _nn_3pass(h, rt_ref[...]), -jnp.inf)
        m1 = jnp.max(logits, axis=-1, keepdims=True)
        i1 = jnp.min(jnp.where(logits == m1, lane, LANES), axis=-1, keepdims=True)
        rest = jnp.where(lane == i1, -jnp.inf, logits)
        m2 = jnp.max(rest, axis=-1, keepdims=True)
        i2 = jnp.min(jnp.where(rest == m2, lane, LANES), axis=-1, keepdims=True)
        e2 = jnp.exp(m2 - m1)
        den = 1.0 + e2
        gate_s[...] = jnp.where(lane == i1, 1.0 / den, 0.0) + jnp.where(lane == i2, e2 / den, 0.0)
        chosen = jnp.where((lane == i1) | (lane == i2), 1.0, 0.0)
        before = jnp.where(_iota((tm, tm), 1) < _iota((tm, tm), 0), 1.0, 0.0).astype(BF16)
        rank = jnp.where(chosen > 0.0, _nn(before, chosen.astype(BF16)), -1.0)
        rank_s[...] = rank
        rank_t_s[...] = rank.T
        count_s[...] = jnp.sum(chosen, axis=0, keepdims=True)

    mine = lane == e
    gate = jnp.sum(jnp.where(mine, gate_s[...], 0.0), axis=-1, keepdims=True)
    rank_col = jnp.sum(jnp.where(mine, rank_s[...], 0.0), axis=-1, keepdims=True)
    rank_row = rank_t_s[pl.ds(e, 1), :]
    count = jnp.sum(jnp.where(mine, count_s[...], 0.0))
    def one_pass(first, rows):
        lanes = -(-rows // LANES) * LANES
        slot_rows = _iota((rows, tm), 0).astype(F32)
        slot_cols = _iota((tm, lanes), 1).astype(F32)
        take = jnp.where(rank_row - first == slot_rows, 1.0, 0.0).astype(BF16)
        xg = _nn(take, h_s[...]).astype(BF16)
        a = _nn(xg, w1_ref[0])
        b = _nn(xg, w3_ref[0])
        y = _nn((a * _sigmoid(a) * b).astype(BF16), w2_ref[0]).astype(BF16)
        hit = rank_col - first == slot_cols
        if lanes > rows:
            y = jnp.concatenate([y, jnp.zeros((lanes - rows, y.shape[1]), BF16)], axis=0)
            hit = hit & (slot_cols < rows)
        put = jnp.where(hit, 1.0, 0.0).astype(BF16)
        o_ref[...] += gate * _nn(put, y)

    @pl.when((count > 0.0) & (count <= small))
    def _():
        one_pass(jnp.float32(0.0), small)

    @pl.when(count > small)
    def _():
        def more(first):
            one_pass(first, cap)
            return first + cap
        lax.while_loop(lambda first: first < count, more, jnp.float32(0.0))


def _moe(x2, gain, router_pad, w1, w3, w2, tm=1024, cap=320, small=256):
    T, D = x2.shape
    tm = min(tm, T)
    cap = min(cap, tm)
    small = min(small, cap)
    E, _, F = w1.shape
    return pl.pallas_call(
        functools.partial(_moe_kernel, tm=tm, cap=cap, small=small),
        grid=(T // tm, E),
        in_specs=[pl.BlockSpec((tm, D), lambda i, e: (i, 0)),
                  pl.BlockSpec((1, D), lambda i, e: (0, 0)),
                  pl.BlockSpec((D, LANES), lambda i, e: (0, 0)),
                  pl.BlockSpec((1, D, F), lambda i, e: (e, 0, 0)),
                  pl.BlockSpec((1, D, F), lambda i, e: (e, 0, 0)),
                  pl.BlockSpec((1, F, D), lambda i, e: (e, 0, 0))],
        out_specs=pl.BlockSpec((tm, D), lambda i, e: (i, 0)),
        out_shape=jax.ShapeDtypeStruct((T, D), F32),
        scratch_shapes=[pltpu.VMEM((tm, D), BF16), pltpu.VMEM((tm, LANES), F32), pltpu.VMEM((tm, LANES), F32),
                        pltpu.VMEM((LANES, tm), F32), pltpu.VMEM((1, LANES), F32)],
        compiler_params=_params("parallel", "arbitrary"),
        name="moe",
    )(x2, gain, router_pad, w1, w3, w2)


def kernel(x, norm_mix, w_in, rw_mu, rw_w0, rw_w2, rw_a0, rw_a2, rw_g2, rw_k_k, rw_k_a, rw_r_k, rw_ln_w, rw_ln_b, sb_q_norm, sb_k_norm, ret_gn, conv_dw, conv_b, conv_ln_w, conv_ln_b, w_gate, w_branch, w_out, norm_ffn, ffn_w1, ffn_w3, ffn_w2, router, moe_w1, moe_w3, moe_w2):
    B, S, D = x.shape
    depth = norm_mix.shape[0]
    x2 = x.reshape(B * S, D)
    tables = _retention_tables(S)
    row = lambda t: t.reshape(1, -1)
    for l in range(depth):
        p = _inproj(x2, row(norm_mix[l]), w_in[l].astype(BF16))
        zeros = jnp.zeros_like(rw_w2[l])
        wa2 = jnp.concatenate([jnp.concatenate([rw_w2[l], zeros], axis=1),
                               jnp.concatenate([zeros, rw_a2[l]], axis=1)], axis=0).astype(BF16)
        y_rw = _rwkv(p, B, S, row(rw_mu[l]), row(rw_w0[l]), row(rw_a0[l]), wa2, rw_g2[l].astype(BF16),
                     row(rw_k_k[l]), row(rw_k_a[l]), row(rw_r_k[l]), row(rw_ln_w[l]), row(rw_ln_b[l]))
        y_sb = _stick_breaking(p, B, S, row(jnp.tile(sb_q_norm[l], 2)), row(jnp.tile(sb_k_norm[l], 2)))
        y_ret = _retention(p, B, S, row(ret_gn[l]), tables)
        dw = jnp.concatenate([conv_dw[l], jnp.zeros((CONV_HALO - CONV_WIDTH, WIDTH), F32)], axis=0)
        y_conv = _conformer_conv(p, B, S, dw, row(conv_b[l]), row(conv_ln_w[l]), row(conv_ln_b[l]))
        x2 = _merge(x2, row(norm_mix[l]), (y_rw, y_sb, y_ret, y_conv),
                    w_gate[l].astype(BF16), w_branch[l].astype(BF16), w_out[l].astype(BF16))
        if l % 2 == 0:
            x2 = _ffn(x2, row(norm_ffn[l]), ffn_w1[l // 2].astype(BF16), ffn_w3[l // 2].astype(BF16),
                      ffn_w2[l // 2].astype(BF16))
        else:
            rt = jnp.concatenate([router[l // 2], jnp.zeros((D, LANES - N_EXPERTS), F32)], axis=1)
            x2 = _moe(x2, row(norm_ffn[l]), rt, moe_w1[l // 2].astype(BF16), moe_w3[l // 2].astype(BF16),
                      moe_w2[l // 2].astype(BF16))
    return x2.reshape(B, S, D)
```

```python
import functools

import jax
import jax.numpy as jnp
from jax import lax
from jax.experimental import pallas as pl
from jax.experimental.pallas import tpu as pltpu

F32 = jnp.float32
BF16 = jnp.bfloat16

LANES = 128
SUBLANES = 8
VMEM_LIMIT = 56 * 1024 * 1024

D_MODEL = 1024
HEAD_DIM = 64
N_HEADS = 4
WIDTH = N_HEADS * HEAD_DIM
NORM_EPS = 1e-6
LOG2_E = 1.4426950408889634
RW_DECAY_SCALE = 0.606531
RW_LN_EPS = 64e-5
RW_CHUNK = 64
RW_BASE = 8
RET_CHUNK = 128
RET_GN_EPS = 1e-5
ROPE_BASE = 10000.0
CONV_WIDTH = 31
CONV_HALO = 32
CONV_LN_EPS = 1e-5
N_EXPERTS = 8
N_IN = 3328
COL_RW = 0
COL_SB_Q, COL_SB_K, COL_SB_V = 8, 10, 12
COL_RET_Q, COL_RET_K, COL_RET_V, COL_RET_G = 14, 16, 18, 20
COL_CONV_A, COL_CONV_B = 11, 12


def _nn(a, b):
    return lax.dot_general(a, b, (((1,), (0,)), ((), ())), preferred_element_type=F32)


def _nt(a, b):
    return lax.dot_general(a, b, (((1,), (1,)), ((), ())), preferred_element_type=F32)


def _mm(a, b):
    return _nn(a.astype(BF16), b.astype(BF16))


def _split(x):
    hi = x.astype(BF16)
    lo = (x - hi.astype(F32)).astype(BF16)
    return hi, lo


def _nn_rhs_exact(x, m):
    hi, lo = _split(x)
    return _nn(hi, m) + _nn(lo, m)


def _nn_lhs_exact(m, x):
    hi, lo = _split(x)
    return _nn(m, hi) + _nn(m, lo)


def _nn_3pass(a, b):
    ah, al = _split(a)
    bh, bl = _split(b)
    return _nn(ah, bh) + (_nn(ah, bl) + _nn(al, bh))


def _iota(shape, axis):
    return lax.broadcasted_iota(jnp.int32, shape, axis)


def _head_block_matrix(n, value):
    same = (_iota((n, n), 0) >> 6) == (_iota((n, n), 1) >> 6)
    return jnp.where(same, value, 0.0).astype(BF16)


def _sigmoid(x):
    return 0.5 * jnp.tanh(0.5 * x) + 0.5


def _rmsnorm_rows(x, gain):
    return x * lax.rsqrt(jnp.mean(x * x, axis=-1, keepdims=True) + NORM_EPS) * gain


def _params(*sem):
    return pltpu.CompilerParams(dimension_semantics=sem, vmem_limit_bytes=VMEM_LIMIT)


def _inproj_kernel(x_ref, g_ref, w_ref, p_ref):
    h = _rmsnorm_rows(x_ref[...], g_ref[...])
    p_ref[...] = _nn(h.astype(BF16), w_ref[...])


def _inproj(x2, gain, w_bf16, tm=512):
    T, D = x2.shape
    N = w_bf16.shape[1]
    return pl.pallas_call(
        _inproj_kernel,
        grid=(T // tm,),
        in_specs=[pl.BlockSpec((tm, D), lambda i: (i, 0)),
                  pl.BlockSpec((1, D), lambda i: (0, 0)),
                  pl.BlockSpec((D, N), lambda i: (0, 0))],
        out_specs=pl.BlockSpec((tm, N), lambda i: (i, 0)),
        out_shape=jax.ShapeDtypeStruct((T, N), F32),
        compiler_params=_params("parallel"),
        name="inproj",
    )(x2, gain, w_bf16)


def _rw_kernel(p_ref, mu_ref, w0_ref, a0_ref, wa2_ref, g2_ref, kk_ref, ka_ref, rk_ref, lnw_ref, lnb_ref,
               o_ref, prev_ref, zt_ref, at_s, bt_s, kt_s, rt_s, qh_s, gmat_s, v_s, ecl_s, y0_s, y_s, bonus_s, g_s, hmat_s,
               *, ts, nbatch, group):
    C = RW_CHUNK
    W = WIDTH
    i = pl.program_id(0)

    @pl.when(i == 0)
    def _():
        prev_ref[...] = jnp.zeros_like(prev_ref)
        zt_ref[...] = jnp.zeros_like(zt_ref)

    lane = _iota((1, LANES), 1)
    row = _iota((ts, 1), 0)
    r4 = _iota((W, W), 0)
    c4 = _iota((W, W), 1)
    same_head = (r4 >> 6) == (c4 >> 6)
    tri4 = jnp.where(same_head & (c4 <= r4), 1.0, 0.0).astype(BF16)
    ones_bd = _head_block_matrix(W, 1.0)

    for b in range(nbatch):
        p = p_ref[b]
        shifted = jnp.where(row == 0, prev_ref[b, 0:1, :], pltpu.roll(p, 1, 0))
        prev_ref[b, 0:1, :] = p_ref[b, ts - 1:ts, :]
        pm = p + (shifted - p) * mu_ref[...]
        r = pm[:, 0:W]
        k = pm[:, W:2 * W]
        v = pm[:, 2 * W:3 * W]
        lora = pm[:, 3 * W:3 * W + LANES]
        pg = pm[:, 3 * W + LANES:]
        wa = _mm(jnp.where(lane < 64, jnp.tanh(lora), lora), wa2_ref[...])
        lw = -RW_DECAY_SCALE * _sigmoid(w0_ref[...] + wa[:, 0:W])
        a = _sigmoid(a0_ref[...] + wa[:, W:])
        g_s[b] = _mm(_sigmoid(pg), g2_ref[...])
        kk = k * kk_ref[...]
        kk = kk * jnp.minimum(lax.rsqrt(_nn_rhs_exact(kk * kk, ones_bd)), 1e12)
        kp = k * (1.0 + (a - 1.0) * ka_ref[...])
        bonus_s[b] = _nn_rhs_exact(r * kp * rk_ref[...], ones_bd) * v
        cl = jnp.concatenate([_nn_lhs_exact(tri4, lw[n * W:(n + 1) * W, :]) for n in range(ts // W)], axis=0)
        ecl = jnp.exp(cl)
        eml = jnp.exp(-cl)
        at_s[b] = (-kk * jnp.exp(cl - lw)).astype(BF16)
        bt_s[b] = (kk * a * eml).astype(BF16)
        kt_s[b] = (kp * eml).astype(BF16)
        rt_s[b] = (r * ecl).astype(BF16)
        v_s[b] = v
        ecl_s[b] = ecl

    head = _iota((1, W), 1) >> 6
    tw = _iota((C, W), 0)
    sw = _iota((C, W), 1) & (C - 1)
    strict = tw > sw
    incl = tw >= sw
    eye = jnp.where(tw == sw, 1.0, 0.0)

    def same_block(size):
        return (tw & -size) == (sw & -size)

    def stack4(x):
        return jnp.concatenate([jnp.where(head == h, x, jnp.zeros_like(x)) for h in range(N_HEADS)], axis=0)

    def block_diag(x):
        return jnp.concatenate([x.astype(BF16)] * N_HEADS, axis=0) * ones_bd

    chunks = [(b, slice(c * C, (c + 1) * C)) for c in range(ts // C) for b in range(nbatch)]
    for g0 in range(0, len(chunks), group):
        grp = chunks[g0:g0 + group]
        n = range(len(grp))
        at = [at_s[b, rows, :] for b, rows in grp]
        rt = [rt_s[b, rows, :] for b, rows in grp]
        b4 = [stack4(bt_s[b, rows, :]) for b, rows in grp]
        k4 = [stack4(kt_s[b, rows, :]) for b, rows in grp]
        v4 = [stack4(v_s[b, rows, :].astype(BF16)) for b, rows in grp]
        lab = [jnp.where(strict, _nt(at[j], b4[j]), 0.0) for j in n]
        lak = [jnp.where(strict, _nt(at[j], k4[j]), 0.0) for j in n]
        mrb = [jnp.where(incl, _nt(rt[j], b4[j]), 0.0) for j in n]
        mrk = [jnp.where(incl, _nt(rt[j], k4[j]), 0.0) for j in n]
        m = [jnp.where(same_block(RW_BASE), lab[j], 0.0) for j in n]
        tinv = [eye + m[j] for j in n]
        for _ in range(2):
            m = [_nn(m[j].astype(BF16), block_diag(m[j])) for j in n]
            tinv = [tinv[j] + _nn(tinv[j].astype(BF16), block_diag(m[j])) for j in n]
        size = RW_BASE
        while size < C:
            below = same_block(2 * size) & jnp.logical_not(same_block(size))
            e_bd = [block_diag(jnp.where(below, lab[j], 0.0)) for j in n]
            half = [_nn(tinv[j].astype(BF16), e_bd[j]) for j in n]
            tinv = [tinv[j] + _nn(half[j].astype(BF16), block_diag(tinv[j])) for j in n]
            size *= 2
        tb = [tinv[j].astype(BF16) for j in n]
        mb = [mrb[j].astype(BF16) for j in n]
        w = [_nn(tb[j], stack4(at[j])) for j in n]
        u0 = [_nn(tb[j], stack4(_nn(lak[j].astype(BF16), v4[j])).astype(BF16)) for j in n]
        qh = [rt[j].astype(F32) + _nn(mb[j], stack4(w[j]).astype(BF16)) for j in n]
        y0 = [_nn(mb[j], stack4(u0[j]).astype(BF16)) + _nn(mrk[j].astype(BF16), v4[j]) for j in n]
        pad = jnp.zeros((C, W), F32)
        for j, (b, rows) in enumerate(grp):
            c = rows.start // C
            wt = jnp.concatenate([w[j], pad], axis=0).T.astype(BF16)
            uvt = jnp.concatenate([u0[j], v_s[b, rows, :]], axis=0).T.astype(BF16)
            bt = bt_s[b, rows, :]
            gmat = _nn(wt, jnp.concatenate([bt, jnp.zeros_like(bt)], axis=0))
            hmat = _nn(uvt, jnp.concatenate([bt, kt_s[b, rows, :]], axis=0))
            gmat_s[b, c] = jnp.where(same_head, gmat, 0.0).astype(BF16)
            hmat_s[b, c] = jnp.where(same_head, hmat, 0.0)
            qh_s[b, rows, :] = qh[j].astype(BF16)
            y0_s[b, rows, :] = y0[j]

    def chunk(c, carry):
        rows = pl.ds(pl.multiple_of(c * C, C), C)
        for b in range(nbatch):
            zt = zt_ref[b]
            ztb = zt.astype(BF16)
            y_s[b, rows, :] = _nt(qh_s[b, rows, :], ztb) + y0_s[b, rows, :]
            g_end = ecl_s[b, pl.ds(c * C + C - 1, 1), :]
            zt_ref[b] = (zt + _nn(ztb, gmat_s[b, c]) + hmat_s[b, c]) * g_end
        return carry

    lax.fori_loop(0, ts // C, chunk, 0)

    mean_bd = _head_block_matrix(W, 1.0 / HEAD_DIM)
    for b in range(nbatch):
        y = y_s[b]
        yc = y - _nn_rhs_exact(y, mean_bd)
        var = _nn_rhs_exact(yc * yc, mean_bd)
        y = yc * lax.rsqrt(var + RW_LN_EPS) * lnw_ref[...] + lnb_ref[...]
        o_ref[b] = (y + bonus_s[b]) * g_s[b]


def _rwkv(p, B, S, mu, w0, a0, wa2, g2, k_k, k_a, r_k, ln_w, ln_b, ts=512, group=8):
    ts = min(ts, S)
    W = WIDTH
    vec = lambda n: pl.BlockSpec((1, n), lambda i: (0, 0))
    full = lambda a: pl.BlockSpec(a.shape, lambda i: (0, 0))
    out = pl.pallas_call(
        functools.partial(_rw_kernel, ts=ts, nbatch=B, group=group),
        grid=(S // ts,),
        in_specs=[pl.BlockSpec((B, ts, 4 * W), lambda i: (0, i, COL_RW)),
                  vec(4 * W), vec(W), vec(W), full(wa2), full(g2), vec(W), vec(W), vec(W), vec(W), vec(W)],
        out_specs=pl.BlockSpec((B, ts, W), lambda i: (0, i, 0)),
        out_shape=jax.ShapeDtypeStruct((B, S, W), F32),
        scratch_shapes=[pltpu.VMEM((B, 8, 4 * W), F32), pltpu.VMEM((B, W, W), F32)]
                       + [pltpu.VMEM((B, ts, W), BF16)] * 5 + [pltpu.VMEM((B, ts // RW_CHUNK, W, W), BF16)]
                       + [pltpu.VMEM((B, ts, W), F32)] * 6 + [pltpu.VMEM((B, ts // RW_CHUNK, W, W), F32)],
        compiler_params=_params("arbitrary"),
        name="rwkv7",
    )(p.reshape(B, S, -1), mu, w0, a0, wa2, g2, k_k, k_a, r_k, ln_w, ln_b)
    return out.reshape(B * S, W)


def _sb_kernel(q_ref, k_ref, v_ref, qg_ref, kg_ref, o_ref, kb, vb, acc_o, acc_l, knorm,
               *, tq, tk, seq, prep, nsub):
    i = pl.program_id(2)
    lane = _iota((1, LANES), 1)
    mean_bd = _head_block_matrix(LANES, 1.0 / HEAD_DIM)
    ones = jnp.ones((LANES, LANES), BF16)

    @pl.when(i == 0)
    def _():
        knorm[...] = jnp.zeros_like(knorm)

        def body(c, carry):
            rows = pl.ds(pl.multiple_of(c * prep, prep), prep)
            kf = k_ref[rows, :]
            ms = _nn_rhs_exact(kf * kf, mean_bd)
            kn = (kf * lax.rsqrt(ms + NORM_EPS) * kg_ref[...]).astype(BF16)
            kb[rows, :] = kn
            vb[rows, :] = v_ref[rows, :].astype(BF16)
            kn = kn.astype(F32)
            sq = _nn_rhs_exact(kn * kn, ones)
            knorm[...] = jnp.maximum(knorm[...], jnp.max(sq.reshape(prep // 8, 8, LANES), axis=0))
            return carry
        lax.fori_loop(0, seq // prep, body, 0)

    subs = range(nsub)
    tile = [i * nsub + s for s in subs]
    q = q_ref[...]
    ms = _nn_rhs_exact(q * q, mean_bd)
    qn = q * lax.rsqrt(ms + NORM_EPS) * qg_ref[...] * (HEAD_DIM ** -0.5 * LOG2_E)
    q_lo = jnp.where(lane < 64, qn, 0.0).astype(BF16)
    q_hi = jnp.where(lane >= 64, qn, 0.0).astype(BF16)
    q2 = [jnp.concatenate([q_lo[s * tq:(s + 1) * tq], q_hi[s * tq:(s + 1) * tq]], axis=0) for s in subs]

    qf = (q_lo + q_hi).astype(F32)
    qsq = jnp.max(_nn_rhs_exact(qf * qf, ones))
    zmax = jnp.sqrt(qsq * jnp.max(knorm[...])) * 1.01
    stop_at = 152.0 + zmax * (1.0 / 256.0)

    def softplus2(z2):
        return jnp.maximum(z2, 0.0) + jnp.log2(1.0 + jnp.exp2(-jnp.abs(z2)))

    def rev_incl(n):
        return jnp.where(_iota((n, n), 0) >= _iota((n, n), 1), 1.0, 0.0).astype(BF16)

    prev = [pl.ds(pl.multiple_of(jnp.maximum(tile[s] - 1, 0) * tk, tk), tk) for s in subs]
    diag = [pl.ds(pl.multiple_of(tile[s] * tk, tk), tk) for s in subs]
    col = _iota((2 * tq, 2 * tk), 1)
    causal = (col - tk) < (_iota((2 * tq, 2 * tk), 0) & (tq - 1))
    valid = [causal & (col >= jnp.where(i > 0, 0, tk))] + [causal] * (nsub - 1)
    rev = rev_incl(tk)
    z2 = [_nt(q2[s], jnp.concatenate([kb[prev[s], :], kb[diag[s], :]], axis=0)) for s in subs]
    sp = [jnp.where(valid[s], softplus2(z2[s]), 0.0).astype(BF16) for s in subs]
    cum_diag = [_nn(sp[s][:, tk:], rev) for s in subs]
    cum = [jnp.concatenate([_nn(sp[s][:, :tk], rev) + cum_diag[s][:, 0:1], cum_diag[s]], axis=1) for s in subs]
    attn = [jnp.where(valid[s], jnp.exp2(z2[s] - cum[s]), 0.0).astype(BF16) for s in subs]
    for s in subs:
        acc_o[s] = _nn(attn[s], jnp.concatenate([vb[prev[s], :], vb[diag[s], :]], axis=0))
        acc_l[s] = jnp.broadcast_to(cum[s][:, 0:1], (2 * tq, LANES))

    for s in subs:
        def step(j, s=s):
            rows = pl.ds(pl.multiple_of(j * tk, tk), tk)
            z2 = _nt(q2[s], kb[rows, :])
            cum = _nn(softplus2(z2).astype(BF16), rev)
            seen = acc_l[s]
            attn = jnp.exp2(z2 - cum - jnp.concatenate([seen] * (tk // LANES), axis=1))
            acc_o[s] += _nn(attn.astype(BF16), vb[rows, :])
            seen = seen + cum[:, 0:1]
            acc_l[s] = seen
            return jnp.min(seen)

        def cond(carry, s=s):
            t, low = carry
            return (t < tile[s]) & (low < stop_at)

        def body(carry, s=s, step=step):
            t, _ = carry
            return t + 1, step(tile[s] - 1 - t)
        lax.while_loop(cond, body, (1, jnp.min(cum[s][:, 0:1])))

    for s in subs:
        o_ref[s * tq:(s + 1) * tq, :] = jnp.where(lane < 64, acc_o[s, 0:tq, :], acc_o[s, tq:2 * tq, :])


def _stick_breaking(p, B, S, qg, kg, tq=256, tk=256, nsub=2):
    assert tq == tk and S % (nsub * tq) == 0
    nq = S // (nsub * tq)
    T = B * S
    prep = min(512, S)
    return pl.pallas_call(
        functools.partial(_sb_kernel, tq=tq, tk=tk, seq=S, prep=prep, nsub=nsub),
        grid=(B, 2, nq),
        in_specs=[pl.BlockSpec((nsub * tq, LANES), lambda b, h, i: (b * nq + i, COL_SB_Q + h)),
                  pl.BlockSpec((S, LANES), lambda b, h, i: (b, COL_SB_K + h)),
                  pl.BlockSpec((S, LANES), lambda b, h, i: (b, COL_SB_V + h)),
                  pl.BlockSpec((1, LANES), lambda b, h, i: (0, 0)),
                  pl.BlockSpec((1, LANES), lambda b, h, i: (0, 0))],
        out_specs=pl.BlockSpec((nsub * tq, LANES), lambda b, h, i: (b * nq + i, h)),
        out_shape=jax.ShapeDtypeStruct((T, WIDTH), F32),
        scratch_shapes=[pltpu.VMEM((S, LANES), BF16), pltpu.VMEM((S, LANES), BF16),
                        pltpu.VMEM((nsub, 2 * tq, LANES), F32), pltpu.VMEM((nsub, 2 * tq, LANES), F32),
                        pltpu.VMEM((8, LANES), F32)],
        compiler_params=_params("arbitrary", "arbitrary", "arbitrary"),
        name="stick_breaking",
    )(p, p, p, qg, kg)


def _ret_kernel(q_ref, k_ref, v_ref, gt_ref, cos_ref, sin_ref, gn_ref, dm_ref, kdec_ref, qdec_ref, cdec_ref,
                o_ref, st_ref, *, ts):
    C = RET_CHUNK
    nch = ts // C
    pairs = range(2)
    chunks = range(nch)
    i = pl.program_id(1)

    @pl.when(i == 0)
    def _():
        st_ref[...] = jnp.zeros_like(st_ref)

    lane = _iota((1, LANES), 1)
    first_half = (lane & 63) < 32
    same_head = (_iota((LANES, LANES), 0) >> 6) == (_iota((LANES, LANES), 1) >> 6)
    mean_bd = _head_block_matrix(LANES, 1.0 / HEAD_DIM)
    cos = cos_ref[...]
    sin = sin_ref[...]

    def rotary(t):
        swapped = jnp.where(first_half, pltpu.roll(t, LANES - 32, 1), pltpu.roll(t, 32, 1))
        return t * cos + swapped * sin

    def rows(t, c):
        return t[c * C:(c + 1) * C]

    cols = [slice(h * LANES, (h + 1) * LANES) for h in pairs]
    q = [rotary(q_ref[:, cols[h]]) for h in pairs]
    k = [rotary(k_ref[:, cols[h]]) * (HEAD_DIM ** -0.5) for h in pairs]
    vb = [v_ref[:, cols[h]].astype(BF16) for h in pairs]
    kb = [k[h].astype(BF16) for h in pairs]
    q_lo = [jnp.where(lane < 64, q[h], 0.0).astype(BF16) for h in pairs]
    q_hi = [jnp.where(lane >= 64, q[h], 0.0).astype(BF16) for h in pairs]
    qd = [(q[h] * jnp.concatenate([qdec_ref[h]] * nch, axis=0)).astype(BF16) for h in pairs]
    kd = [k[h] * jnp.concatenate([kdec_ref[h]] * nch, axis=0) for h in pairs]

    hc = [(h, c) for h in pairs for c in chunks]
    scores = {(h, c): _nt(jnp.concatenate([rows(q_lo[h], c), rows(q_hi[h], c)], axis=0), rows(kb[h], c)) * dm_ref[h]
              for h, c in hc}
    kv = {(h, c): _nn(rows(kd[h], c).T.astype(BF16), rows(vb[h], c)) for h, c in hc}
    intra2 = {(h, c): _nn(scores[h, c].astype(BF16), rows(vb[h], c)) for h, c in hc}
    state = {}
    for h in pairs:
        st = st_ref[h]
        for c in chunks:
            state[h, c] = st
            st = cdec_ref[h] * st + jnp.where(same_head, kv[h, c], 0.0)
        st_ref[h] = st
    inter = {(h, c): _nn(rows(qd[h], c), state[h, c].astype(BF16)) for h, c in hc}
    for h in pairs:
        o = jnp.concatenate([jnp.where(lane < 64, intra2[h, c][0:C], intra2[h, c][C:2 * C]) + inter[h, c]
                             for c in chunks], axis=0)
        oc = o - _nn_rhs_exact(o, mean_bd)
        var = _nn_rhs_exact(oc * oc, mean_bd)
        on = oc * lax.rsqrt(var + RET_GN_EPS) * gn_ref[:, cols[h]]
        gt = gt_ref[:, cols[h]]
        o_ref[:, cols[h]] = on * (gt * _sigmoid(gt))


def _retention_tables(S):
    C = RET_CHUNK
    inv_freq = ROPE_BASE ** (-jnp.arange(0, HEAD_DIM, 2, dtype=F32) / HEAD_DIM)
    ang = jnp.arange(S, dtype=F32)[:, None] * inv_freq[None, :]
    cos = jnp.tile(jnp.cos(ang), (1, 4))
    sin = jnp.tile(jnp.concatenate([-jnp.sin(ang), jnp.sin(ang)], axis=1), (1, 2))
    log_gamma = jnp.log(1.0 - 2.0 ** (-5.0 - jnp.arange(N_HEADS, dtype=F32)))
    idx = jnp.arange(C, dtype=F32)
    rel = idx[:, None] - idx[None, :]
    intra = jnp.where(rel >= 0, jnp.exp(jnp.maximum(rel, 0.0) * log_gamma[:, None, None]), 0.0)
    dm = intra.reshape(2, 2 * C, C)
    lanes = lambda t: jnp.repeat(t.reshape(2, 2, C).transpose(0, 2, 1), HEAD_DIM, axis=2)
    kdec = lanes(jnp.exp((C - 1 - idx)[None, :] * log_gamma[:, None]))
    qdec = lanes(jnp.exp((idx + 1.0)[None, :] * log_gamma[:, None]))
    cdec = jnp.repeat(jnp.exp(C * log_gamma).reshape(2, 1, 2), HEAD_DIM, axis=2)
    return cos, sin, dm, kdec, qdec, cdec


def _retention(p, B, S, gn, tables, ts=512):
    ts = min(ts, S)
    nb = S // ts
    T = B * S
    C = RET_CHUNK
    cos, sin, dm, kdec, qdec, cdec = tables
    col = lambda c0: pl.BlockSpec((ts, WIDTH), lambda b, i: (b * nb + i, c0 // 2))
    tab = pl.BlockSpec((ts, LANES), lambda b, i: (i, 0))
    full = lambda a: pl.BlockSpec(a.shape, lambda b, i: (0,) * a.ndim)
    return pl.pallas_call(
        functools.partial(_ret_kernel, ts=ts),
        grid=(B, nb),
        in_specs=[col(COL_RET_Q), col(COL_RET_K), col(COL_RET_V), col(COL_RET_G), tab, tab,
                  full(gn), full(dm), full(kdec), full(qdec), full(cdec)],
        out_specs=pl.BlockSpec((ts, WIDTH), lambda b, i: (b * nb + i, 0)),
        out_shape=jax.ShapeDtypeStruct((T, WIDTH), F32),
        scratch_shapes=[pltpu.VMEM((2, LANES, LANES), F32)],
        compiler_params=_params("arbitrary", "arbitrary"),
        name="retention",
    )(p, p, p, p, cos, sin, gn, dm, kdec, qdec, cdec)


def _conv_kernel(ua_ref, ub_ref, dw_ref, db_ref, lnw_ref, lnb_ref, o_ref, buf, shifted, *, ts):
    i = pl.program_id(1)
    span = ts + CONV_HALO - SUBLANES

    @pl.when(i == 0)
    def _():
        buf[0:CONV_HALO, :] = jnp.zeros((CONV_HALO, WIDTH), F32)

    buf[CONV_HALO:, :] = ua_ref[...] * _sigmoid(ub_ref[...])
    for s in range(1, SUBLANES):
        shifted[s] = buf[s:s + span, :]
    acc = jnp.zeros((ts, WIDTH), F32) + db_ref[...]
    for j in range(CONV_WIDTH):
        start = CONV_HALO - (CONV_WIDTH - 1) + j
        s = start % SUBLANES
        window = buf[start:start + ts, :] if s == 0 else shifted[s, start - s:start - s + ts, :]
        acc = acc + dw_ref[j:j + 1, :] * window
    buf[0:CONV_HALO, :] = buf[ts:ts + CONV_HALO, :]
    xc = acc - jnp.mean(acc, axis=-1, keepdims=True)
    var = jnp.mean(xc * xc, axis=-1, keepdims=True)
    y = xc * lax.rsqrt(var + CONV_LN_EPS) * lnw_ref[...] + lnb_ref[...]
    o_ref[...] = y * _sigmoid(y)


def _conformer_conv(p, B, S, dw, db, ln_w, ln_b, ts=512):
    ts = min(ts, S)
    nb = S // ts
    T = B * S
    vec = pl.BlockSpec((1, WIDTH), lambda b, i: (0, 0))
    return pl.pallas_call(
        functools.partial(_conv_kernel, ts=ts),
        grid=(B, nb),
        in_specs=[pl.BlockSpec((ts, WIDTH), lambda b, i: (b * nb + i, COL_CONV_A)),
                  pl.BlockSpec((ts, WIDTH), lambda b, i: (b * nb + i, COL_CONV_B)),
                  pl.BlockSpec((CONV_HALO, WIDTH), lambda b, i: (0, 0)), vec, vec, vec],
        out_specs=pl.BlockSpec((ts, WIDTH), lambda b, i: (b * nb + i, 0)),
        out_shape=jax.ShapeDtypeStruct((T, WIDTH), F32),
        scratch_shapes=[pltpu.VMEM((ts + CONV_HALO, WIDTH), F32),
                        pltpu.VMEM((SUBLANES, ts + CONV_HALO - SUBLANES, WIDTH), F32)],
        compiler_params=_params("arbitrary", "arbitrary"),
        name="conformer_conv",
    )(p, p, dw, db, ln_w, ln_b)


def _merge_kernel(x_ref, g_ref, y0_ref, y1_ref, y2_ref, y3_ref, wg_ref, wb_ref, wo_ref, o_ref):
    x = x_ref[...]
    h = _rmsnorm_rows(x, g_ref[...]).astype(BF16)
    merged = None
    for n, y_ref in enumerate((y0_ref, y1_ref, y2_ref, y3_ref)):
        term = _sigmoid(_nn(h, wg_ref[n])) * _nn(y_ref[...].astype(BF16), wb_ref[n])
        merged = term if merged is None else merged + term
    o_ref[...] = x + _nn(merged.astype(BF16), wo_ref[...])


def _merge(x2, gain, ys, wg, wb, wo, tm=512):
    T, D = x2.shape
    tm = min(tm, T)
    row = lambda n: pl.BlockSpec((tm, n), lambda i: (i, 0))
    return pl.pallas_call(
        _merge_kernel,
        grid=(T // tm,),
        in_specs=[row(D), pl.BlockSpec((1, D), lambda i: (0, 0)), row(WIDTH), row(WIDTH), row(WIDTH), row(WIDTH),
                  pl.BlockSpec(wg.shape, lambda i: (0, 0, 0)),
                  pl.BlockSpec(wb.shape, lambda i: (0, 0, 0)),
                  pl.BlockSpec(wo.shape, lambda i: (0, 0))],
        out_specs=row(D),
        out_shape=jax.ShapeDtypeStruct((T, D), F32),
        compiler_params=_params("parallel"),
        name="merge",
    )(x2, gain, *ys, wg, wb, wo)


def _ffn_kernel(x_ref, g_ref, w1_ref, w3_ref, w2_ref, o_ref, h_s):
    f = pl.program_id(1)

    @pl.when(f == 0)
    def _():
        x = x_ref[...]
        h_s[...] = _rmsnorm_rows(x, g_ref[...]).astype(BF16)
        o_ref[...] = x

    h = h_s[...]
    a = _nn(h, w1_ref[...])
    b = _nn(h, w3_ref[...])
    o_ref[...] += _nn((a * _sigmoid(a) * b).astype(BF16), w2_ref[...])


def _ffn(x2, gain, w1, w3, w2, tm=1024, tf=1408):
    T, D = x2.shape
    tm = min(tm, T)
    nf = w1.shape[1] // tf
    return pl.pallas_call(
        _ffn_kernel,
        grid=(T // tm, nf),
        in_specs=[pl.BlockSpec((tm, D), lambda i, f: (i, 0)),
                  pl.BlockSpec((1, D), lambda i, f: (0, 0)),
                  pl.BlockSpec((D, tf), lambda i, f: (0, f)),
                  pl.BlockSpec((D, tf), lambda i, f: (0, f)),
                  pl.BlockSpec((tf, D), lambda i, f: (f, 0))],
        out_specs=pl.BlockSpec((tm, D), lambda i, f: (i, 0)),
        out_shape=jax.ShapeDtypeStruct((T, D), F32),
        scratch_shapes=[pltpu.VMEM((tm, D), BF16)],
        compiler_params=_params("parallel", "arbitrary"),
        name="ffn",
    )(x2, gain, w1, w3, w2)


def _moe_kernel(x_ref, g_ref, rt_ref, w1_ref, w3_ref, w2_ref, o_ref, h_s, gate_s, rank_s, rank_t_s, count_s,
                *, tm, cap, small):
    e = pl.program_id(1)
    lane = _iota((1, LANES), 1)

    @pl.when(e == 0)
    def _():
        x = x_ref[...]
        h = _rmsnorm_rows(x, g_ref[...])
        h_s[...] = h.astype(BF16)
        o_ref[...] = x
        logits = jnp.where(lane < N_EXPERTS, _nn_3pass(h, rt_ref[...]), -jnp.inf)
        m1 = jnp.max(logits, axis=-1, keepdims=True)
        i1 = jnp.min(jnp.where(logits == m1, lane, LANES), axis=-1, keepdims=True)
        rest = jnp.where(lane == i1, -jnp.inf, logits)
        m2 = jnp.max(rest, axis=-1, keepdims=True)
        i2 = jnp.min(jnp.where(rest == m2, lane, LANES), axis=-1, keepdims=True)
        e2 = jnp.exp(m2 - m1)
        den = 1.0 + e2
        gate_s[...] = jnp.where(lane == i1, 1.0 / den, 0.0) + jnp.where(lane == i2, e2 / den, 0.0)
        chosen = jnp.where((lane == i1) | (lane == i2), 1.0, 0.0)
        before = jnp.where(_iota((tm, tm), 1) < _iota((tm, tm), 0), 1.0, 0.0).astype(BF16)
        rank = jnp.where(chosen > 0.0, _nn(before, chosen.astype(BF16)), -1.0)
        rank_s[...] = rank
        rank_t_s[...] = rank.T
        count_s[...] = jnp.sum(chosen, axis=0, keepdims=True)

    mine = lane == e
    gate = jnp.sum(jnp.where(mine, gate_s[...], 0.0), axis=-1, keepdims=True)
    rank_col = jnp.sum(jnp.where(mine, rank_s[...], 0.0), axis=-1, keepdims=True)
    rank_row = rank_t_s[pl.ds(e, 1), :]
    count = jnp.sum(jnp.where(mine, count_s[...], 0.0))
    def one_pass(first, rows):
        lanes = -(-rows // LANES) * LANES
        slot_rows = _iota((rows, tm), 0).astype(F32)
        slot_cols = _iota((tm, lanes), 1).astype(F32)
        take = jnp.where(rank_row - first == slot_rows, 1.0, 0.0).astype(BF16)
        xg = _nn(take, h_s[...]).astype(BF16)
        a = _nn(xg, w1_ref[0])
        b = _nn(xg, w3_ref[0])
        y = _nn((a * _sigmoid(a) * b).astype(BF16), w2_ref[0]).astype(BF16)
        hit = rank_col - first == slot_cols
        if lanes > rows:
            y = jnp.concatenate([y, jnp.zeros((lanes - rows, y.shape[1]), BF16)], axis=0)
            hit = hit & (slot_cols < rows)
        put = jnp.where(hit, 1.0, 0.0).astype(BF16)
        o_ref[...] += gate * _nn(put, y)

    @pl.when((count > 0.0) & (count <= small))
    def _():
        one_pass(jnp.float32(0.0), small)

    @pl.when(count > small)
    def _():
        def more(first):
            one_pass(first, cap)
            return first + cap
        lax.while_loop(lambda first: first < count, more, jnp.float32(0.0))


def _moe(x2, gain, router_pad, w1, w3, w2, tm=1024, cap=320, small=256):
    T, D = x2.shape
    tm = min(tm, T)
    cap = min(cap, tm)
    small = min(small, cap)
    E, _, F = w1.shape
    return pl.pallas_call(
        functools.partial(_moe_kernel, tm=tm, cap=cap, small=small),
        grid=(T // tm, E),
        in_specs=[pl.BlockSpec((tm, D), lambda i, e: (i, 0)),
                  pl.BlockSpec((1, D), lambda i, e: (0, 0)),
                  pl.BlockSpec((D, LANES), lambda i, e: (0, 0)),
                  pl.BlockSpec((1, D, F), lambda i, e: (e, 0, 0)),
                  pl.BlockSpec((1, D, F), lambda i, e: (e, 0, 0)),
                  pl.BlockSpec((1, F, D), lambda i, e: (e, 0, 0))],
        out_specs=pl.BlockSpec((tm, D), lambda i, e: (i, 0)),
        out_shape=jax.ShapeDtypeStruct((T, D), F32),
        scratch_shapes=[pltpu.VMEM((tm, D), BF16), pltpu.VMEM((tm, LANES), F32), pltpu.VMEM((tm, LANES), F32),
                        pltpu.VMEM((LANES, tm), F32), pltpu.VMEM((1, LANES), F32)],
        compiler_params=_params("parallel", "arbitrary"),
        name="moe",
    )(x2, gain, router_pad, w1, w3, w2)


def kernel(x, norm_mix, w_in, rw_mu, rw_w0, rw_w2, rw_a0, rw_a2, rw_g2, rw_k_k, rw_k_a, rw_r_k, rw_ln_w, rw_ln_b, sb_q_norm, sb_k_norm, ret_gn, conv_dw, conv_b, conv_ln_w, conv_ln_b, w_gate, w_branch, w_out, norm_ffn, ffn_w1, ffn_w3, ffn_w2, router, moe_w1, moe_w3, moe_w2):
    B, S, D = x.shape
    depth = norm_mix.shape[0]
    x2 = x.reshape(B * S, D)
    tables = _retention_tables(S)
    row = lambda t: t.reshape(1, -1)
    for l in range(depth):
        p = _inproj(x2, row(norm_mix[l]), w_in[l].astype(BF16))
        zeros = jnp.zeros_like(rw_w2[l])
        wa2 = jnp.concatenate([jnp.concatenate([rw_w2[l], zeros], axis=1),
                               jnp.concatenate([zeros, rw_a2[l]], axis=1)], axis=0).astype(BF16)
        y_rw = _rwkv(p, B, S, row(rw_mu[l]), row(rw_w0[l]), row(rw_a0[l]), wa2, rw_g2[l].astype(BF16),
                     row(rw_k_k[l]), row(rw_k_a[l]), row(rw_r_k[l]), row(rw_ln_w[l]), row(rw_ln_b[l]))
        y_sb = _stick_breaking(p, B, S, row(jnp.tile(sb_q_norm[l], 2)), row(jnp.tile(sb_k_norm[l], 2)))
        y_ret = _retention(p, B, S, row(ret_gn[l]), tables)
        dw = jnp.concatenate([conv_dw[l], jnp.zeros((CONV_HALO - CONV_WIDTH, WIDTH), F32)], axis=0)
        y_conv = _conformer_conv(p, B, S, dw, row(conv_b[l]), row(conv_ln_w[l]), row(conv_ln_b[l]))
        x2 = _merge(x2, row(norm_mix[l]), (y_rw, y_sb, y_ret, y_conv),
                    w_gate[l].astype(BF16), w_branch[l].astype(BF16), w_out[l].astype(BF16))
        if l % 2 == 0:
            x2 = _ffn(x2, row(norm_ffn[l]), ffn_w1[l // 2].astype(BF16), ffn_w3[l // 2].astype(BF16),
                      ffn_w2[l // 2].astype(BF16))
        else:
            rt = jnp.concatenate([router[l // 2], jnp.zeros((D, LANES - N_EXPERTS), F32)], axis=1)
            x2 = _moe(x2, row(norm_ffn[l]), rt, moe_w1[l // 2].astype(BF16), moe_w3[l // 2].astype(BF16),
                      moe_w2[l // 2].astype(BF16))
    return x2.reshape(B, S, D)
```

```python
import functools

import jax
import jax.numpy as jnp
from jax import lax
from jax.experimental import pallas as pl
from jax.experimental.pallas import tpu as pltpu

F32 = jnp.float32
BF16 = jnp.bfloat16

LANES = 128
SUBLANES = 8
VMEM_LIMIT = 56 * 1024 * 1024

D_MODEL = 1024
HEAD_DIM = 64
HEAD_SHIFT = HEAD_DIM.bit_length() - 1
N_HEADS = 4
WIDTH = N_HEADS * HEAD_DIM
NORM_EPS = 1e-6
LOG2_E = 1.4426950408889634
RW_DECAY_SCALE = 0.606531
RW_LN_EPS = 64e-5
RW_CHUNK = 64
RW_BASE = 8
RW_MIN_NORM = 1e-12
RW_DECAY_LORA = 64
BF16_ROUNDING = 2.0 ** -8
SB_ZERO_EXPONENT = 152.0
SB_NORM_MARGIN = 1.01
RET_CHUNK = 128
RET_GN_EPS = 1e-5
ROPE_BASE = 10000.0
CONV_WIDTH = 31
CONV_HALO = 32
CONV_LN_EPS = 1e-5
N_EXPERTS = 8
N_IN = 3328
COL_RW = 0
COL_SB_Q, COL_SB_K, COL_SB_V = 8, 10, 12
COL_RET_Q, COL_RET_K, COL_RET_V, COL_RET_G = 14, 16, 18, 20
COL_CONV_A, COL_CONV_B = 11, 12


def _nn(a, b):
    return lax.dot_general(a, b, (((1,), (0,)), ((), ())), preferred_element_type=F32)


def _nt(a, b):
    return lax.dot_general(a, b, (((1,), (1,)), ((), ())), preferred_element_type=F32)


def _mm(a, b):
    return _nn(a.astype(BF16), b.astype(BF16))


def _split(x):
    hi = x.astype(BF16)
    lo = (x - hi.astype(F32)).astype(BF16)
    return hi, lo


def _nn_rhs_exact(x, m):
    hi, lo = _split(x)
    return _nn(hi, m) + _nn(lo, m)


def _nn_lhs_exact(m, x):
    hi, lo = _split(x)
    return _nn(m, hi) + _nn(m, lo)


def _nn_3pass(a, b):
    ah, al = _split(a)
    bh, bl = _split(b)
    return _nn(ah, bh) + (_nn(ah, bl) + _nn(al, bh))


def _iota(shape, axis):
    return lax.broadcasted_iota(jnp.int32, shape, axis)


def _head_block_matrix(n, value):
    same = (_iota((n, n), 0) >> HEAD_SHIFT) == (_iota((n, n), 1) >> HEAD_SHIFT)
    return jnp.where(same, value, 0.0).astype(BF16)


def _sigmoid(x):
    return 0.5 * jnp.tanh(0.5 * x) + 0.5


def _rmsnorm_rows(x, gain):
    return x * lax.rsqrt(jnp.mean(x * x, axis=-1, keepdims=True) + NORM_EPS) * gain


def _params(*sem):
    return pltpu.CompilerParams(dimension_semantics=sem, vmem_limit_bytes=VMEM_LIMIT)


def _inproj_kernel(x_ref, g_ref, w_ref, p_ref):
    h = _rmsnorm_rows(x_ref[...], g_ref[...])
    p_ref[...] = _nn(h.astype(BF16), w_ref[...])


def _inproj(x2, gain, w_bf16, tm=512):
    T, D = x2.shape
    N = w_bf16.shape[1]
    return pl.pallas_call(
        _inproj_kernel,
        grid=(T // tm,),
        in_specs=[pl.BlockSpec((tm, D), lambda i: (i, 0)),
                  pl.BlockSpec((1, D), lambda i: (0, 0)),
                  pl.BlockSpec((D, N), lambda i: (0, 0))],
        out_specs=pl.BlockSpec((tm, N), lambda i: (i, 0)),
        out_shape=jax.ShapeDtypeStruct((T, N), F32),
        compiler_params=_params("parallel"),
        name="inproj",
    )(x2, gain, w_bf16)


def _rw_kernel(p_ref, mu_ref, w0_ref, a0_ref, wa2_ref, g2_ref, kk_ref, ka_ref, rk_ref, lnw_ref, lnb_ref,
               o_ref, prev_ref, zt_ref, at_s, bt_s, kt_s, rt_s, qh_s, gmat_s, v_s, ecl_s, y0_s, y_s, bonus_s, g_s, hmat_s,
               *, ts, nbatch, group):
    C = RW_CHUNK
    W = WIDTH
    i = pl.program_id(0)

    @pl.when(i == 0)
    def _():
        prev_ref[...] = jnp.zeros_like(prev_ref)
        zt_ref[...] = jnp.zeros_like(zt_ref)

    lane = _iota((1, LANES), 1)
    row = _iota((ts, 1), 0)
    r4 = _iota((W, W), 0)
    c4 = _iota((W, W), 1)
    same_head = (r4 >> HEAD_SHIFT) == (c4 >> HEAD_SHIFT)
    tri4 = jnp.where(same_head & (c4 <= r4), 1.0, 0.0).astype(BF16)
    ones_bd = _head_block_matrix(W, 1.0)

    for b in range(nbatch):
        p = p_ref[b]
        shifted = jnp.where(row == 0, prev_ref[b, 0:1, :], pltpu.roll(p, 1, 0))
        prev_ref[b, 0:1, :] = p_ref[b, ts - 1:ts, :]
        pm = p + (shifted - p) * mu_ref[...]
        r = pm[:, 0:W]
        k = pm[:, W:2 * W]
        v = pm[:, 2 * W:3 * W]
        lora = pm[:, 3 * W:3 * W + LANES]
        pg = pm[:, 3 * W + LANES:]
        wa = _mm(jnp.where(lane < RW_DECAY_LORA, jnp.tanh(lora), lora), wa2_ref[...])
        lw = -RW_DECAY_SCALE * _sigmoid(w0_ref[...] + wa[:, 0:W])
        a = _sigmoid(a0_ref[...] + wa[:, W:])
        g_s[b] = _mm(_sigmoid(pg), g2_ref[...])
        kk = k * kk_ref[...]
        kk = kk * jnp.minimum(lax.rsqrt(_nn_rhs_exact(kk * kk, ones_bd)), 1.0 / RW_MIN_NORM)
        kp = k * (1.0 + (a - 1.0) * ka_ref[...])
        bonus_s[b] = _nn_rhs_exact(r * kp * rk_ref[...], ones_bd) * v
        cl = jnp.concatenate([_nn_lhs_exact(tri4, lw[n * W:(n + 1) * W, :]) for n in range(ts // W)], axis=0)
        ecl = jnp.exp(cl)
        eml = jnp.exp(-cl)
        at_s[b] = (-kk * jnp.exp(cl - lw)).astype(BF16)
        bt_s[b] = (kk * a * eml).astype(BF16)
        kt_s[b] = (kp * eml).astype(BF16)
        rt_s[b] = (r * ecl).astype(BF16)
        v_s[b] = v
        ecl_s[b] = ecl

    head = _iota((1, W), 1) >> HEAD_SHIFT
    tw = _iota((C, W), 0)
    sw = _iota((C, W), 1) & (C - 1)
    strict = tw > sw
    incl = tw >= sw
    eye = jnp.where(tw == sw, 1.0, 0.0)

    def same_block(size):
        return (tw & -size) == (sw & -size)

    def stack4(x):
        return jnp.concatenate([jnp.where(head == h, x, jnp.zeros_like(x)) for h in range(N_HEADS)], axis=0)

    def block_diag(x):
        return jnp.concatenate([x.astype(BF16)] * N_HEADS, axis=0) * ones_bd

    chunks = [(b, slice(c * C, (c + 1) * C)) for c in range(ts // C) for b in range(nbatch)]
    for g0 in range(0, len(chunks), group):
        grp = chunks[g0:g0 + group]
        n = range(len(grp))
        at = [at_s[b, rows, :] for b, rows in grp]
        rt = [rt_s[b, rows, :] for b, rows in grp]
        b4 = [stack4(bt_s[b, rows, :]) for b, rows in grp]
        k4 = [stack4(kt_s[b, rows, :]) for b, rows in grp]
        v4 = [stack4(v_s[b, rows, :].astype(BF16)) for b, rows in grp]
        lab = [jnp.where(strict, _nt(at[j], b4[j]), 0.0) for j in n]
        lak = [jnp.where(strict, _nt(at[j], k4[j]), 0.0) for j in n]
        mrb = [jnp.where(incl, _nt(rt[j], b4[j]), 0.0) for j in n]
        mrk = [jnp.where(incl, _nt(rt[j], k4[j]), 0.0) for j in n]
        m = [jnp.where(same_block(RW_BASE), lab[j], 0.0) for j in n]
        tinv = [eye + m[j] for j in n]
        for _ in range(2):
            m = [_nn(m[j].astype(BF16), block_diag(m[j])) for j in n]
            tinv = [tinv[j] + _nn(tinv[j].astype(BF16), block_diag(m[j])) for j in n]
        size = RW_BASE
        while size < C:
            below = same_block(2 * size) & jnp.logical_not(same_block(size))
            e_bd = [block_diag(jnp.where(below, lab[j], 0.0)) for j in n]
            half = [_nn(tinv[j].astype(BF16), e_bd[j]) for j in n]
            tinv = [tinv[j] + _nn(half[j].astype(BF16), block_diag(tinv[j])) for j in n]
            size *= 2
        tb = [tinv[j].astype(BF16) for j in n]
        mb = [mrb[j].astype(BF16) for j in n]
        w = [_nn(tb[j], stack4(at[j])) for j in n]
        u0 = [_nn(tb[j], stack4(_nn(lak[j].astype(BF16), v4[j])).astype(BF16)) for j in n]
        qh = [rt[j].astype(F32) + _nn(mb[j], stack4(w[j]).astype(BF16)) for j in n]
        y0 = [_nn(mb[j], stack4(u0[j]).astype(BF16)) + _nn(mrk[j].astype(BF16), v4[j]) for j in n]
        pad = jnp.zeros((C, W), F32)
        for j, (b, rows) in enumerate(grp):
            c = rows.start // C
            wt = jnp.concatenate([w[j], pad], axis=0).T.astype(BF16)
            uvt = jnp.concatenate([u0[j], v_s[b, rows, :]], axis=0).T.astype(BF16)
            bt = bt_s[b, rows, :]
            gmat = _nn(wt, jnp.concatenate([bt, jnp.zeros_like(bt)], axis=0))
            hmat = _nn(uvt, jnp.concatenate([bt, kt_s[b, rows, :]], axis=0))
            gmat_s[b, c] = jnp.where(same_head, gmat, 0.0).astype(BF16)
            hmat_s[b, c] = jnp.where(same_head, hmat, 0.0)
            qh_s[b, rows, :] = qh[j].astype(BF16)
            y0_s[b, rows, :] = y0[j]

    def chunk(c, carry):
        rows = pl.ds(pl.multiple_of(c * C, C), C)
        for b in range(nbatch):
            zt = zt_ref[b]
            ztb = zt.astype(BF16)
            y_s[b, rows, :] = _nt(qh_s[b, rows, :], ztb) + y0_s[b, rows, :]
            g_end = ecl_s[b, pl.ds(c * C + C - 1, 1), :]
            zt_ref[b] = (zt + _nn(ztb, gmat_s[b, c]) + hmat_s[b, c]) * g_end
        return carry

    lax.fori_loop(0, ts // C, chunk, 0)

    mean_bd = _head_block_matrix(W, 1.0 / HEAD_DIM)
    for b in range(nbatch):
        y = y_s[b]
        yc = y - _nn_rhs_exact(y, mean_bd)
        var = _nn_rhs_exact(yc * yc, mean_bd)
        y = yc * lax.rsqrt(var + RW_LN_EPS) * lnw_ref[...] + lnb_ref[...]
        o_ref[b] = (y + bonus_s[b]) * g_s[b]


def _rwkv(p, B, S, mu, w0, a0, wa2, g2, k_k, k_a, r_k, ln_w, ln_b, ts=512, group=8):
    ts = min(ts, S)
    W = WIDTH
    vec = lambda n: pl.BlockSpec((1, n), lambda i: (0, 0))
    full = lambda a: pl.BlockSpec(a.shape, lambda i: (0, 0))
    out = pl.pallas_call(
        functools.partial(_rw_kernel, ts=ts, nbatch=B, group=group),
        grid=(S // ts,),
        in_specs=[pl.BlockSpec((B, ts, 4 * W), lambda i: (0, i, COL_RW)),
                  vec(4 * W), vec(W), vec(W), full(wa2), full(g2), vec(W), vec(W), vec(W), vec(W), vec(W)],
        out_specs=pl.BlockSpec((B, ts, W), lambda i: (0, i, 0)),
        out_shape=jax.ShapeDtypeStruct((B, S, W), F32),
        scratch_shapes=[pltpu.VMEM((B, 8, 4 * W), F32), pltpu.VMEM((B, W, W), F32)]
                       + [pltpu.VMEM((B, ts, W), BF16)] * 5 + [pltpu.VMEM((B, ts // RW_CHUNK, W, W), BF16)]
                       + [pltpu.VMEM((B, ts, W), F32)] * 6 + [pltpu.VMEM((B, ts // RW_CHUNK, W, W), F32)],
        compiler_params=_params("arbitrary"),
        name="rwkv7",
    )(p.reshape(B, S, -1), mu, w0, a0, wa2, g2, k_k, k_a, r_k, ln_w, ln_b)
    return out.reshape(B * S, W)


def _sb_kernel(q_ref, k_ref, v_ref, qg_ref, kg_ref, o_ref, kb, vb, acc_o, acc_l, knorm,
               *, tq, tk, seq, prep, nsub):
    i = pl.program_id(2)
    lane = _iota((1, LANES), 1)
    mean_bd = _head_block_matrix(LANES, 1.0 / HEAD_DIM)
    ones = jnp.ones((LANES, LANES), BF16)

    @pl.when(i == 0)
    def _():
        knorm[...] = jnp.zeros_like(knorm)

        def body(c, carry):
            rows = pl.ds(pl.multiple_of(c * prep, prep), prep)
            kf = k_ref[rows, :]
            ms = _nn_rhs_exact(kf * kf, mean_bd)
            kn = (kf * lax.rsqrt(ms + NORM_EPS) * kg_ref[...]).astype(BF16)
            kb[rows, :] = kn
            vb[rows, :] = v_ref[rows, :].astype(BF16)
            kn = kn.astype(F32)
            sq = _nn_rhs_exact(kn * kn, ones)
            knorm[...] = jnp.maximum(knorm[...], jnp.max(sq.reshape(prep // 8, 8, LANES), axis=0))
            return carry
        lax.fori_loop(0, seq // prep, body, 0)

    subs = range(nsub)
    tile = [i * nsub + s for s in subs]
    q = q_ref[...]
    ms = _nn_rhs_exact(q * q, mean_bd)
    qn = q * lax.rsqrt(ms + NORM_EPS) * qg_ref[...] * (HEAD_DIM ** -0.5 * LOG2_E)
    q_lo = jnp.where(lane < HEAD_DIM, qn, 0.0).astype(BF16)
    q_hi = jnp.where(lane >= HEAD_DIM, qn, 0.0).astype(BF16)
    q2 = [jnp.concatenate([q_lo[s * tq:(s + 1) * tq], q_hi[s * tq:(s + 1) * tq]], axis=0) for s in subs]

    qf = (q_lo + q_hi).astype(F32)
    qsq = jnp.max(_nn_rhs_exact(qf * qf, ones))
    zmax = jnp.sqrt(qsq * jnp.max(knorm[...])) * SB_NORM_MARGIN
    stop_at = SB_ZERO_EXPONENT + zmax * BF16_ROUNDING

    def softplus2(z2):
        return jnp.maximum(z2, 0.0) + jnp.log2(1.0 + jnp.exp2(-jnp.abs(z2)))

    def rev_incl(n):
        return jnp.where(_iota((n, n), 0) >= _iota((n, n), 1), 1.0, 0.0).astype(BF16)

    prev = [pl.ds(pl.multiple_of(jnp.maximum(tile[s] - 1, 0) * tk, tk), tk) for s in subs]
    diag = [pl.ds(pl.multiple_of(tile[s] * tk, tk), tk) for s in subs]
    col = _iota((2 * tq, 2 * tk), 1)
    causal = (col - tk) < (_iota((2 * tq, 2 * tk), 0) & (tq - 1))
    valid = [causal & (col >= jnp.where(i > 0, 0, tk))] + [causal] * (nsub - 1)
    rev = rev_incl(tk)
    z2 = [_nt(q2[s], jnp.concatenate([kb[prev[s], :], kb[diag[s], :]], axis=0)) for s in subs]
    sp = [jnp.where(valid[s], softplus2(z2[s]), 0.0).astype(BF16) for s in subs]
    cum_diag = [_nn(sp[s][:, tk:], rev) for s in subs]
    cum = [jnp.concatenate([_nn(sp[s][:, :tk], rev) + cum_diag[s][:, 0:1], cum_diag[s]], axis=1) for s in subs]
    attn = [jnp.where(valid[s], jnp.exp2(z2[s] - cum[s]), 0.0).astype(BF16) for s in subs]
    for s in subs:
        acc_o[s] = _nn(attn[s], jnp.concatenate([vb[prev[s], :], vb[diag[s], :]], axis=0))
        acc_l[s] = jnp.broadcast_to(cum[s][:, 0:1], (2 * tq, LANES))

    for s in subs:
        def step(j, s=s):
            rows = pl.ds(pl.multiple_of(j * tk, tk), tk)
            z2 = _nt(q2[s], kb[rows, :])
            cum = _nn(softplus2(z2).astype(BF16), rev)
            seen = acc_l[s]
            attn = jnp.exp2(z2 - cum - jnp.concatenate([seen] * (tk // LANES), axis=1))
            acc_o[s] += _nn(attn.astype(BF16), vb[rows, :])
            seen = seen + cum[:, 0:1]
            acc_l[s] = seen
            return jnp.min(seen)

        def cond(carry, s=s):
            t, low = carry
            return (t < tile[s]) & (low < stop_at)

        def body(carry, s=s, step=step):
            t, _ = carry
            return t + 1, step(tile[s] - 1 - t)
        lax.while_loop(cond, body, (1, jnp.min(cum[s][:, 0:1])))

    for s in subs:
        o_ref[s * tq:(s + 1) * tq, :] = jnp.where(lane < HEAD_DIM, acc_o[s, 0:tq, :], acc_o[s, tq:2 * tq, :])


def _stick_breaking(p, B, S, qg, kg, tq=256, tk=256, nsub=2):
    assert tq == tk and S % (nsub * tq) == 0
    nq = S // (nsub * tq)
    T = B * S
    prep = min(512, S)
    return pl.pallas_call(
        functools.partial(_sb_kernel, tq=tq, tk=tk, seq=S, prep=prep, nsub=nsub),
        grid=(B, 2, nq),
        in_specs=[pl.BlockSpec((nsub * tq, LANES), lambda b, h, i: (b * nq + i, COL_SB_Q + h)),
                  pl.BlockSpec((S, LANES), lambda b, h, i: (b, COL_SB_K + h)),
                  pl.BlockSpec((S, LANES), lambda b, h, i: (b, COL_SB_V + h)),
                  pl.BlockSpec((1, LANES), lambda b, h, i: (0, 0)),
                  pl.BlockSpec((1, LANES), lambda b, h, i: (0, 0))],
        out_specs=pl.BlockSpec((nsub * tq, LANES), lambda b, h, i: (b * nq + i, h)),
        out_shape=jax.ShapeDtypeStruct((T, WIDTH), F32),
        scratch_shapes=[pltpu.VMEM((S, LANES), BF16), pltpu.VMEM((S, LANES), BF16),
                        pltpu.VMEM((nsub, 2 * tq, LANES), F32), pltpu.VMEM((nsub, 2 * tq, LANES), F32),
                        pltpu.VMEM((8, LANES), F32)],
        compiler_params=_params("arbitrary", "arbitrary", "arbitrary"),
        name="stick_breaking",
    )(p, p, p, qg, kg)


def _ret_kernel(q_ref, k_ref, v_ref, gt_ref, cos_ref, sin_ref, gn_ref, dm_ref, kdec_ref, qdec_ref, cdec_ref,
                o_ref, st_ref, *, ts):
    C = RET_CHUNK
    nch = ts // C
    pairs = range(2)
    chunks = range(nch)
    i = pl.program_id(1)

    @pl.when(i == 0)
    def _():
        st_ref[...] = jnp.zeros_like(st_ref)

    lane = _iota((1, LANES), 1)
    half = HEAD_DIM // 2
    first_half = (lane & (HEAD_DIM - 1)) < half
    same_head = (_iota((LANES, LANES), 0) >> HEAD_SHIFT) == (_iota((LANES, LANES), 1) >> HEAD_SHIFT)
    mean_bd = _head_block_matrix(LANES, 1.0 / HEAD_DIM)
    cos = cos_ref[...]
    sin = sin_ref[...]

    def rotary(t):
        swapped = jnp.where(first_half, pltpu.roll(t, LANES - half, 1), pltpu.roll(t, half, 1))
        return t * cos + swapped * sin

    def rows(t, c):
        return t[c * C:(c + 1) * C]

    cols = [slice(h * LANES, (h + 1) * LANES) for h in pairs]
    q = [rotary(q_ref[:, cols[h]]) for h in pairs]
    k = [rotary(k_ref[:, cols[h]]) * (HEAD_DIM ** -0.5) for h in pairs]
    vb = [v_ref[:, cols[h]].astype(BF16) for h in pairs]
    kb = [k[h].astype(BF16) for h in pairs]
    q_lo = [jnp.where(lane < HEAD_DIM, q[h], 0.0).astype(BF16) for h in pairs]
    q_hi = [jnp.where(lane >= HEAD_DIM, q[h], 0.0).astype(BF16) for h in pairs]
    qd = [(q[h] * jnp.concatenate([qdec_ref[h]] * nch, axis=0)).astype(BF16) for h in pairs]
    kd = [k[h] * jnp.concatenate([kdec_ref[h]] * nch, axis=0) for h in pairs]

    hc = [(h, c) for h in pairs for c in chunks]
    scores = {(h, c): _nt(jnp.concatenate([rows(q_lo[h], c), rows(q_hi[h], c)], axis=0), rows(kb[h], c)) * dm_ref[h]
              for h, c in hc}
    kv = {(h, c): _nn(rows(kd[h], c).T.astype(BF16), rows(vb[h], c)) for h, c in hc}
    intra2 = {(h, c): _nn(scores[h, c].astype(BF16), rows(vb[h], c)) for h, c in hc}
    state = {}
    for h in pairs:
        st = st_ref[h]
        for c in chunks:
            state[h, c] = st
            st = cdec_ref[h] * st + jnp.where(same_head, kv[h, c], 0.0)
        st_ref[h] = st
    inter = {(h, c): _nn(rows(qd[h], c), state[h, c].astype(BF16)) for h, c in hc}
    for h in pairs:
        o = jnp.concatenate([jnp.where(lane < HEAD_DIM, intra2[h, c][0:C], intra2[h, c][C:2 * C]) + inter[h, c]
                             for c in chunks], axis=0)
        oc = o - _nn_rhs_exact(o, mean_bd)
        var = _nn_rhs_exact(oc * oc, mean_bd)
        on = oc * lax.rsqrt(var + RET_GN_EPS) * gn_ref[:, cols[h]]
        gt = gt_ref[:, cols[h]]
        o_ref[:, cols[h]] = on * (gt * _sigmoid(gt))


def _retention_tables(S):
    C = RET_CHUNK
    inv_freq = ROPE_BASE ** (-jnp.arange(0, HEAD_DIM, 2, dtype=F32) / HEAD_DIM)
    ang = jnp.arange(S, dtype=F32)[:, None] * inv_freq[None, :]
    cos = jnp.tile(jnp.cos(ang), (1, 4))
    sin = jnp.tile(jnp.concatenate([-jnp.sin(ang), jnp.sin(ang)], axis=1), (1, 2))
    log_gamma = jnp.log(1.0 - 2.0 ** (-5.0 - jnp.arange(N_HEADS, dtype=F32)))
    idx = jnp.arange(C, dtype=F32)
    rel = idx[:, None] - idx[None, :]
    intra = jnp.where(rel >= 0, jnp.exp(jnp.maximum(rel, 0.0) * log_gamma[:, None, None]), 0.0)
    dm = intra.reshape(2, 2 * C, C)
    lanes = lambda t: jnp.repeat(t.reshape(2, 2, C).transpose(0, 2, 1), HEAD_DIM, axis=2)
    kdec = lanes(jnp.exp((C - 1 - idx)[None, :] * log_gamma[:, None]))
    qdec = lanes(jnp.exp((idx + 1.0)[None, :] * log_gamma[:, None]))
    cdec = jnp.repeat(jnp.exp(C * log_gamma).reshape(2, 1, 2), HEAD_DIM, axis=2)
    return cos, sin, dm, kdec, qdec, cdec


def _retention(p, B, S, gn, tables, ts=512):
    ts = min(ts, S)
    nb = S // ts
    T = B * S
    C = RET_CHUNK
    cos, sin, dm, kdec, qdec, cdec = tables
    col = lambda c0: pl.BlockSpec((ts, WIDTH), lambda b, i: (b * nb + i, c0 // 2))
    tab = pl.BlockSpec((ts, LANES), lambda b, i: (i, 0))
    full = lambda a: pl.BlockSpec(a.shape, lambda b, i: (0,) * a.ndim)
    return pl.pallas_call(
        functools.partial(_ret_kernel, ts=ts),
        grid=(B, nb),
        in_specs=[col(COL_RET_Q), col(COL_RET_K), col(COL_RET_V), col(COL_RET_G), tab, tab,
                  full(gn), full(dm), full(kdec), full(qdec), full(cdec)],
        out_specs=pl.BlockSpec((ts, WIDTH), lambda b, i: (b * nb + i, 0)),
        out_shape=jax.ShapeDtypeStruct((T, WIDTH), F32),
        scratch_shapes=[pltpu.VMEM((2, LANES, LANES), F32)],
        compiler_params=_params("arbitrary", "arbitrary"),
        name="retention",
    )(p, p, p, p, cos, sin, gn, dm, kdec, qdec, cdec)


def _conv_kernel(ua_ref, ub_ref, dw_ref, db_ref, lnw_ref, lnb_ref, o_ref, buf, shifted, *, ts):
    i = pl.program_id(1)
    span = ts + CONV_HALO - SUBLANES

    @pl.when(i == 0)
    def _():
        buf[0:CONV_HALO, :] = jnp.zeros((CONV_HALO, WIDTH), F32)

    buf[CONV_HALO:, :] = ua_ref[...] * _sigmoid(ub_ref[...])
    for s in range(1, SUBLANES):
        shifted[s] = buf[s:s + span, :]
    acc = jnp.zeros((ts, WIDTH), F32) + db_ref[...]
    for j in range(CONV_WIDTH):
        start = CONV_HALO - (CONV_WIDTH - 1) + j
        s = start % SUBLANES
        window = buf[start:start + ts, :] if s == 0 else shifted[s, start - s:start - s + ts, :]
        acc = acc + dw_ref[j:j + 1, :] * window
    buf[0:CONV_HALO, :] = buf[ts:ts + CONV_HALO, :]
    xc = acc - jnp.mean(acc, axis=-1, keepdims=True)
    var = jnp.mean(xc * xc, axis=-1, keepdims=True)
    y = xc * lax.rsqrt(var + CONV_LN_EPS) * lnw_ref[...] + lnb_ref[...]
    o_ref[...] = y * _sigmoid(y)


def _conformer_conv(p, B, S, dw, db, ln_w, ln_b, ts=512):
    ts = min(ts, S)
    nb = S // ts
    T = B * S
    vec = pl.BlockSpec((1, WIDTH), lambda b, i: (0, 0))
    return pl.pallas_call(
        functools.partial(_conv_kernel, ts=ts),
        grid=(B, nb),
        in_specs=[pl.BlockSpec((ts, WIDTH), lambda b, i: (b * nb + i, COL_CONV_A)),
                  pl.BlockSpec((ts, WIDTH), lambda b, i: (b * nb + i, COL_CONV_B)),
                  pl.BlockSpec((CONV_HALO, WIDTH), lambda b, i: (0, 0)), vec, vec, vec],
        out_specs=pl.BlockSpec((ts, WIDTH), lambda b, i: (b * nb + i, 0)),
        out_shape=jax.ShapeDtypeStruct((T, WIDTH), F32),
        scratch_shapes=[pltpu.VMEM((ts + CONV_HALO, WIDTH), F32),
                        pltpu.VMEM((SUBLANES, ts + CONV_HALO - SUBLANES, WIDTH), F32)],
        compiler_params=_params("arbitrary", "arbitrary"),
        name="conformer_conv",
    )(p, p, dw, db, ln_w, ln_b)


def _merge_kernel(x_ref, g_ref, y0_ref, y1_ref, y2_ref, y3_ref, wg_ref, wb_ref, wo_ref, o_ref):
    x = x_ref[...]
    h = _rmsnorm_rows(x, g_ref[...]).astype(BF16)
    merged = None
    for n, y_ref in enumerate((y0_ref, y1_ref, y2_ref, y3_ref)):
        term = _sigmoid(_nn(h, wg_ref[n])) * _nn(y_ref[...].astype(BF16), wb_ref[n])
        merged = term if merged is None else merged + term
    o_ref[...] = x + _nn(merged.astype(BF16), wo_ref[...])


def _merge(x2, gain, ys, wg, wb, wo, tm=512):
    T, D = x2.shape
    tm = min(tm, T)
    row = lambda n: pl.BlockSpec((tm, n), lambda i: (i, 0))
    return pl.pallas_call(
        _merge_kernel,
        grid=(T // tm,),
        in_specs=[row(D), pl.BlockSpec((1, D), lambda i: (0, 0)), row(WIDTH), row(WIDTH), row(WIDTH), row(WIDTH),
                  pl.BlockSpec(wg.shape, lambda i: (0, 0, 0)),
                  pl.BlockSpec(wb.shape, lambda i: (0, 0, 0)),
                  pl.BlockSpec(wo.shape, lambda i: (0, 0))],
        out_specs=row(D),
        out_shape=jax.ShapeDtypeStruct((T, D), F32),
        compiler_params=_params("parallel"),
        name="merge",
    )(x2, gain, *ys, wg, wb, wo)


def _ffn_kernel(x_ref, g_ref, w1_ref, w3_ref, w2_ref, o_ref, h_s):
    f = pl.program_id(1)

    @pl.when(f == 0)
    def _():
        x = x_ref[...]
        h_s[...] = _rmsnorm_rows(x, g_ref[...]).astype(BF16)
        o_ref[...] = x

    h = h_s[...]
    a = _nn(h, w1_ref[...])
    b = _nn(h, w3_ref[...])
    o_ref[...] += _nn((a * _sigmoid(a) * b).astype(BF16), w2_ref[...])


def _ffn(x2, gain, w1, w3, w2, tm=1024, tf=1408):
    T, D = x2.shape
    tm = min(tm, T)
    nf = w1.shape[1] // tf
    return pl.pallas_call(
        _ffn_kernel,
        grid=(T // tm, nf),
        in_specs=[pl.BlockSpec((tm, D), lambda i, f: (i, 0)),
                  pl.BlockSpec((1, D), lambda i, f: (0, 0)),
                  pl.BlockSpec((D, tf), lambda i, f: (0, f)),
                  pl.BlockSpec((D, tf), lambda i, f: (0, f)),
                  pl.BlockSpec((tf, D), lambda i, f: (f, 0))],
        out_specs=pl.BlockSpec((tm, D), lambda i, f: (i, 0)),
        out_shape=jax.ShapeDtypeStruct((T, D), F32),
        scratch_shapes=[pltpu.VMEM((tm, D), BF16)],
        compiler_params=_params("parallel", "arbitrary"),
        name="ffn",
    )(x2, gain, w1, w3, w2)


def _moe_kernel(x_ref, g_ref, rt_ref, w1_ref, w3_ref, w2_ref, o_ref, h_s, gate_s, rank_s, rank_t_s, count_s,
                *, tm, cap, small):
    e = pl.program_id(1)
    lane = _iota((1, LANES), 1)

    @pl.when(e == 0)
    def _():
        x = x_ref[...]
        h = _rmsnorm_rows(x, g_ref[...])
        h_s[...] = h.astype(BF16)
        o_ref[...] = x
        logits = jnp.where(lane < N_EXPERTS, _nn_3pass(h, rt_ref[...]), -jnp.inf)
        lane_f = lane.astype(F32)
        m1 = jnp.max(logits, axis=-1, keepdims=True)
        i1 = jnp.min(jnp.where(logits == m1, lane_f, float(LANES)), axis=-1, keepdims=True)
        rest = jnp.where(lane_f == i1, -jnp.inf, logits)
        m2 = jnp.max(rest, axis=-1, keepdims=True)
        i2 = jnp.min(jnp.where(rest == m2, lane_f, float(LANES)), axis=-1, keepdims=True)
        e2 = jnp.exp(m2 - m1)
        den = 1.0 + e2
        gate_s[...] = jnp.where(lane_f == i1, 1.0 / den, 0.0) + jnp.where(lane_f == i2, e2 / den, 0.0)
        chosen = jnp.where((lane_f == i1) | (lane_f == i2), 1.0, 0.0)
        before = jnp.where(_iota((tm, tm), 1) < _iota((tm, tm), 0), 1.0, 0.0).astype(BF16)
        rank = jnp.where(chosen > 0.0, _nn(before, chosen.astype(BF16)), -1.0)
        rank_s[...] = rank
        rank_t_s[...] = rank.T
        count_s[...] = jnp.sum(chosen, axis=0, keepdims=True)

    mine = lane == e
    gate = jnp.sum(jnp.where(mine, gate_s[...], 0.0), axis=-1, keepdims=True)
    rank_col = jnp.sum(jnp.where(mine, rank_s[...], 0.0), axis=-1, keepdims=True)
    rank_row = rank_t_s[pl.ds(e, 1), :]
    count = jnp.sum(jnp.where(mine, count_s[...], 0.0))
    def one_pass(first, rows):
        lanes = -(-rows // LANES) * LANES
        slot_rows = _iota((rows, tm), 0).astype(F32)
        slot_cols = _iota((tm, lanes), 1).astype(F32)
        take = jnp.where(rank_row - first == slot_rows, 1.0, 0.0).astype(BF16)
        xg = _nn(take, h_s[...]).astype(BF16)
        a = _nn(xg, w1_ref[0])
        b = _nn(xg, w3_ref[0])
        y = _nn((a * _sigmoid(a) * b).astype(BF16), w2_ref[0]).astype(BF16)
        hit = rank_col - first == slot_cols
        if lanes > rows:
            y = jnp.concatenate([y, jnp.zeros((lanes - rows, y.shape[1]), BF16)], axis=0)
            hit = hit & (slot_cols < rows)
        put = jnp.where(hit, 1.0, 0.0).astype(BF16)
        o_ref[...] += gate * _nn(put, y)

    @pl.when((count > 0.0) & (count <= small))
    def _():
        one_pass(jnp.float32(0.0), small)

    @pl.when(count > small)
    def _():
        def more(first):
            one_pass(first, cap)
            return first + cap
        lax.while_loop(lambda first: first < count, more, jnp.float32(0.0))


def _moe(x2, gain, router_pad, w1, w3, w2, tm=1024, cap=320, small=256):
    T, D = x2.shape
    tm = min(tm, T)
    cap = min(cap, tm)
    small = min(small, cap)
    E, _, F = w1.shape
    return pl.pallas_call(
        functools.partial(_moe_kernel, tm=tm, cap=cap, small=small),
        grid=(T // tm, E),
        in_specs=[pl.BlockSpec((tm, D), lambda i, e: (i, 0)),
                  pl.BlockSpec((1, D), lambda i, e: (0, 0)),
                  pl.BlockSpec((D, LANES), lambda i, e: (0, 0)),
                  pl.BlockSpec((1, D, F), lambda i, e: (e, 0, 0)),
                  pl.BlockSpec((1, D, F), lambda i, e: (e, 0, 0)),
                  pl.BlockSpec((1, F, D), lambda i, e: (e, 0, 0))],
        out_specs=pl.BlockSpec((tm, D), lambda i, e: (i, 0)),
        out_shape=jax.ShapeDtypeStruct((T, D), F32),
        scratch_shapes=[pltpu.VMEM((tm, D), BF16), pltpu.VMEM((tm, LANES), F32), pltpu.VMEM((tm, LANES), F32),
                        pltpu.VMEM((LANES, tm), F32), pltpu.VMEM((1, LANES), F32)],
        compiler_params=_params("parallel", "arbitrary"),
        name="moe",
    )(x2, gain, router_pad, w1, w3, w2)


def kernel(x, norm_mix, w_in, rw_mu, rw_w0, rw_w2, rw_a0, rw_a2, rw_g2, rw_k_k, rw_k_a, rw_r_k, rw_ln_w, rw_ln_b, sb_q_norm, sb_k_norm, ret_gn, conv_dw, conv_b, conv_ln_w, conv_ln_b, w_gate, w_branch, w_out, norm_ffn, ffn_w1, ffn_w3, ffn_w2, router, moe_w1, moe_w3, moe_w2):
    B, S, D = x.shape
    depth = norm_mix.shape[0]
    x2 = x.reshape(B * S, D)
    tables = _retention_tables(S)
    row = lambda t: t.reshape(1, -1)
    for l in range(depth):
        p = _inproj(x2, row(norm_mix[l]), w_in[l].astype(BF16))
        zeros = jnp.zeros_like(rw_w2[l])
        wa2 = jnp.concatenate([jnp.concatenate([rw_w2[l], zeros], axis=1),
                               jnp.concatenate([zeros, rw_a2[l]], axis=1)], axis=0).astype(BF16)
        y_rw = _rwkv(p, B, S, row(rw_mu[l]), row(rw_w0[l]), row(rw_a0[l]), wa2, rw_g2[l].astype(BF16),
                     row(rw_k_k[l]), row(rw_k_a[l]), row(rw_r_k[l]), row(rw_ln_w[l]), row(rw_ln_b[l]))
        y_sb = _stick_breaking(p, B, S, row(jnp.tile(sb_q_norm[l], 2)), row(jnp.tile(sb_k_norm[l], 2)))
        y_ret = _retention(p, B, S, row(ret_gn[l]), tables)
        dw = jnp.concatenate([conv_dw[l], jnp.zeros((CONV_HALO - CONV_WIDTH, WIDTH), F32)], axis=0)
        y_conv = _conformer_conv(p, B, S, dw, row(conv_b[l]), row(conv_ln_w[l]), row(conv_ln_b[l]))
        x2 = _merge(x2, row(norm_mix[l]), (y_rw, y_sb, y_ret, y_conv),
                    w_gate[l].astype(BF16), w_branch[l].astype(BF16), w_out[l].astype(BF16))
        if l % 2 == 0:
            x2 = _ffn(x2, row(norm_ffn[l]), ffn_w1[l // 2].astype(BF16), ffn_w3[l // 2].astype(BF16),
                      ffn_w2[l // 2].astype(BF16))
        else:
            rt = jnp.concatenate([router[l // 2], jnp.zeros((D, LANES - N_EXPERTS), F32)], axis=1)
            x2 = _moe(x2, row(norm_ffn[l]), rt, moe_w1[l // 2].astype(BF16), moe_w3[l // 2].astype(BF16),
                      moe_w2[l // 2].astype(BF16))
    return x2.reshape(B, S, D)
```

```python
import functools

import jax
import jax.numpy as jnp
from jax import lax
from jax.experimental import pallas as pl
from jax.experimental.pallas import tpu as pltpu

F32 = jnp.float32
BF16 = jnp.bfloat16

LANES = 128
SUBLANES = 8
VMEM_LIMIT = 56 * 1024 * 1024

D_MODEL = 1024
HEAD_DIM = 64
HEAD_SHIFT = HEAD_DIM.bit_length() - 1
N_HEADS = 4
WIDTH = N_HEADS * HEAD_DIM
NORM_EPS = 1e-6
LOG2_E = 1.4426950408889634
RW_DECAY_SCALE = 0.606531
RW_LN_EPS = 64e-5
RW_CHUNK = 64
RW_BASE = 8
RW_MIN_NORM = 1e-12
RW_DECAY_LORA = 64
BF16_ROUNDING = 2.0 ** -8
SB_ZERO_EXPONENT = 152.0
SB_NORM_MARGIN = 1.01
RET_CHUNK = 128
RET_GN_EPS = 1e-5
ROPE_BASE = 10000.0
CONV_WIDTH = 31
CONV_HALO = 32
CONV_LN_EPS = 1e-5
N_EXPERTS = 8
N_IN = 3328
COL_RW = 0
COL_SB_Q, COL_SB_K, COL_SB_V = 8, 10, 12
COL_RET_Q, COL_RET_K, COL_RET_V, COL_RET_G = 14, 16, 18, 20
COL_CONV_A, COL_CONV_B = 11, 12


def _nn(a, b):
    return lax.dot_general(a, b, (((1,), (0,)), ((), ())), preferred_element_type=F32)


def _nt(a, b):
    return lax.dot_general(a, b, (((1,), (1,)), ((), ())), preferred_element_type=F32)


def _mm(a, b):
    return _nn(a.astype(BF16), b.astype(BF16))


def _split(x):
    hi = x.astype(BF16)
    lo = (x - hi.astype(F32)).astype(BF16)
    return hi, lo


def _nn_rhs_exact(x, m):
    hi, lo = _split(x)
    return _nn(hi, m) + _nn(lo, m)


def _nn_lhs_exact(m, x):
    hi, lo = _split(x)
    return _nn(m, hi) + _nn(m, lo)


def _nn_3pass(a, b):
    ah, al = _split(a)
    bh, bl = _split(b)
    return _nn(ah, bh) + (_nn(ah, bl) + _nn(al, bh))


def _iota(shape, axis):
    return lax.broadcasted_iota(jnp.int32, shape, axis)


def _head_block_matrix(n, value):
    same = (_iota((n, n), 0) >> HEAD_SHIFT) == (_iota((n, n), 1) >> HEAD_SHIFT)
    return jnp.where(same, value, 0.0).astype(BF16)


def _sigmoid(x):
    return 0.5 * jnp.tanh(0.5 * x) + 0.5


def _rmsnorm_rows(x, gain):
    return x * lax.rsqrt(jnp.mean(x * x, axis=-1, keepdims=True) + NORM_EPS) * gain


def _params(*sem):
    return pltpu.CompilerParams(dimension_semantics=sem, vmem_limit_bytes=VMEM_LIMIT)


def _inproj_kernel(x_ref, g_ref, w_ref, p_ref):
    h = _rmsnorm_rows(x_ref[...], g_ref[...])
    p_ref[...] = _nn(h.astype(BF16), w_ref[...])


def _inproj(x2, gain, w_bf16, tm=512):
    T, D = x2.shape
    N = w_bf16.shape[1]
    return pl.pallas_call(
        _inproj_kernel,
        grid=(T // tm,),
        in_specs=[pl.BlockSpec((tm, D), lambda i: (i, 0)),
                  pl.BlockSpec((1, D), lambda i: (0, 0)),
                  pl.BlockSpec((D, N), lambda i: (0, 0))],
        out_specs=pl.BlockSpec((tm, N), lambda i: (i, 0)),
        out_shape=jax.ShapeDtypeStruct((T, N), F32),
        compiler_params=_params("parallel"),
        name="inproj",
    )(x2, gain, w_bf16)


def _rw_kernel(p_ref, mu_ref, w0_ref, a0_ref, wa2_ref, g2_ref, kk_ref, ka_ref, rk_ref, lnw_ref, lnb_ref,
               o_ref, prev_ref, zt_ref, at_s, bt_s, kt_s, rt_s, qh_s, gmat_s, v_s, ecl_s, y0_s, y_s, bonus_s, g_s, hmat_s,
               *, ts, nbatch, group):
    C = RW_CHUNK
    W = WIDTH
    i = pl.program_id(0)

    @pl.when(i == 0)
    def _():
        prev_ref[...] = jnp.zeros_like(prev_ref)
        zt_ref[...] = jnp.zeros_like(zt_ref)

    lane = _iota((1, LANES), 1)
    row = _iota((ts, 1), 0)
    r4 = _iota((W, W), 0)
    c4 = _iota((W, W), 1)
    same_head = (r4 >> HEAD_SHIFT) == (c4 >> HEAD_SHIFT)
    tri4 = jnp.where(same_head & (c4 <= r4), 1.0, 0.0).astype(BF16)
    ones_bd = _head_block_matrix(W, 1.0)

    for b in range(nbatch):
        p = p_ref[b]
        shifted = jnp.where(row == 0, prev_ref[b, 0:1, :], pltpu.roll(p, 1, 0))
        prev_ref[b, 0:1, :] = p_ref[b, ts - 1:ts, :]
        pm = p + (shifted - p) * mu_ref[...]
        r = pm[:, 0:W]
        k = pm[:, W:2 * W]
        v = pm[:, 2 * W:3 * W]
        lora = pm[:, 3 * W:3 * W + LANES]
        pg = pm[:, 3 * W + LANES:]
        wa = _mm(jnp.where(lane < RW_DECAY_LORA, jnp.tanh(lora), lora), wa2_ref[...])
        lw = -RW_DECAY_SCALE * _sigmoid(w0_ref[...] + wa[:, 0:W])
        a = _sigmoid(a0_ref[...] + wa[:, W:])
        g_s[b] = _mm(_sigmoid(pg), g2_ref[...])
        kk = k * kk_ref[...]
        kk = kk * jnp.minimum(lax.rsqrt(_nn_rhs_exact(kk * kk, ones_bd)), 1.0 / RW_MIN_NORM)
        kp = k * (1.0 + (a - 1.0) * ka_ref[...])
        bonus_s[b] = _nn_rhs_exact(r * kp * rk_ref[...], ones_bd) * v
        cl = jnp.concatenate([_nn_lhs_exact(tri4, lw[n * W:(n + 1) * W, :]) for n in range(ts // W)], axis=0)
        ecl = jnp.exp(cl)
        eml = jnp.exp(-cl)
        at_s[b] = (-kk * jnp.exp(cl - lw)).astype(BF16)
        bt_s[b] = (kk * a * eml).astype(BF16)
        kt_s[b] = (kp * eml).astype(BF16)
        rt_s[b] = (r * ecl).astype(BF16)
        v_s[b] = v
        ecl_s[b] = ecl

    head = _iota((1, W), 1) >> HEAD_SHIFT
    tw = _iota((C, W), 0)
    sw = _iota((C, W), 1) & (C - 1)
    strict = tw > sw
    incl = tw >= sw
    eye = jnp.where(tw == sw, 1.0, 0.0)

    def same_block(size):
        return (tw & -size) == (sw & -size)

    def stack4(x):
        return jnp.concatenate([jnp.where(head == h, x, jnp.zeros_like(x)) for h in range(N_HEADS)], axis=0)

    def block_diag(x):
        return jnp.concatenate([x.astype(BF16)] * N_HEADS, axis=0) * ones_bd

    chunks = [(b, slice(c * C, (c + 1) * C)) for c in range(ts // C) for b in range(nbatch)]
    for g0 in range(0, len(chunks), group):
        grp = chunks[g0:g0 + group]
        n = range(len(grp))
        at = [at_s[b, rows, :] for b, rows in grp]
        rt = [rt_s[b, rows, :] for b, rows in grp]
        b4 = [stack4(bt_s[b, rows, :]) for b, rows in grp]
        k4 = [stack4(kt_s[b, rows, :]) for b, rows in grp]
        v4 = [stack4(v_s[b, rows, :].astype(BF16)) for b, rows in grp]
        lab = [jnp.where(strict, _nt(at[j], b4[j]), 0.0) for j in n]
        lak = [jnp.where(strict, _nt(at[j], k4[j]), 0.0) for j in n]
        mrb = [jnp.where(incl, _nt(rt[j], b4[j]), 0.0) for j in n]
        mrk = [jnp.where(incl, _nt(rt[j], k4[j]), 0.0) for j in n]
        m = [jnp.where(same_block(RW_BASE), lab[j], 0.0) for j in n]
        tinv = [eye + m[j] for j in n]
        for _ in range(2):
            m = [_nn(m[j].astype(BF16), block_diag(m[j])) for j in n]
            tinv = [tinv[j] + _nn(tinv[j].astype(BF16), block_diag(m[j])) for j in n]
        size = RW_BASE
        while size < C:
            below = same_block(2 * size) & jnp.logical_not(same_block(size))
            e_bd = [block_diag(jnp.where(below, lab[j], 0.0)) for j in n]
            half = [_nn(tinv[j].astype(BF16), e_bd[j]) for j in n]
            tinv = [tinv[j] + _nn(half[j].astype(BF16), block_diag(tinv[j])) for j in n]
            size *= 2
        tb = [tinv[j].astype(BF16) for j in n]
        mb = [mrb[j].astype(BF16) for j in n]
        w = [_nn(tb[j], stack4(at[j])) for j in n]
        u0 = [_nn(tb[j], stack4(_nn(lak[j].astype(BF16), v4[j])).astype(BF16)) for j in n]
        qh = [rt[j].astype(F32) + _nn(mb[j], stack4(w[j]).astype(BF16)) for j in n]
        y0 = [_nn(mb[j], stack4(u0[j]).astype(BF16)) + _nn(mrk[j].astype(BF16), v4[j]) for j in n]
        pad = jnp.zeros((C, W), F32)
        for j, (b, rows) in enumerate(grp):
            c = rows.start // C
            wt = jnp.concatenate([w[j], pad], axis=0).T.astype(BF16)
            uvt = jnp.concatenate([u0[j], v_s[b, rows, :]], axis=0).T.astype(BF16)
            bt = bt_s[b, rows, :]
            gmat = _nn(wt, jnp.concatenate([bt, jnp.zeros_like(bt)], axis=0))
            hmat = _nn(uvt, jnp.concatenate([bt, kt_s[b, rows, :]], axis=0))
            gmat_s[b, c] = jnp.where(same_head, gmat, 0.0).astype(BF16)
            hmat_s[b, c] = jnp.where(same_head, hmat, 0.0)
            qh_s[b, rows, :] = qh[j].astype(BF16)
            y0_s[b, rows, :] = y0[j]

    def chunk(c, carry):
        rows = pl.ds(pl.multiple_of(c * C, C), C)
        for b in range(nbatch):
            zt = zt_ref[b]
            ztb = zt.astype(BF16)
            y_s[b, rows, :] = _nt(qh_s[b, rows, :], ztb) + y0_s[b, rows, :]
            g_end = ecl_s[b, pl.ds(c * C + C - 1, 1), :]
            zt_ref[b] = (zt + _nn(ztb, gmat_s[b, c]) + hmat_s[b, c]) * g_end
        return carry

    lax.fori_loop(0, ts // C, chunk, 0)

    mean_bd = _head_block_matrix(W, 1.0 / HEAD_DIM)
    for b in range(nbatch):
        y = y_s[b]
        yc = y - _nn_rhs_exact(y, mean_bd)
        var = _nn_rhs_exact(yc * yc, mean_bd)
        y = yc * lax.rsqrt(var + RW_LN_EPS) * lnw_ref[...] + lnb_ref[...]
        o_ref[b] = (y + bonus_s[b]) * g_s[b]


def _rwkv(p, B, S, mu, w0, a0, wa2, g2, k_k, k_a, r_k, ln_w, ln_b, ts=512, group=8):
    ts = min(ts, S)
    W = WIDTH
    vec = lambda n: pl.BlockSpec((1, n), lambda i: (0, 0))
    full = lambda a: pl.BlockSpec(a.shape, lambda i: (0, 0))
    out = pl.pallas_call(
        functools.partial(_rw_kernel, ts=ts, nbatch=B, group=group),
        grid=(S // ts,),
        in_specs=[pl.BlockSpec((B, ts, 4 * W), lambda i: (0, i, COL_RW)),
                  vec(4 * W), vec(W), vec(W), full(wa2), full(g2), vec(W), vec(W), vec(W), vec(W), vec(W)],
        out_specs=pl.BlockSpec((B, ts, W), lambda i: (0, i, 0)),
        out_shape=jax.ShapeDtypeStruct((B, S, W), F32),
        scratch_shapes=[pltpu.VMEM((B, 8, 4 * W), F32), pltpu.VMEM((B, W, W), F32)]
                       + [pltpu.VMEM((B, ts, W), BF16)] * 5 + [pltpu.VMEM((B, ts // RW_CHUNK, W, W), BF16)]
                       + [pltpu.VMEM((B, ts, W), F32)] * 6 + [pltpu.VMEM((B, ts // RW_CHUNK, W, W), F32)],
        compiler_params=_params("arbitrary"),
        name="rwkv7",
    )(p.reshape(B, S, -1), mu, w0, a0, wa2, g2, k_k, k_a, r_k, ln_w, ln_b)
    return out.reshape(B * S, W)


def _sb_kernel(q_ref, k_ref, v_ref, qg_ref, kg_ref, o_ref, kb, vb, acc_o, acc_l, knorm,
               *, tq, tk, seq, prep, nsub):
    i = pl.program_id(2)
    lane = _iota((1, LANES), 1)
    mean_bd = _head_block_matrix(LANES, 1.0 / HEAD_DIM)
    ones = jnp.ones((LANES, LANES), BF16)

    @pl.when(i == 0)
    def _():
        knorm[...] = jnp.zeros_like(knorm)

        def body(c, carry):
            rows = pl.ds(pl.multiple_of(c * prep, prep), prep)
            kf = k_ref[rows, :]
            ms = _nn_rhs_exact(kf * kf, mean_bd)
            kn = (kf * lax.rsqrt(ms + NORM_EPS) * kg_ref[...]).astype(BF16)
            kb[rows, :] = kn
            vb[rows, :] = v_ref[rows, :].astype(BF16)
            kn = kn.astype(F32)
            sq = _nn_rhs_exact(kn * kn, ones)
            knorm[...] = jnp.maximum(knorm[...], jnp.max(sq.reshape(prep // 8, 8, LANES), axis=0))
            return carry
        lax.fori_loop(0, seq // prep, body, 0)

    subs = range(nsub)
    tile = [i * nsub + s for s in subs]
    q = q_ref[...]
    ms = _nn_rhs_exact(q * q, mean_bd)
    qn = q * lax.rsqrt(ms + NORM_EPS) * qg_ref[...] * (HEAD_DIM ** -0.5 * LOG2_E)
    q_lo = jnp.where(lane < HEAD_DIM, qn, 0.0).astype(BF16)
    q_hi = jnp.where(lane >= HEAD_DIM, qn, 0.0).astype(BF16)
    q2 = [jnp.concatenate([q_lo[s * tq:(s + 1) * tq], q_hi[s * tq:(s + 1) * tq]], axis=0) for s in subs]

    qf = (q_lo + q_hi).astype(F32)
    qsq = jnp.max(_nn_rhs_exact(qf * qf, ones))
    zmax = jnp.sqrt(qsq * jnp.max(knorm[...])) * SB_NORM_MARGIN
    stop_at = SB_ZERO_EXPONENT + zmax * BF16_ROUNDING

    def softplus2(z2):
        return jnp.maximum(z2, 0.0) + jnp.log2(1.0 + jnp.exp2(-jnp.abs(z2)))

    def rev_incl(n):
        return jnp.where(_iota((n, n), 0) >= _iota((n, n), 1), 1.0, 0.0).astype(BF16)

    prev = [pl.ds(pl.multiple_of(jnp.maximum(tile[s] - 1, 0) * tk, tk), tk) for s in subs]
    diag = [pl.ds(pl.multiple_of(tile[s] * tk, tk), tk) for s in subs]
    col = _iota((2 * tq, 2 * tk), 1)
    causal = (col - tk) < (_iota((2 * tq, 2 * tk), 0) & (tq - 1))
    valid = [causal & (col >= jnp.where(i > 0, 0, tk))] + [causal] * (nsub - 1)
    rev = rev_incl(tk)
    z2 = [_nt(q2[s], jnp.concatenate([kb[prev[s], :], kb[diag[s], :]], axis=0)) for s in subs]
    sp = [jnp.where(valid[s], softplus2(z2[s]), 0.0).astype(BF16) for s in subs]
    cum_diag = [_nn(sp[s][:, tk:], rev) for s in subs]
    cum = [jnp.concatenate([_nn(sp[s][:, :tk], rev) + cum_diag[s][:, 0:1], cum_diag[s]], axis=1) for s in subs]
    attn = [jnp.where(valid[s], jnp.exp2(z2[s] - cum[s]), 0.0).astype(BF16) for s in subs]
    for s in subs:
        acc_o[s] = _nn(attn[s], jnp.concatenate([vb[prev[s], :], vb[diag[s], :]], axis=0))
        acc_l[s] = jnp.broadcast_to(cum[s][:, 0:1], (2 * tq, LANES))

    for s in subs:
        def step(j, s=s):
            rows = pl.ds(pl.multiple_of(j * tk, tk), tk)
            z2 = _nt(q2[s], kb[rows, :])
            cum = _nn(softplus2(z2).astype(BF16), rev)
            seen = acc_l[s]
            attn = jnp.exp2(z2 - cum - jnp.concatenate([seen] * (tk // LANES), axis=1))
            acc_o[s] += _nn(attn.astype(BF16), vb[rows, :])
            seen = seen + cum[:, 0:1]
            acc_l[s] = seen
            return jnp.min(seen)

        def cond(carry, s=s):
            t, low = carry
            return (t < tile[s]) & (low < stop_at)

        def body(carry, s=s, step=step):
            t, _ = carry
            return t + 1, step(tile[s] - 1 - t)
        lax.while_loop(cond, body, (1, jnp.min(cum[s][:, 0:1])))

    for s in subs:
        o_ref[s * tq:(s + 1) * tq, :] = jnp.where(lane < HEAD_DIM, acc_o[s, 0:tq, :], acc_o[s, tq:2 * tq, :])


def _stick_breaking(p, B, S, qg, kg, tq=256, tk=256, nsub=2):
    assert tq == tk and S % (nsub * tq) == 0
    nq = S // (nsub * tq)
    T = B * S
    prep = min(512, S)
    return pl.pallas_call(
        functools.partial(_sb_kernel, tq=tq, tk=tk, seq=S, prep=prep, nsub=nsub),
        grid=(B, 2, nq),
        in_specs=[pl.BlockSpec((nsub * tq, LANES), lambda b, h, i: (b * nq + i, COL_SB_Q + h)),
                  pl.BlockSpec((S, LANES), lambda b, h, i: (b, COL_SB_K + h)),
                  pl.BlockSpec((S, LANES), lambda b, h, i: (b, COL_SB_V + h)),
                  pl.BlockSpec((1, LANES), lambda b, h, i: (0, 0)),
                  pl.BlockSpec((1, LANES), lambda b, h, i: (0, 0))],
        out_specs=pl.BlockSpec((nsub * tq, LANES), lambda b, h, i: (b * nq + i, h)),
        out_shape=jax.ShapeDtypeStruct((T, WIDTH), F32),
        scratch_shapes=[pltpu.VMEM((S, LANES), BF16), pltpu.VMEM((S, LANES), BF16),
                        pltpu.VMEM((nsub, 2 * tq, LANES), F32), pltpu.VMEM((nsub, 2 * tq, LANES), F32),
                        pltpu.VMEM((8, LANES), F32)],
        compiler_params=_params("arbitrary", "arbitrary", "arbitrary"),
        name="stick_breaking",
    )(p, p, p, qg, kg)


def _ret_kernel(q_ref, k_ref, v_ref, gt_ref, cos_ref, sin_ref, gn_ref, dm_ref, kdec_ref, qdec_ref, cdec_ref,
                o_ref, st_ref, *, ts):
    C = RET_CHUNK
    nch = ts // C
    pairs = range(2)
    chunks = range(nch)
    i = pl.program_id(1)

    @pl.when(i == 0)
    def _():
        st_ref[...] = jnp.zeros_like(st_ref)

    lane = _iota((1, LANES), 1)
    half = HEAD_DIM // 2
    first_half = (lane & (HEAD_DIM - 1)) < half
    same_head = (_iota((LANES, LANES), 0) >> HEAD_SHIFT) == (_iota((LANES, LANES), 1) >> HEAD_SHIFT)
    mean_bd = _head_block_matrix(LANES, 1.0 / HEAD_DIM)
    cos = cos_ref[...]
    sin = sin_ref[...]

    def rotary(t):
        swapped = jnp.where(first_half, pltpu.roll(t, LANES - half, 1), pltpu.roll(t, half, 1))
        return t * cos + swapped * sin

    def rows(t, c):
        return t[c * C:(c + 1) * C]

    cols = [slice(h * LANES, (h + 1) * LANES) for h in pairs]
    q = [rotary(q_ref[:, cols[h]]) for h in pairs]
    k = [rotary(k_ref[:, cols[h]]) * (HEAD_DIM ** -0.5) for h in pairs]
    vb = [v_ref[:, cols[h]].astype(BF16) for h in pairs]
    kb = [k[h].astype(BF16) for h in pairs]
    q_lo = [jnp.where(lane < HEAD_DIM, q[h], 0.0).astype(BF16) for h in pairs]
    q_hi = [jnp.where(lane >= HEAD_DIM, q[h], 0.0).astype(BF16) for h in pairs]
    qd = [(q[h] * jnp.concatenate([qdec_ref[h]] * nch, axis=0)).astype(BF16) for h in pairs]
    kd = [k[h] * jnp.concatenate([kdec_ref[h]] * nch, axis=0) for h in pairs]

    hc = [(h, c) for h in pairs for c in chunks]
    scores = {(h, c): _nt(jnp.concatenate([rows(q_lo[h], c), rows(q_hi[h], c)], axis=0), rows(kb[h], c)) * dm_ref[h]
              for h, c in hc}
    kv = {(h, c): _nn(rows(kd[h], c).T.astype(BF16), rows(vb[h], c)) for h, c in hc}
    intra2 = {(h, c): _nn(scores[h, c].astype(BF16), rows(vb[h], c)) for h, c in hc}
    state = {}
    for h in pairs:
        st = st_ref[h]
        for c in chunks:
            state[h, c] = st
            st = cdec_ref[h] * st + jnp.where(same_head, kv[h, c], 0.0)
        st_ref[h] = st
    inter = {(h, c): _nn(rows(qd[h], c), state[h, c].astype(BF16)) for h, c in hc}
    for h in pairs:
        o = jnp.concatenate([jnp.where(lane < HEAD_DIM, intra2[h, c][0:C], intra2[h, c][C:2 * C]) + inter[h, c]
                             for c in chunks], axis=0)
        oc = o - _nn_rhs_exact(o, mean_bd)
        var = _nn_rhs_exact(oc * oc, mean_bd)
        on = oc * lax.rsqrt(var + RET_GN_EPS) * gn_ref[:, cols[h]]
        gt = gt_ref[:, cols[h]]
        o_ref[:, cols[h]] = on * (gt * _sigmoid(gt))


def _retention_tables(S):
    C = RET_CHUNK
    inv_freq = ROPE_BASE ** (-jnp.arange(0, HEAD_DIM, 2, dtype=F32) / HEAD_DIM)
    ang = jnp.arange(S, dtype=F32)[:, None] * inv_freq[None, :]
    cos = jnp.tile(jnp.cos(ang), (1, 4))
    sin = jnp.tile(jnp.concatenate([-jnp.sin(ang), jnp.sin(ang)], axis=1), (1, 2))
    log_gamma = jnp.log(1.0 - 2.0 ** (-5.0 - jnp.arange(N_HEADS, dtype=F32)))
    idx = jnp.arange(C, dtype=F32)
    rel = idx[:, None] - idx[None, :]
    intra = jnp.where(rel >= 0, jnp.exp(jnp.maximum(rel, 0.0) * log_gamma[:, None, None]), 0.0)
    dm = intra.reshape(2, 2 * C, C)
    lanes = lambda t: jnp.repeat(t.reshape(2, 2, C).transpose(0, 2, 1), HEAD_DIM, axis=2)
    kdec = lanes(jnp.exp((C - 1 - idx)[None, :] * log_gamma[:, None]))
    qdec = lanes(jnp.exp((idx + 1.0)[None, :] * log_gamma[:, None]))
    cdec = jnp.repeat(jnp.exp(C * log_gamma).reshape(2, 1, 2), HEAD_DIM, axis=2)
    return cos, sin, dm, kdec, qdec, cdec


def _retention(p, B, S, gn, tables, ts=1024):
    ts = min(ts, S)
    nb = S // ts
    T = B * S
    C = RET_CHUNK
    cos, sin, dm, kdec, qdec, cdec = tables
    col = lambda c0: pl.BlockSpec((ts, WIDTH), lambda b, i: (b * nb + i, c0 // 2))
    tab = pl.BlockSpec((ts, LANES), lambda b, i: (i, 0))
    full = lambda a: pl.BlockSpec(a.shape, lambda b, i: (0,) * a.ndim)
    return pl.pallas_call(
        functools.partial(_ret_kernel, ts=ts),
        grid=(B, nb),
        in_specs=[col(COL_RET_Q), col(COL_RET_K), col(COL_RET_V), col(COL_RET_G), tab, tab,
                  full(gn), full(dm), full(kdec), full(qdec), full(cdec)],
        out_specs=pl.BlockSpec((ts, WIDTH), lambda b, i: (b * nb + i, 0)),
        out_shape=jax.ShapeDtypeStruct((T, WIDTH), F32),
        scratch_shapes=[pltpu.VMEM((2, LANES, LANES), F32)],
        compiler_params=_params("arbitrary", "arbitrary"),
        name="retention",
    )(p, p, p, p, cos, sin, gn, dm, kdec, qdec, cdec)


def _conv_kernel(ua_ref, ub_ref, dw_ref, db_ref, lnw_ref, lnb_ref, o_ref, buf, shifted, *, ts):
    i = pl.program_id(1)
    span = ts + CONV_HALO - SUBLANES

    @pl.when(i == 0)
    def _():
        buf[0:CONV_HALO, :] = jnp.zeros((CONV_HALO, WIDTH), F32)

    buf[CONV_HALO:, :] = ua_ref[...] * _sigmoid(ub_ref[...])
    for s in range(1, SUBLANES):
        shifted[s] = buf[s:s + span, :]
    acc = jnp.zeros((ts, WIDTH), F32) + db_ref[...]
    for j in range(CONV_WIDTH):
        start = CONV_HALO - (CONV_WIDTH - 1) + j
        s = start % SUBLANES
        window = buf[start:start + ts, :] if s == 0 else shifted[s, start - s:start - s + ts, :]
        acc = acc + dw_ref[j:j + 1, :] * window
    buf[0:CONV_HALO, :] = buf[ts:ts + CONV_HALO, :]
    xc = acc - jnp.mean(acc, axis=-1, keepdims=True)
    var = jnp.mean(xc * xc, axis=-1, keepdims=True)
    y = xc * lax.rsqrt(var + CONV_LN_EPS) * lnw_ref[...] + lnb_ref[...]
    o_ref[...] = y * _sigmoid(y)


def _conformer_conv(p, B, S, dw, db, ln_w, ln_b, ts=1024):
    ts = min(ts, S)
    nb = S // ts
    T = B * S
    vec = pl.BlockSpec((1, WIDTH), lambda b, i: (0, 0))
    return pl.pallas_call(
        functools.partial(_conv_kernel, ts=ts),
        grid=(B, nb),
        in_specs=[pl.BlockSpec((ts, WIDTH), lambda b, i: (b * nb + i, COL_CONV_A)),
                  pl.BlockSpec((ts, WIDTH), lambda b, i: (b * nb + i, COL_CONV_B)),
                  pl.BlockSpec((CONV_HALO, WIDTH), lambda b, i: (0, 0)), vec, vec, vec],
        out_specs=pl.BlockSpec((ts, WIDTH), lambda b, i: (b * nb + i, 0)),
        out_shape=jax.ShapeDtypeStruct((T, WIDTH), F32),
        scratch_shapes=[pltpu.VMEM((ts + CONV_HALO, WIDTH), F32),
                        pltpu.VMEM((SUBLANES, ts + CONV_HALO - SUBLANES, WIDTH), F32)],
        compiler_params=_params("arbitrary", "arbitrary"),
        name="conformer_conv",
    )(p, p, dw, db, ln_w, ln_b)


def _merge_kernel(x_ref, g_ref, y0_ref, y1_ref, y2_ref, y3_ref, wg_ref, wb_ref, wo_ref, o_ref):
    x = x_ref[...]
    h = _rmsnorm_rows(x, g_ref[...]).astype(BF16)
    merged = None
    for n, y_ref in enumerate((y0_ref, y1_ref, y2_ref, y3_ref)):
        term = _sigmoid(_nn(h, wg_ref[n])) * _nn(y_ref[...].astype(BF16), wb_ref[n])
        merged = term if merged is None else merged + term
    o_ref[...] = x + _nn(merged.astype(BF16), wo_ref[...])


def _merge(x2, gain, ys, wg, wb, wo, tm=512):
    T, D = x2.shape
    tm = min(tm, T)
    row = lambda n: pl.BlockSpec((tm, n), lambda i: (i, 0))
    return pl.pallas_call(
        _merge_kernel,
        grid=(T // tm,),
        in_specs=[row(D), pl.BlockSpec((1, D), lambda i: (0, 0)), row(WIDTH), row(WIDTH), row(WIDTH), row(WIDTH),
                  pl.BlockSpec(wg.shape, lambda i: (0, 0, 0)),
                  pl.BlockSpec(wb.shape, lambda i: (0, 0, 0)),
                  pl.BlockSpec(wo.shape, lambda i: (0, 0))],
        out_specs=row(D),
        out_shape=jax.ShapeDtypeStruct((T, D), F32),
        compiler_params=_params("parallel"),
        name="merge",
    )(x2, gain, *ys, wg, wb, wo)


def _ffn_kernel(x_ref, g_ref, w1_ref, w3_ref, w2_ref, o_ref, h_s):
    f = pl.program_id(1)

    @pl.when(f == 0)
    def _():
        x = x_ref[...]
        h_s[...] = _rmsnorm_rows(x, g_ref[...]).astype(BF16)
        o_ref[...] = x

    h = h_s[...]
    a = _nn(h, w1_ref[...])
    b = _nn(h, w3_ref[...])
    o_ref[...] += _nn((a * _sigmoid(a) * b).astype(BF16), w2_ref[...])


def _ffn(x2, gain, w1, w3, w2, tm=1024, tf=1408):
    T, D = x2.shape
    tm = min(tm, T)
    nf = w1.shape[1] // tf
    return pl.pallas_call(
        _ffn_kernel,
        grid=(T // tm, nf),
        in_specs=[pl.BlockSpec((tm, D), lambda i, f: (i, 0)),
                  pl.BlockSpec((1, D), lambda i, f: (0, 0)),
                  pl.BlockSpec((D, tf), lambda i, f: (0, f)),
                  pl.BlockSpec((D, tf), lambda i, f: (0, f)),
                  pl.BlockSpec((tf, D), lambda i, f: (f, 0))],
        out_specs=pl.BlockSpec((tm, D), lambda i, f: (i, 0)),
        out_shape=jax.ShapeDtypeStruct((T, D), F32),
        scratch_shapes=[pltpu.VMEM((tm, D), BF16)],
        compiler_params=_params("parallel", "arbitrary"),
        name="ffn",
    )(x2, gain, w1, w3, w2)


def _moe_kernel(x_ref, g_ref, rt_ref, w1_ref, w3_ref, w2_ref, o_ref, h_s, gate_s, rank_s, rank_t_s, count_s,
                *, tm, cap, small):
    e = pl.program_id(1)
    lane = _iota((1, LANES), 1)

    @pl.when(e == 0)
    def _():
        x = x_ref[...]
        h = _rmsnorm_rows(x, g_ref[...])
        h_s[...] = h.astype(BF16)
        o_ref[...] = x
        logits = jnp.where(lane < N_EXPERTS, _nn_3pass(h, rt_ref[...]), -jnp.inf)
        lane_f = lane.astype(F32)
        m1 = jnp.max(logits, axis=-1, keepdims=True)
        i1 = jnp.min(jnp.where(logits == m1, lane_f, float(LANES)), axis=-1, keepdims=True)
        rest = jnp.where(lane_f == i1, -jnp.inf, logits)
        m2 = jnp.max(rest, axis=-1, keepdims=True)
        i2 = jnp.min(jnp.where(rest == m2, lane_f, float(LANES)), axis=-1, keepdims=True)
        e2 = jnp.exp(m2 - m1)
        den = 1.0 + e2
        gate_s[...] = jnp.where(lane_f == i1, 1.0 / den, 0.0) + jnp.where(lane_f == i2, e2 / den, 0.0)
        chosen = jnp.where((lane_f == i1) | (lane_f == i2), 1.0, 0.0)
        before = jnp.where(_iota((tm, tm), 1) < _iota((tm, tm), 0), 1.0, 0.0).astype(BF16)
        rank = jnp.where(chosen > 0.0, _nn(before, chosen.astype(BF16)), -1.0)
        rank_s[...] = rank
        rank_t_s[...] = rank.T
        count_s[...] = jnp.sum(chosen, axis=0, keepdims=True)

    mine = lane == e
    gate = jnp.sum(jnp.where(mine, gate_s[...], 0.0), axis=-1, keepdims=True)
    rank_col = jnp.sum(jnp.where(mine, rank_s[...], 0.0), axis=-1, keepdims=True)
    rank_row = rank_t_s[pl.ds(e, 1), :]
    count = jnp.sum(jnp.where(mine, count_s[...], 0.0))
    def one_pass(first, rows):
        lanes = -(-rows // LANES) * LANES
        slot_rows = _iota((rows, tm), 0).astype(F32)
        slot_cols = _iota((tm, lanes), 1).astype(F32)
        take = jnp.where(rank_row - first == slot_rows, 1.0, 0.0).astype(BF16)
        xg = _nn(take, h_s[...]).astype(BF16)
        a = _nn(xg, w1_ref[0])
        b = _nn(xg, w3_ref[0])
        y = _nn((a * _sigmoid(a) * b).astype(BF16), w2_ref[0]).astype(BF16)
        hit = rank_col - first == slot_cols
        if lanes > rows:
            y = jnp.concatenate([y, jnp.zeros((lanes - rows, y.shape[1]), BF16)], axis=0)
            hit = hit & (slot_cols < rows)
        put = jnp.where(hit, 1.0, 0.0).astype(BF16)
        o_ref[...] += gate * _nn(put, y)

    @pl.when((count > 0.0) & (count <= small))
    def _():
        one_pass(jnp.float32(0.0), small)

    @pl.when(count > small)
    def _():
        def more(first):
            one_pass(first, cap)
            return first + cap
        lax.while_loop(lambda first: first < count, more, jnp.float32(0.0))


def _moe(x2, gain, router_pad, w1, w3, w2, tm=1024, cap=320, small=256):
    T, D = x2.shape
    tm = min(tm, T)
    cap = min(cap, tm)
    small = min(small, cap)
    E, _, F = w1.shape
    return pl.pallas_call(
        functools.partial(_moe_kernel, tm=tm, cap=cap, small=small),
        grid=(T // tm, E),
        in_specs=[pl.BlockSpec((tm, D), lambda i, e: (i, 0)),
                  pl.BlockSpec((1, D), lambda i, e: (0, 0)),
                  pl.BlockSpec((D, LANES), lambda i, e: (0, 0)),
                  pl.BlockSpec((1, D, F), lambda i, e: (e, 0, 0)),
                  pl.BlockSpec((1, D, F), lambda i, e: (e, 0, 0)),
                  pl.BlockSpec((1, F, D), lambda i, e: (e, 0, 0))],
        out_specs=pl.BlockSpec((tm, D), lambda i, e: (i, 0)),
        out_shape=jax.ShapeDtypeStruct((T, D), F32),
        scratch_shapes=[pltpu.VMEM((tm, D), BF16), pltpu.VMEM((tm, LANES), F32), pltpu.VMEM((tm, LANES), F32),
                        pltpu.VMEM((LANES, tm), F32), pltpu.VMEM((1, LANES), F32)],
        compiler_params=_params("parallel", "arbitrary"),
        name="moe",
    )(x2, gain, router_pad, w1, w3, w2)


def kernel(x, norm_mix, w_in, rw_mu, rw_w0, rw_w2, rw_a0, rw_a2, rw_g2, rw_k_k, rw_k_a, rw_r_k, rw_ln_w, rw_ln_b, sb_q_norm, sb_k_norm, ret_gn, conv_dw, conv_b, conv_ln_w, conv_ln_b, w_gate, w_branch, w_out, norm_ffn, ffn_w1, ffn_w3, ffn_w2, router, moe_w1, moe_w3, moe_w2):
    B, S, D = x.shape
    depth = norm_mix.shape[0]
    x2 = x.reshape(B * S, D)
    tables = _retention_tables(S)
    row = lambda t: t.reshape(1, -1)
    for l in range(depth):
        p = _inproj(x2, row(norm_mix[l]), w_in[l].astype(BF16))
        zeros = jnp.zeros_like(rw_w2[l])
        wa2 = jnp.concatenate([jnp.concatenate([rw_w2[l], zeros], axis=1),
                               jnp.concatenate([zeros, rw_a2[l]], axis=1)], axis=0).astype(BF16)
        y_rw = _rwkv(p, B, S, row(rw_mu[l]), row(rw_w0[l]), row(rw_a0[l]), wa2, rw_g2[l].astype(BF16),
                     row(rw_k_k[l]), row(rw_k_a[l]), row(rw_r_k[l]), row(rw_ln_w[l]), row(rw_ln_b[l]))
        y_sb = _stick_breaking(p, B, S, row(jnp.tile(sb_q_norm[l], 2)), row(jnp.tile(sb_k_norm[l], 2)))
        y_ret = _retention(p, B, S, row(ret_gn[l]), tables)
        dw = jnp.concatenate([conv_dw[l], jnp.zeros((CONV_HALO - CONV_WIDTH, WIDTH), F32)], axis=0)
        y_conv = _conformer_conv(p, B, S, dw, row(conv_b[l]), row(conv_ln_w[l]), row(conv_ln_b[l]))
        x2 = _merge(x2, row(norm_mix[l]), (y_rw, y_sb, y_ret, y_conv),
                    w_gate[l].astype(BF16), w_branch[l].astype(BF16), w_out[l].astype(BF16))
        if l % 2 == 0:
            x2 = _ffn(x2, row(norm_ffn[l]), ffn_w1[l // 2].astype(BF16), ffn_w3[l // 2].astype(BF16),
                      ffn_w2[l // 2].astype(BF16))
        else:
            rt = jnp.concatenate([router[l // 2], jnp.zeros((D, LANES - N_EXPERTS), F32)], axis=1)
            x2 = _moe(x2, row(norm_ffn[l]), rt, moe_w1[l // 2].astype(BF16), moe_w3[l // 2].astype(BF16),
                      moe_w2[l // 2].astype(BF16))
    return x2.reshape(B, S, D)
```

```python
import functools

import jax
import jax.numpy as jnp
from jax import lax
from jax.experimental import pallas as pl
from jax.experimental.pallas import tpu as pltpu

F32 = jnp.float32
BF16 = jnp.bfloat16

LANES = 128
SUBLANES = 8
VMEM_LIMIT = 56 * 1024 * 1024

D_MODEL = 1024
HEAD_DIM = 64
HEAD_SHIFT = HEAD_DIM.bit_length() - 1
N_HEADS = 4
WIDTH = N_HEADS * HEAD_DIM
NORM_EPS = 1e-6
LOG2_E = 1.4426950408889634
RW_DECAY_SCALE = 0.606531
RW_LN_EPS = 64e-5
RW_CHUNK = 64
RW_BASE = 8
RW_MIN_NORM = 1e-12
RW_DECAY_LORA = 64
BF16_ROUNDING = 2.0 ** -8
SB_ZERO_EXPONENT = 152.0
SB_NORM_MARGIN = 1.01
RET_CHUNK = 128
RET_GN_EPS = 1e-5
ROPE_BASE = 10000.0
CONV_WIDTH = 31
CONV_HALO = 32
CONV_LN_EPS = 1e-5
N_EXPERTS = 8
N_IN = 3328
COL_RW = 0
COL_SB_Q, COL_SB_K, COL_SB_V = 8, 10, 12
COL_RET_Q, COL_RET_K, COL_RET_V, COL_RET_G = 14, 16, 18, 20
COL_CONV_A, COL_CONV_B = 11, 12


def _nn(a, b):
    return lax.dot_general(a, b, (((1,), (0,)), ((), ())), preferred_element_type=F32)


def _nt(a, b):
    return lax.dot_general(a, b, (((1,), (1,)), ((), ())), preferred_element_type=F32)


def _mm(a, b):
    return _nn(a.astype(BF16), b.astype(BF16))


def _split(x):
    hi = x.astype(BF16)
    lo = (x - hi.astype(F32)).astype(BF16)
    return hi, lo


def _nn_rhs_exact(x, m):
    hi, lo = _split(x)
    return _nn(hi, m) + _nn(lo, m)


def _nn_lhs_exact(m, x):
    hi, lo = _split(x)
    return _nn(m, hi) + _nn(m, lo)


def _nn_3pass(a, b):
    ah, al = _split(a)
    bh, bl = _split(b)
    return _nn(ah, bh) + (_nn(ah, bl) + _nn(al, bh))


def _iota(shape, axis):
    return lax.broadcasted_iota(jnp.int32, shape, axis)


def _head_block_matrix(n, value):
    same = (_iota((n, n), 0) >> HEAD_SHIFT) == (_iota((n, n), 1) >> HEAD_SHIFT)
    return jnp.where(same, value, 0.0).astype(BF16)


def _sigmoid(x):
    return 0.5 * jnp.tanh(0.5 * x) + 0.5


def _rmsnorm_rows(x, gain):
    return x * lax.rsqrt(jnp.mean(x * x, axis=-1, keepdims=True) + NORM_EPS) * gain


def _params(*sem):
    return pltpu.CompilerParams(dimension_semantics=sem, vmem_limit_bytes=VMEM_LIMIT)


def _inproj_kernel(x_ref, g_ref, w_ref, p_ref):
    h = _rmsnorm_rows(x_ref[...], g_ref[...])
    p_ref[...] = _nn(h.astype(BF16), w_ref[...])


def _inproj(x2, gain, w_bf16, tm=512):
    T, D = x2.shape
    N = w_bf16.shape[1]
    return pl.pallas_call(
        _inproj_kernel,
        grid=(T // tm,),
        in_specs=[pl.BlockSpec((tm, D), lambda i: (i, 0)),
                  pl.BlockSpec((1, D), lambda i: (0, 0)),
                  pl.BlockSpec((D, N), lambda i: (0, 0))],
        out_specs=pl.BlockSpec((tm, N), lambda i: (i, 0)),
        out_shape=jax.ShapeDtypeStruct((T, N), F32),
        compiler_params=_params("parallel"),
        name="inproj",
    )(x2, gain, w_bf16)


def _rw_kernel(p_ref, mu_ref, w0_ref, a0_ref, wa2_ref, g2_ref, kk_ref, ka_ref, rk_ref, lnw_ref, lnb_ref,
               o_ref, prev_ref, zt_ref, at_s, bt_s, kt_s, rt_s, qh_s, gmat_s, v_s, ecl_s, y0_s, y_s, bonus_s, g_s, hmat_s,
               *, ts, nbatch, group):
    C = RW_CHUNK
    W = WIDTH
    i = pl.program_id(0)

    @pl.when(i == 0)
    def _():
        prev_ref[...] = jnp.zeros_like(prev_ref)
        zt_ref[...] = jnp.zeros_like(zt_ref)

    lane = _iota((1, LANES), 1)
    row = _iota((ts, 1), 0)
    r4 = _iota((W, W), 0)
    c4 = _iota((W, W), 1)
    same_head = (r4 >> HEAD_SHIFT) == (c4 >> HEAD_SHIFT)
    tri4 = jnp.where(same_head & (c4 <= r4), 1.0, 0.0).astype(BF16)
    ones_bd = _head_block_matrix(W, 1.0)

    for b in range(nbatch):
        p = p_ref[b]
        shifted = jnp.where(row == 0, prev_ref[b, 0:1, :], pltpu.roll(p, 1, 0))
        prev_ref[b, 0:1, :] = p_ref[b, ts - 1:ts, :]
        pm = p + (shifted - p) * mu_ref[...]
        r = pm[:, 0:W]
        k = pm[:, W:2 * W]
        v = pm[:, 2 * W:3 * W]
        lora = pm[:, 3 * W:3 * W + LANES]
        pg = pm[:, 3 * W + LANES:]
        wa = _mm(jnp.where(lane < RW_DECAY_LORA, jnp.tanh(lora), lora), wa2_ref[...])
        lw = -RW_DECAY_SCALE * _sigmoid(w0_ref[...] + wa[:, 0:W])
        a = _sigmoid(a0_ref[...] + wa[:, W:])
        g_s[b] = _mm(_sigmoid(pg), g2_ref[...])
        kk = k * kk_ref[...]
        kk = kk * jnp.minimum(lax.rsqrt(_nn_rhs_exact(kk * kk, ones_bd)), 1.0 / RW_MIN_NORM)
        kp = k * (1.0 + (a - 1.0) * ka_ref[...])
        bonus_s[b] = _nn_rhs_exact(r * kp * rk_ref[...], ones_bd) * v
        cl = jnp.concatenate([_nn_lhs_exact(tri4, lw[n * W:(n + 1) * W, :]) for n in range(ts // W)], axis=0)
        ecl = jnp.exp(cl)
        eml = jnp.exp(-cl)
        at_s[b] = (-kk * jnp.exp(cl - lw)).astype(BF16)
        bt_s[b] = (kk * a * eml).astype(BF16)
        kt_s[b] = (kp * eml).astype(BF16)
        rt_s[b] = (r * ecl).astype(BF16)
        v_s[b] = v
        ecl_s[b] = ecl

    head = _iota((1, W), 1) >> HEAD_SHIFT
    tw = _iota((C, W), 0)
    sw = _iota((C, W), 1) & (C - 1)
    strict = tw > sw
    incl = tw >= sw
    eye = jnp.where(tw == sw, 1.0, 0.0)

    def same_block(size):
        return (tw & -size) == (sw & -size)

    def stack4(x):
        return jnp.concatenate([jnp.where(head == h, x, jnp.zeros_like(x)) for h in range(N_HEADS)], axis=0)

    def block_diag(x):
        return jnp.concatenate([x.astype(BF16)] * N_HEADS, axis=0) * ones_bd

    chunks = [(b, slice(c * C, (c + 1) * C)) for c in range(ts // C) for b in range(nbatch)]
    for g0 in range(0, len(chunks), group):
        grp = chunks[g0:g0 + group]
        n = range(len(grp))
        at = [at_s[b, rows, :] for b, rows in grp]
        rt = [rt_s[b, rows, :] for b, rows in grp]
        b4 = [stack4(bt_s[b, rows, :]) for b, rows in grp]
        k4 = [stack4(kt_s[b, rows, :]) for b, rows in grp]
        v4 = [stack4(v_s[b, rows, :].astype(BF16)) for b, rows in grp]
        lab = [jnp.where(strict, _nt(at[j], b4[j]), 0.0) for j in n]
        lak = [jnp.where(strict, _nt(at[j], k4[j]), 0.0) for j in n]
        mrb = [jnp.where(incl, _nt(rt[j], b4[j]), 0.0) for j in n]
        mrk = [jnp.where(incl, _nt(rt[j], k4[j]), 0.0) for j in n]
        m = [jnp.where(same_block(RW_BASE), lab[j], 0.0) for j in n]
        tinv = [eye + m[j] for j in n]
        for _ in range(2):
            m = [_nn(m[j].astype(BF16), block_diag(m[j])) for j in n]
            tinv = [tinv[j] + _nn(tinv[j].astype(BF16), block_diag(m[j])) for j in n]
        size = RW_BASE
        while size < C:
            below = same_block(2 * size) & jnp.logical_not(same_block(size))
            e_bd = [block_diag(jnp.where(below, lab[j], 0.0)) for j in n]
            half = [_nn(tinv[j].astype(BF16), e_bd[j]) for j in n]
            tinv = [tinv[j] + _nn(half[j].astype(BF16), block_diag(tinv[j])) for j in n]
            size *= 2
        tb = [tinv[j].astype(BF16) for j in n]
        mb = [mrb[j].astype(BF16) for j in n]
        w = [_nn(tb[j], stack4(at[j])) for j in n]
        u0 = [_nn(tb[j], stack4(_nn(lak[j].astype(BF16), v4[j])).astype(BF16)) for j in n]
        qh = [rt[j].astype(F32) + _nn(mb[j], stack4(w[j]).astype(BF16)) for j in n]
        y0 = [_nn(mb[j], stack4(u0[j]).astype(BF16)) + _nn(mrk[j].astype(BF16), v4[j]) for j in n]
        pad = jnp.zeros((C, W), F32)
        for j, (b, rows) in enumerate(grp):
            c = rows.start // C
            wt = jnp.concatenate([w[j], pad], axis=0).T.astype(BF16)
            uvt = jnp.concatenate([u0[j], v_s[b, rows, :]], axis=0).T.astype(BF16)
            bt = bt_s[b, rows, :]
            gmat = _nn(wt, jnp.concatenate([bt, jnp.zeros_like(bt)], axis=0))
            hmat = _nn(uvt, jnp.concatenate([bt, kt_s[b, rows, :]], axis=0))
            gmat_s[b, c] = jnp.where(same_head, gmat, 0.0).astype(BF16)
            hmat_s[b, c] = jnp.where(same_head, hmat, 0.0)
            qh_s[b, rows, :] = qh[j].astype(BF16)
            y0_s[b, rows, :] = y0[j]

    def chunk(c, carry):
        rows = pl.ds(pl.multiple_of(c * C, C), C)
        for b in range(nbatch):
            zt = zt_ref[b]
            ztb = zt.astype(BF16)
            y_s[b, rows, :] = _nt(qh_s[b, rows, :], ztb) + y0_s[b, rows, :]
            g_end = ecl_s[b, pl.ds(c * C + C - 1, 1), :]
            zt_ref[b] = (zt + _nn(ztb, gmat_s[b, c]) + hmat_s[b, c]) * g_end
        return carry

    lax.fori_loop(0, ts // C, chunk, 0)

    mean_bd = _head_block_matrix(W, 1.0 / HEAD_DIM)
    for b in range(nbatch):
        y = y_s[b]
        yc = y - _nn_rhs_exact(y, mean_bd)
        var = _nn_rhs_exact(yc * yc, mean_bd)
        y = yc * lax.rsqrt(var + RW_LN_EPS) * lnw_ref[...] + lnb_ref[...]
        o_ref[b] = (y + bonus_s[b]) * g_s[b]


def _rwkv(p, B, S, mu, w0, a0, wa2, g2, k_k, k_a, r_k, ln_w, ln_b, ts=512, group=8):
    ts = min(ts, S)
    W = WIDTH
    vec = lambda n: pl.BlockSpec((1, n), lambda i: (0, 0))
    full = lambda a: pl.BlockSpec(a.shape, lambda i: (0, 0))
    out = pl.pallas_call(
        functools.partial(_rw_kernel, ts=ts, nbatch=B, group=group),
        grid=(S // ts,),
        in_specs=[pl.BlockSpec((B, ts, 4 * W), lambda i: (0, i, COL_RW)),
                  vec(4 * W), vec(W), vec(W), full(wa2), full(g2), vec(W), vec(W), vec(W), vec(W), vec(W)],
        out_specs=pl.BlockSpec((B, ts, W), lambda i: (0, i, 0)),
        out_shape=jax.ShapeDtypeStruct((B, S, W), F32),
        scratch_shapes=[pltpu.VMEM((B, 8, 4 * W), F32), pltpu.VMEM((B, W, W), F32)]
                       + [pltpu.VMEM((B, ts, W), BF16)] * 5 + [pltpu.VMEM((B, ts // RW_CHUNK, W, W), BF16)]
                       + [pltpu.VMEM((B, ts, W), F32)] * 6 + [pltpu.VMEM((B, ts // RW_CHUNK, W, W), F32)],
        compiler_params=_params("arbitrary"),
        name="rwkv7",
    )(p.reshape(B, S, -1), mu, w0, a0, wa2, g2, k_k, k_a, r_k, ln_w, ln_b)
    return out.reshape(B * S, W)


def _sb_kernel(q_ref, k_ref, v_ref, qg_ref, kg_ref, o_ref, kb, vb, acc_o, acc_l, knorm,
               *, tq, tk, seq, prep, nsub):
    i = pl.program_id(2)
    lane = _iota((1, LANES), 1)
    mean_bd = _head_block_matrix(LANES, 1.0 / HEAD_DIM)
    ones = jnp.ones((LANES, LANES), BF16)

    @pl.when(i == 0)
    def _():
        knorm[...] = jnp.zeros_like(knorm)

        def body(c, carry):
            rows = pl.ds(pl.multiple_of(c * prep, prep), prep)
            kf = k_ref[rows, :]
            ms = _nn_rhs_exact(kf * kf, mean_bd)
            kn = (kf * lax.rsqrt(ms + NORM_EPS) * kg_ref[...]).astype(BF16)
            kb[rows, :] = kn
            vb[rows, :] = v_ref[rows, :].astype(BF16)
            kn = kn.astype(F32)
            sq = _nn_rhs_exact(kn * kn, ones)
            knorm[...] = jnp.maximum(knorm[...], jnp.max(sq.reshape(prep // 8, 8, LANES), axis=0))
            return carry
        lax.fori_loop(0, seq // prep, body, 0)

    subs = range(nsub)
    tile = [i * nsub + s for s in subs]
    q = q_ref[...]
    ms = _nn_rhs_exact(q * q, mean_bd)
    qn = q * lax.rsqrt(ms + NORM_EPS) * qg_ref[...] * (HEAD_DIM ** -0.5 * LOG2_E)
    q_lo = jnp.where(lane < HEAD_DIM, qn, 0.0).astype(BF16)
    q_hi = jnp.where(lane >= HEAD_DIM, qn, 0.0).astype(BF16)
    q2 = [jnp.concatenate([q_lo[s * tq:(s + 1) * tq], q_hi[s * tq:(s + 1) * tq]], axis=0) for s in subs]

    qf = (q_lo + q_hi).astype(F32)
    qsq = jnp.max(_nn_rhs_exact(qf * qf, ones))
    zmax = jnp.sqrt(qsq * jnp.max(knorm[...])) * SB_NORM_MARGIN
    stop_at = SB_ZERO_EXPONENT + zmax * BF16_ROUNDING

    def softplus2(z2):
        return jnp.maximum(z2, 0.0) + jnp.log2(1.0 + jnp.exp2(-jnp.abs(z2)))

    def rev_incl(n):
        return jnp.where(_iota((n, n), 0) >= _iota((n, n), 1), 1.0, 0.0).astype(BF16)

    prev = [pl.ds(pl.multiple_of(jnp.maximum(tile[s] - 1, 0) * tk, tk), tk) for s in subs]
    diag = [pl.ds(pl.multiple_of(tile[s] * tk, tk), tk) for s in subs]
    col = _iota((2 * tq, 2 * tk), 1)
    causal = (col - tk) < (_iota((2 * tq, 2 * tk), 0) & (tq - 1))
    valid = [causal & (col >= jnp.where(i > 0, 0, tk))] + [causal] * (nsub - 1)
    rev = rev_incl(tk)
    z2 = [_nt(q2[s], jnp.concatenate([kb[prev[s], :], kb[diag[s], :]], axis=0)) for s in subs]
    sp = [jnp.where(valid[s], softplus2(z2[s]), 0.0).astype(BF16) for s in subs]
    cum_diag = [_nn(sp[s][:, tk:], rev) for s in subs]
    cum = [jnp.concatenate([_nn(sp[s][:, :tk], rev) + cum_diag[s][:, 0:1], cum_diag[s]], axis=1) for s in subs]
    attn = [jnp.where(valid[s], jnp.exp2(z2[s] - cum[s]), 0.0).astype(BF16) for s in subs]
    for s in subs:
        acc_o[s] = _nn(attn[s], jnp.concatenate([vb[prev[s], :], vb[diag[s], :]], axis=0))
        acc_l[s] = jnp.broadcast_to(cum[s][:, 0:1], (2 * tq, LANES))

    for s in subs:
        def step(j, s=s):
            rows = pl.ds(pl.multiple_of(j * tk, tk), tk)
            z2 = _nt(q2[s], kb[rows, :])
            cum = _nn(softplus2(z2).astype(BF16), rev)
            seen = acc_l[s]
            attn = jnp.exp2(z2 - cum - jnp.concatenate([seen] * (tk // LANES), axis=1))
            acc_o[s] += _nn(attn.astype(BF16), vb[rows, :])
            seen = seen + cum[:, 0:1]
            acc_l[s] = seen
            return jnp.min(seen)

        def cond(carry, s=s):
            t, low = carry
            return (t < tile[s]) & (low < stop_at)

        def body(carry, s=s, step=step):
            t, _ = carry
            return t + 1, step(tile[s] - 1 - t)
        lax.while_loop(cond, body, (1, jnp.min(cum[s][:, 0:1])))

    for s in subs:
        o_ref[s * tq:(s + 1) * tq, :] = jnp.where(lane < HEAD_DIM, acc_o[s, 0:tq, :], acc_o[s, tq:2 * tq, :])


def _stick_breaking(p, B, S, qg, kg, tq=256, tk=256, nsub=2):
    assert tq == tk and S % (nsub * tq) == 0
    nq = S // (nsub * tq)
    T = B * S
    prep = min(512, S)
    return pl.pallas_call(
        functools.partial(_sb_kernel, tq=tq, tk=tk, seq=S, prep=prep, nsub=nsub),
        grid=(B, 2, nq),
        in_specs=[pl.BlockSpec((nsub * tq, LANES), lambda b, h, i: (b * nq + i, COL_SB_Q + h)),
                  pl.BlockSpec((S, LANES), lambda b, h, i: (b, COL_SB_K + h)),
                  pl.BlockSpec((S, LANES), lambda b, h, i: (b, COL_SB_V + h)),
                  pl.BlockSpec((1, LANES), lambda b, h, i: (0, 0)),
                  pl.BlockSpec((1, LANES), lambda b, h, i: (0, 0))],
        out_specs=pl.BlockSpec((nsub * tq, LANES), lambda b, h, i: (b * nq + i, h)),
        out_shape=jax.ShapeDtypeStruct((T, WIDTH), F32),
        scratch_shapes=[pltpu.VMEM((S, LANES), BF16), pltpu.VMEM((S, LANES), BF16),
                        pltpu.VMEM((nsub, 2 * tq, LANES), F32), pltpu.VMEM((nsub, 2 * tq, LANES), F32),
                        pltpu.VMEM((8, LANES), F32)],
        compiler_params=_params("arbitrary", "arbitrary", "arbitrary"),
        name="stick_breaking",
    )(p, p, p, qg, kg)


def _ret_kernel(q_ref, k_ref, v_ref, gt_ref, cos_ref, sin_ref, gn_ref, dm_ref, kdec_ref, qdec_ref, cdec_ref,
                o_ref, st_ref, *, ts):
    C = RET_CHUNK
    nch = ts // C
    pairs = range(2)
    chunks = range(nch)
    i = pl.program_id(1)

    @pl.when(i == 0)
    def _():
        st_ref[...] = jnp.zeros_like(st_ref)

    lane = _iota((1, LANES), 1)
    half = HEAD_DIM // 2
    first_half = (lane & (HEAD_DIM - 1)) < half
    same_head = (_iota((LANES, LANES), 0) >> HEAD_SHIFT) == (_iota((LANES, LANES), 1) >> HEAD_SHIFT)
    mean_bd = _head_block_matrix(LANES, 1.0 / HEAD_DIM)
    cos = cos_ref[...]
    sin = sin_ref[...]

    def rotary(t):
        swapped = jnp.where(first_half, pltpu.roll(t, LANES - half, 1), pltpu.roll(t, half, 1))
        return t * cos + swapped * sin

    def rows(t, c):
        return t[c * C:(c + 1) * C]

    cols = [slice(h * LANES, (h + 1) * LANES) for h in pairs]
    q = [rotary(q_ref[:, cols[h]]) for h in pairs]
    k = [rotary(k_ref[:, cols[h]]) * (HEAD_DIM ** -0.5) for h in pairs]
    vb = [v_ref[:, cols[h]].astype(BF16) for h in pairs]
    kb = [k[h].astype(BF16) for h in pairs]
    q_lo = [jnp.where(lane < HEAD_DIM, q[h], 0.0).astype(BF16) for h in pairs]
    q_hi = [jnp.where(lane >= HEAD_DIM, q[h], 0.0).astype(BF16) for h in pairs]
    qd = [(q[h] * jnp.concatenate([qdec_ref[h]] * nch, axis=0)).astype(BF16) for h in pairs]
    kd = [k[h] * jnp.concatenate([kdec_ref[h]] * nch, axis=0) for h in pairs]

    hc = [(h, c) for h in pairs for c in chunks]
    scores = {(h, c): _nt(jnp.concatenate([rows(q_lo[h], c), rows(q_hi[h], c)], axis=0), rows(kb[h], c)) * dm_ref[h]
              for h, c in hc}
    kv = {(h, c): _nn(rows(kd[h], c).T.astype(BF16), rows(vb[h], c)) for h, c in hc}
    intra2 = {(h, c): _nn(scores[h, c].astype(BF16), rows(vb[h], c)) for h, c in hc}
    state = {}
    for h in pairs:
        st = st_ref[h]
        for c in chunks:
            state[h, c] = st
            st = cdec_ref[h] * st + jnp.where(same_head, kv[h, c], 0.0)
        st_ref[h] = st
    inter = {(h, c): _nn(rows(qd[h], c), state[h, c].astype(BF16)) for h, c in hc}
    for h in pairs:
        o = jnp.concatenate([jnp.where(lane < HEAD_DIM, intra2[h, c][0:C], intra2[h, c][C:2 * C]) + inter[h, c]
                             for c in chunks], axis=0)
        oc = o - _nn_rhs_exact(o, mean_bd)
        var = _nn_rhs_exact(oc * oc, mean_bd)
        on = oc * lax.rsqrt(var + RET_GN_EPS) * gn_ref[:, cols[h]]
        gt = gt_ref[:, cols[h]]
        o_ref[:, cols[h]] = on * (gt * _sigmoid(gt))


def _retention_tables(S):
    C = RET_CHUNK
    inv_freq = ROPE_BASE ** (-jnp.arange(0, HEAD_DIM, 2, dtype=F32) / HEAD_DIM)
    ang = jnp.arange(S, dtype=F32)[:, None] * inv_freq[None, :]
    cos = jnp.tile(jnp.cos(ang), (1, 4))
    sin = jnp.tile(jnp.concatenate([-jnp.sin(ang), jnp.sin(ang)], axis=1), (1, 2))
    log_gamma = jnp.log(1.0 - 2.0 ** (-5.0 - jnp.arange(N_HEADS, dtype=F32)))
    idx = jnp.arange(C, dtype=F32)
    rel = idx[:, None] - idx[None, :]
    intra = jnp.where(rel >= 0, jnp.exp(jnp.maximum(rel, 0.0) * log_gamma[:, None, None]), 0.0)
    dm = intra.reshape(2, 2 * C, C)
    lanes = lambda t: jnp.repeat(t.reshape(2, 2, C).transpose(0, 2, 1), HEAD_DIM, axis=2)
    kdec = lanes(jnp.exp((C - 1 - idx)[None, :] * log_gamma[:, None]))
    qdec = lanes(jnp.exp((idx + 1.0)[None, :] * log_gamma[:, None]))
    cdec = jnp.repeat(jnp.exp(C * log_gamma).reshape(2, 1, 2), HEAD_DIM, axis=2)
    return cos, sin, dm, kdec, qdec, cdec


def _retention(p, B, S, gn, tables, ts=1024):
    ts = min(ts, S)
    nb = S // ts
    T = B * S
    C = RET_CHUNK
    cos, sin, dm, kdec, qdec, cdec = tables
    col = lambda c0: pl.BlockSpec((ts, WIDTH), lambda b, i: (b * nb + i, c0 // 2))
    tab = pl.BlockSpec((ts, LANES), lambda b, i: (i, 0))
    full = lambda a: pl.BlockSpec(a.shape, lambda b, i: (0,) * a.ndim)
    return pl.pallas_call(
        functools.partial(_ret_kernel, ts=ts),
        grid=(B, nb),
        in_specs=[col(COL_RET_Q), col(COL_RET_K), col(COL_RET_V), col(COL_RET_G), tab, tab,
                  full(gn), full(dm), full(kdec), full(qdec), full(cdec)],
        out_specs=pl.BlockSpec((ts, WIDTH), lambda b, i: (b * nb + i, 0)),
        out_shape=jax.ShapeDtypeStruct((T, WIDTH), F32),
        scratch_shapes=[pltpu.VMEM((2, LANES, LANES), F32)],
        compiler_params=_params("arbitrary", "arbitrary"),
        name="retention",
    )(p, p, p, p, cos, sin, gn, dm, kdec, qdec, cdec)


def _conv_kernel(ua_ref, ub_ref, dw_ref, db_ref, lnw_ref, lnb_ref, o_ref, buf, shifted, *, ts):
    i = pl.program_id(1)
    span = ts + CONV_HALO - SUBLANES

    @pl.when(i == 0)
    def _():
        buf[0:CONV_HALO, :] = jnp.zeros((CONV_HALO, WIDTH), F32)

    buf[CONV_HALO:, :] = ua_ref[...] * _sigmoid(ub_ref[...])
    for s in range(1, SUBLANES):
        shifted[s] = buf[s:s + span, :]
    acc = jnp.zeros((ts, WIDTH), F32) + db_ref[...]
    for j in range(CONV_WIDTH):
        start = CONV_HALO - (CONV_WIDTH - 1) + j
        s = start % SUBLANES
        window = buf[start:start + ts, :] if s == 0 else shifted[s, start - s:start - s + ts, :]
        acc = acc + dw_ref[j:j + 1, :] * window
    buf[0:CONV_HALO, :] = buf[ts:ts + CONV_HALO, :]
    xc = acc - jnp.mean(acc, axis=-1, keepdims=True)
    var = jnp.mean(xc * xc, axis=-1, keepdims=True)
    y = xc * lax.rsqrt(var + CONV_LN_EPS) * lnw_ref[...] + lnb_ref[...]
    o_ref[...] = y * _sigmoid(y)


def _conformer_conv(p, B, S, dw, db, ln_w, ln_b, ts=1024):
    ts = min(ts, S)
    nb = S // ts
    T = B * S
    vec = pl.BlockSpec((1, WIDTH), lambda b, i: (0, 0))
    return pl.pallas_call(
        functools.partial(_conv_kernel, ts=ts),
        grid=(B, nb),
        in_specs=[pl.BlockSpec((ts, WIDTH), lambda b, i: (b * nb + i, COL_CONV_A)),
                  pl.BlockSpec((ts, WIDTH), lambda b, i: (b * nb + i, COL_CONV_B)),
                  pl.BlockSpec((CONV_HALO, WIDTH), lambda b, i: (0, 0)), vec, vec, vec],
        out_specs=pl.BlockSpec((ts, WIDTH), lambda b, i: (b * nb + i, 0)),
        out_shape=jax.ShapeDtypeStruct((T, WIDTH), F32),
        scratch_shapes=[pltpu.VMEM((ts + CONV_HALO, WIDTH), F32),
                        pltpu.VMEM((SUBLANES, ts + CONV_HALO - SUBLANES, WIDTH), F32)],
        compiler_params=_params("arbitrary", "arbitrary"),
        name="conformer_conv",
    )(p, p, dw, db, ln_w, ln_b)


def _merge_kernel(x_ref, g_ref, y0_ref, y1_ref, y2_ref, y3_ref, wg_ref, wb_ref, wo_ref, o_ref):
    branch = [_nn(y_ref[...].astype(BF16), wb_ref[n]) for n, y_ref in enumerate((y0_ref, y1_ref, y2_ref, y3_ref))]
    x = x_ref[...]
    h = _rmsnorm_rows(x, g_ref[...]).astype(BF16)
    merged = None
    for n in range(len(branch)):
        term = _sigmoid(_nn(h, wg_ref[n])) * branch[n]
        merged = term if merged is None else merged + term
    o_ref[...] = x + _nn(merged.astype(BF16), wo_ref[...])


def _merge(x2, gain, ys, wg, wb, wo, tm=512):
    T, D = x2.shape
    tm = min(tm, T)
    row = lambda n: pl.BlockSpec((tm, n), lambda i: (i, 0))
    return pl.pallas_call(
        _merge_kernel,
        grid=(T // tm,),
        in_specs=[row(D), pl.BlockSpec((1, D), lambda i: (0, 0)), row(WIDTH), row(WIDTH), row(WIDTH), row(WIDTH),
                  pl.BlockSpec(wg.shape, lambda i: (0, 0, 0)),
                  pl.BlockSpec(wb.shape, lambda i: (0, 0, 0)),
                  pl.BlockSpec(wo.shape, lambda i: (0, 0))],
        out_specs=row(D),
        out_shape=jax.ShapeDtypeStruct((T, D), F32),
        compiler_params=_params("parallel"),
        name="merge",
    )(x2, gain, *ys, wg, wb, wo)


def _ffn_kernel(x_ref, g_ref, w1_ref, w3_ref, w2_ref, o_ref, h_s):
    f = pl.program_id(1)

    @pl.when(f == 0)
    def _():
        x = x_ref[...]
        h_s[...] = _rmsnorm_rows(x, g_ref[...]).astype(BF16)
        o_ref[...] = x

    h = h_s[...]
    a = _nn(h, w1_ref[...])
    b = _nn(h, w3_ref[...])
    o_ref[...] += _nn((a * _sigmoid(a) * b).astype(BF16), w2_ref[...])


def _ffn(x2, gain, w1, w3, w2, tm=1024, tf=1408):
    T, D = x2.shape
    tm = min(tm, T)
    nf = w1.shape[1] // tf
    return pl.pallas_call(
        _ffn_kernel,
        grid=(T // tm, nf),
        in_specs=[pl.BlockSpec((tm, D), lambda i, f: (i, 0)),
                  pl.BlockSpec((1, D), lambda i, f: (0, 0)),
                  pl.BlockSpec((D, tf), lambda i, f: (0, f)),
                  pl.BlockSpec((D, tf), lambda i, f: (0, f)),
                  pl.BlockSpec((tf, D), lambda i, f: (f, 0))],
        out_specs=pl.BlockSpec((tm, D), lambda i, f: (i, 0)),
        out_shape=jax.ShapeDtypeStruct((T, D), F32),
        scratch_shapes=[pltpu.VMEM((tm, D), BF16)],
        compiler_params=_params("parallel", "arbitrary"),
        name="ffn",
    )(x2, gain, w1, w3, w2)


def _moe_kernel(x_ref, g_ref, rt_ref, w1_ref, w3_ref, w2_ref, o_ref, h_s, gate_s, rank_s, rank_t_s, count_s,
                *, tm, cap, small):
    e = pl.program_id(1)
    lane = _iota((1, LANES), 1)

    @pl.when(e == 0)
    def _():
        x = x_ref[...]
        h = _rmsnorm_rows(x, g_ref[...])
        h_s[...] = h.astype(BF16)
        o_ref[...] = x
        logits = jnp.where(lane < N_EXPERTS, _nn_3pass(h, rt_ref[...]), -jnp.inf)
        lane_f = lane.astype(F32)
        m1 = jnp.max(logits, axis=-1, keepdims=True)
        i1 = jnp.min(jnp.where(logits == m1, lane_f, float(LANES)), axis=-1, keepdims=True)
        rest = jnp.where(lane_f == i1, -jnp.inf, logits)
        m2 = jnp.max(rest, axis=-1, keepdims=True)
        i2 = jnp.min(jnp.where(rest == m2, lane_f, float(LANES)), axis=-1, keepdims=True)
        e2 = jnp.exp(m2 - m1)
        den = 1.0 + e2
        gate_s[...] = jnp.where(lane_f == i1, 1.0 / den, 0.0) + jnp.where(lane_f == i2, e2 / den, 0.0)
        chosen = jnp.where((lane_f == i1) | (lane_f == i2), 1.0, 0.0)
        before = jnp.where(_iota((tm, tm), 1) < _iota((tm, tm), 0), 1.0, 0.0).astype(BF16)
        rank = jnp.where(chosen > 0.0, _nn(before, chosen.astype(BF16)), -1.0)
        rank_s[...] = rank
        rank_t_s[...] = rank.T
        count_s[...] = jnp.sum(chosen, axis=0, keepdims=True)

    mine = lane == e
    gate = jnp.sum(jnp.where(mine, gate_s[...], 0.0), axis=-1, keepdims=True)
    rank_col = jnp.sum(jnp.where(mine, rank_s[...], 0.0), axis=-1, keepdims=True)
    rank_row = rank_t_s[pl.ds(e, 1), :]
    count = jnp.sum(jnp.where(mine, count_s[...], 0.0))
    def one_pass(first, rows):
        lanes = -(-rows // LANES) * LANES
        slot_rows = _iota((rows, tm), 0).astype(F32)
        slot_cols = _iota((tm, lanes), 1).astype(F32)
        take = jnp.where(rank_row - first == slot_rows, 1.0, 0.0).astype(BF16)
        hit = rank_col - first == slot_cols
        if lanes > rows:
            hit = hit & (slot_cols < rows)
        put = jnp.where(hit, 1.0, 0.0).astype(BF16)
        xg = _nn(take, h_s[...]).astype(BF16)
        a = _nn(xg, w1_ref[0])
        b = _nn(xg, w3_ref[0])
        y = _nn((a * _sigmoid(a) * b).astype(BF16), w2_ref[0]).astype(BF16)
        if lanes > rows:
            y = jnp.concatenate([y, jnp.zeros((lanes - rows, y.shape[1]), BF16)], axis=0)
        o_ref[...] += gate * _nn(put, y)

    @pl.when((count > 0.0) & (count <= small))
    def _():
        one_pass(jnp.float32(0.0), small)

    @pl.when(count > small)
    def _():
        def more(first):
            one_pass(first, cap)
            return first + cap
        lax.while_loop(lambda first: first < count, more, jnp.float32(0.0))


def _moe(x2, gain, router_pad, w1, w3, w2, tm=1024, cap=320, small=256):
    T, D = x2.shape
    tm = min(tm, T)
    cap = min(cap, tm)
    small = min(small, cap)
    E, _, F = w1.shape
    return pl.pallas_call(
        functools.partial(_moe_kernel, tm=tm, cap=cap, small=small),
        grid=(T // tm, E),
        in_specs=[pl.BlockSpec((tm, D), lambda i, e: (i, 0)),
                  pl.BlockSpec((1, D), lambda i, e: (0, 0)),
                  pl.BlockSpec((D, LANES), lambda i, e: (0, 0)),
                  pl.BlockSpec((1, D, F), lambda i, e: (e, 0, 0)),
                  pl.BlockSpec((1, D, F), lambda i, e: (e, 0, 0)),
                  pl.BlockSpec((1, F, D), lambda i, e: (e, 0, 0))],
        out_specs=pl.BlockSpec((tm, D), lambda i, e: (i, 0)),
        out_shape=jax.ShapeDtypeStruct((T, D), F32),
        scratch_shapes=[pltpu.VMEM((tm, D), BF16), pltpu.VMEM((tm, LANES), F32), pltpu.VMEM((tm, LANES), F32),
                        pltpu.VMEM((LANES, tm), F32), pltpu.VMEM((1, LANES), F32)],
        compiler_params=_params("parallel", "arbitrary"),
        name="moe",
    )(x2, gain, router_pad, w1, w3, w2)


def kernel(x, norm_mix, w_in, rw_mu, rw_w0, rw_w2, rw_a0, rw_a2, rw_g2, rw_k_k, rw_k_a, rw_r_k, rw_ln_w, rw_ln_b, sb_q_norm, sb_k_norm, ret_gn, conv_dw, conv_b, conv_ln_w, conv_ln_b, w_gate, w_branch, w_out, norm_ffn, ffn_w1, ffn_w3, ffn_w2, router, moe_w1, moe_w3, moe_w2):
    B, S, D = x.shape
    depth = norm_mix.shape[0]
    x2 = x.reshape(B * S, D)
    tables = _retention_tables(S)
    row = lambda t: t.reshape(1, -1)
    for l in range(depth):
        p = _inproj(x2, row(norm_mix[l]), w_in[l].astype(BF16))
        zeros = jnp.zeros_like(rw_w2[l])
        wa2 = jnp.concatenate([jnp.concatenate([rw_w2[l], zeros], axis=1),
                               jnp.concatenate([zeros, rw_a2[l]], axis=1)], axis=0).astype(BF16)
        y_rw = _rwkv(p, B, S, row(rw_mu[l]), row(rw_w0[l]), row(rw_a0[l]), wa2, rw_g2[l].astype(BF16),
                     row(rw_k_k[l]), row(rw_k_a[l]), row(rw_r_k[l]), row(rw_ln_w[l]), row(rw_ln_b[l]))
        y_sb = _stick_breaking(p, B, S, row(jnp.tile(sb_q_norm[l], 2)), row(jnp.tile(sb_k_norm[l], 2)))
        y_ret = _retention(p, B, S, row(ret_gn[l]), tables)
        dw = jnp.concatenate([conv_dw[l], jnp.zeros((CONV_HALO - CONV_WIDTH, WIDTH), F32)], axis=0)
        y_conv = _conformer_conv(p, B, S, dw, row(conv_b[l]), row(conv_ln_w[l]), row(conv_ln_b[l]))
        x2 = _merge(x2, row(norm_mix[l]), (y_rw, y_sb, y_ret, y_conv),
                    w_gate[l].astype(BF16), w_branch[l].astype(BF16), w_out[l].astype(BF16))
        if l % 2 == 0:
            x2 = _ffn(x2, row(norm_ffn[l]), ffn_w1[l // 2].astype(BF16), ffn_w3[l // 2].astype(BF16),
                      ffn_w2[l // 2].astype(BF16))
        else:
            rt = jnp.concatenate([router[l // 2], jnp.zeros((D, LANES - N_EXPERTS), F32)], axis=1)
            x2 = _moe(x2, row(norm_ffn[l]), rt, moe_w1[l // 2].astype(BF16), moe_w3[l // 2].astype(BF16),
                      moe_w2[l // 2].astype(BF16))
    return x2.reshape(B, S, D)
```

```python
import functools

import jax
import jax.numpy as jnp
from jax import lax
from jax.experimental import pallas as pl
from jax.experimental.pallas import tpu as pltpu

F32 = jnp.float32
BF16 = jnp.bfloat16

LANES = 128
SUBLANES = 8
VMEM_LIMIT = 56 * 1024 * 1024

D_MODEL = 1024
HEAD_DIM = 64
HEAD_SHIFT = HEAD_DIM.bit_length() - 1
N_HEADS = 4
WIDTH = N_HEADS * HEAD_DIM
NORM_EPS = 1e-6
LOG2_E = 1.4426950408889634
RW_DECAY_SCALE = 0.606531
RW_LN_EPS = 64e-5
RW_CHUNK = 64
RW_BASE = 8
RW_MIN_NORM = 1e-12
RW_DECAY_LORA = 64
BF16_ROUNDING = 2.0 ** -8
SB_ZERO_EXPONENT = 152.0
SB_NORM_MARGIN = 1.01
RET_CHUNK = 128
RET_GN_EPS = 1e-5
ROPE_BASE = 10000.0
CONV_WIDTH = 31
CONV_HALO = 32
CONV_LN_EPS = 1e-5
N_EXPERTS = 8
N_IN = 3328
COL_RW = 0
COL_SB_Q, COL_SB_K, COL_SB_V = 8, 10, 12
COL_RET_Q, COL_RET_K, COL_RET_V, COL_RET_G = 14, 16, 18, 20
COL_CONV_A, COL_CONV_B = 11, 12


def _nn(a, b):
    return lax.dot_general(a, b, (((1,), (0,)), ((), ())), preferred_element_type=F32)


def _nt(a, b):
    return lax.dot_general(a, b, (((1,), (1,)), ((), ())), preferred_element_type=F32)


def _mm(a, b):
    return _nn(a.astype(BF16), b.astype(BF16))


def _split(x):
    hi = x.astype(BF16)
    lo = (x - hi.astype(F32)).astype(BF16)
    return hi, lo


def _nn_rhs_exact(x, m):
    hi, lo = _split(x)
    return _nn(hi, m) + _nn(lo, m)


def _nn_lhs_exact(m, x):
    hi, lo = _split(x)
    return _nn(m, hi) + _nn(m, lo)


def _nn_3pass(a, b):
    ah, al = _split(a)
    bh, bl = _split(b)
    return _nn(ah, bh) + (_nn(ah, bl) + _nn(al, bh))


def _iota(shape, axis):
    return lax.broadcasted_iota(jnp.int32, shape, axis)


def _head_block_matrix(n, value):
    same = (_iota((n, n), 0) >> HEAD_SHIFT) == (_iota((n, n), 1) >> HEAD_SHIFT)
    return jnp.where(same, value, 0.0).astype(BF16)


def _sigmoid(x):
    return 0.5 * jnp.tanh(0.5 * x) + 0.5


def _rmsnorm_rows(x, gain):
    return x * lax.rsqrt(jnp.mean(x * x, axis=-1, keepdims=True) + NORM_EPS) * gain


def _params(*sem):
    return pltpu.CompilerParams(dimension_semantics=sem, vmem_limit_bytes=VMEM_LIMIT)


def _inproj_kernel(x_ref, g_ref, w_ref, p_ref):
    h = _rmsnorm_rows(x_ref[...], g_ref[...])
    p_ref[...] = _nn(h.astype(BF16), w_ref[...])


def _inproj(x2, gain, w_bf16, tm=512):
    T, D = x2.shape
    N = w_bf16.shape[1]
    return pl.pallas_call(
        _inproj_kernel,
        grid=(T // tm,),
        in_specs=[pl.BlockSpec((tm, D), lambda i: (i, 0)),
                  pl.BlockSpec((1, D), lambda i: (0, 0)),
                  pl.BlockSpec((D, N), lambda i: (0, 0))],
        out_specs=pl.BlockSpec((tm, N), lambda i: (i, 0)),
        out_shape=jax.ShapeDtypeStruct((T, N), F32),
        compiler_params=_params("parallel"),
        name="inproj",
    )(x2, gain, w_bf16)


def _rw_kernel(p_ref, mu_ref, w0_ref, a0_ref, wa2_ref, g2_ref, kk_ref, ka_ref, rk_ref, lnw_ref, lnb_ref,
               o_ref, prev_ref, zt_ref, at_s, bt_s, kt_s, rt_s, qh_s, gmat_s, v_s, ecl_s, y0_s, y_s, bonus_s, g_s, hmat_s,
               *, ts, nbatch, group):
    C = RW_CHUNK
    W = WIDTH
    i = pl.program_id(0)

    @pl.when(i == 0)
    def _():
        prev_ref[...] = jnp.zeros_like(prev_ref)
        zt_ref[...] = jnp.zeros_like(zt_ref)

    lane = _iota((1, LANES), 1)
    row = _iota((ts, 1), 0)
    r4 = _iota((W, W), 0)
    c4 = _iota((W, W), 1)
    same_head = (r4 >> HEAD_SHIFT) == (c4 >> HEAD_SHIFT)
    tri4 = jnp.where(same_head & (c4 <= r4), 1.0, 0.0).astype(BF16)
    ones_bd = _head_block_matrix(W, 1.0)

    for b in range(nbatch):
        p = p_ref[b]
        shifted = jnp.where(row == 0, prev_ref[b, 0:1, :], pltpu.roll(p, 1, 0))
        prev_ref[b, 0:1, :] = p_ref[b, ts - 1:ts, :]
        pm = p + (shifted - p) * mu_ref[...]
        r = pm[:, 0:W]
        k = pm[:, W:2 * W]
        v = pm[:, 2 * W:3 * W]
        lora = pm[:, 3 * W:3 * W + LANES]
        pg = pm[:, 3 * W + LANES:]
        wa = _mm(jnp.where(lane < RW_DECAY_LORA, jnp.tanh(lora), lora), wa2_ref[...])
        lw = -RW_DECAY_SCALE * _sigmoid(w0_ref[...] + wa[:, 0:W])
        a = _sigmoid(a0_ref[...] + wa[:, W:])
        g_s[b] = _mm(_sigmoid(pg), g2_ref[...])
        kk = k * kk_ref[...]
        kk = kk * jnp.minimum(lax.rsqrt(_nn_rhs_exact(kk * kk, ones_bd)), 1.0 / RW_MIN_NORM)
        kp = k * (1.0 + (a - 1.0) * ka_ref[...])
        bonus_s[b] = _nn_rhs_exact(r * kp * rk_ref[...], ones_bd) * v
        cl = jnp.concatenate([_nn_lhs_exact(tri4, lw[n * W:(n + 1) * W, :]) for n in range(ts // W)], axis=0)
        ecl = jnp.exp(cl)
        eml = jnp.exp(-cl)
        at_s[b] = (-kk * jnp.exp(cl - lw)).astype(BF16)
        bt_s[b] = (kk * a * eml).astype(BF16)
        kt_s[b] = (kp * eml).astype(BF16)
        rt_s[b] = (r * ecl).astype(BF16)
        v_s[b] = v
        ecl_s[b] = ecl

    head = _iota((1, W), 1) >> HEAD_SHIFT
    tw = _iota((C, W), 0)
    sw = _iota((C, W), 1) & (C - 1)
    strict = tw > sw
    incl = tw >= sw
    eye = jnp.where(tw == sw, 1.0, 0.0)

    def same_block(size):
        return (tw & -size) == (sw & -size)

    def stack4(x):
        return jnp.concatenate([jnp.where(head == h, x, jnp.zeros_like(x)) for h in range(N_HEADS)], axis=0)

    def block_diag(x):
        return jnp.concatenate([x.astype(BF16)] * N_HEADS, axis=0) * ones_bd

    chunks = [(b, slice(c * C, (c + 1) * C)) for c in range(ts // C) for b in range(nbatch)]
    for g0 in range(0, len(chunks), group):
        grp = chunks[g0:g0 + group]
        n = range(len(grp))
        at = [at_s[b, rows, :] for b, rows in grp]
        rt = [rt_s[b, rows, :] for b, rows in grp]
        b4 = [stack4(bt_s[b, rows, :]) for b, rows in grp]
        k4 = [stack4(kt_s[b, rows, :]) for b, rows in grp]
        v4 = [stack4(v_s[b, rows, :].astype(BF16)) for b, rows in grp]
        lab = [jnp.where(strict, _nt(at[j], b4[j]), 0.0) for j in n]
        lak = [jnp.where(strict, _nt(at[j], k4[j]), 0.0) for j in n]
        mrb = [jnp.where(incl, _nt(rt[j], b4[j]), 0.0) for j in n]
        mrk = [jnp.where(incl, _nt(rt[j], k4[j]), 0.0) for j in n]
        m = [jnp.where(same_block(RW_BASE), lab[j], 0.0) for j in n]
        tinv = [eye + m[j] for j in n]
        for _ in range(2):
            m = [_nn(m[j].astype(BF16), block_diag(m[j])) for j in n]
            tinv = [tinv[j] + _nn(tinv[j].astype(BF16), block_diag(m[j])) for j in n]
        size = RW_BASE
        while size < C:
            below = same_block(2 * size) & jnp.logical_not(same_block(size))
            e_bd = [block_diag(jnp.where(below, lab[j], 0.0)) for j in n]
            half = [_nn(tinv[j].astype(BF16), e_bd[j]) for j in n]
            tinv = [tinv[j] + _nn(half[j].astype(BF16), block_diag(tinv[j])) for j in n]
            size *= 2
        tb = [tinv[j].astype(BF16) for j in n]
        mb = [mrb[j].astype(BF16) for j in n]
        w = [_nn(tb[j], stack4(at[j])) for j in n]
        u0 = [_nn(tb[j], stack4(_nn(lak[j].astype(BF16), v4[j])).astype(BF16)) for j in n]
        qh = [rt[j].astype(F32) + _nn(mb[j], stack4(w[j]).astype(BF16)) for j in n]
        y0 = [_nn(mb[j], stack4(u0[j]).astype(BF16)) + _nn(mrk[j].astype(BF16), v4[j]) for j in n]
        pad = jnp.zeros((C, W), F32)
        for j, (b, rows) in enumerate(grp):
            c = rows.start // C
            wt = jnp.concatenate([w[j], pad], axis=0).T.astype(BF16)
            uvt = jnp.concatenate([u0[j], v_s[b, rows, :]], axis=0).T.astype(BF16)
            bt = bt_s[b, rows, :]
            gmat = _nn(wt, jnp.concatenate([bt, jnp.zeros_like(bt)], axis=0))
            hmat = _nn(uvt, jnp.concatenate([bt, kt_s[b, rows, :]], axis=0))
            gmat_s[b, c] = jnp.where(same_head, gmat, 0.0).astype(BF16)
            hmat_s[b, c] = jnp.where(same_head, hmat, 0.0)
            qh_s[b, rows, :] = qh[j].astype(BF16)
            y0_s[b, rows, :] = y0[j]

    def chunk(c, carry):
        rows = pl.ds(pl.multiple_of(c * C, C), C)
        for b in range(nbatch):
            zt = zt_ref[b]
            ztb = zt.astype(BF16)
            y_s[b, rows, :] = _nt(qh_s[b, rows, :], ztb) + y0_s[b, rows, :]
            g_end = ecl_s[b, pl.ds(c * C + C - 1, 1), :]
            zt_ref[b] = (zt + _nn(ztb, gmat_s[b, c]) + hmat_s[b, c]) * g_end
        return carry

    lax.fori_loop(0, ts // C, chunk, 0)

    mean_bd = _head_block_matrix(W, 1.0 / HEAD_DIM)
    for b in range(nbatch):
        y = y_s[b]
        yc = y - _nn_rhs_exact(y, mean_bd)
        var = _nn_rhs_exact(yc * yc, mean_bd)
        y = yc * lax.rsqrt(var + RW_LN_EPS) * lnw_ref[...] + lnb_ref[...]
        o_ref[b] = (y + bonus_s[b]) * g_s[b]


def _rwkv(p, B, S, mu, w0, a0, wa2, g2, k_k, k_a, r_k, ln_w, ln_b, ts=512, group=8):
    ts = min(ts, S)
    W = WIDTH
    vec = lambda n: pl.BlockSpec((1, n), lambda i: (0, 0))
    full = lambda a: pl.BlockSpec(a.shape, lambda i: (0, 0))
    out = pl.pallas_call(
        functools.partial(_rw_kernel, ts=ts, nbatch=B, group=group),
        grid=(S // ts,),
        in_specs=[pl.BlockSpec((B, ts, 4 * W), lambda i: (0, i, COL_RW)),
                  vec(4 * W), vec(W), vec(W), full(wa2), full(g2), vec(W), vec(W), vec(W), vec(W), vec(W)],
        out_specs=pl.BlockSpec((B, ts, W), lambda i: (0, i, 0)),
        out_shape=jax.ShapeDtypeStruct((B, S, W), F32),
        scratch_shapes=[pltpu.VMEM((B, 8, 4 * W), F32), pltpu.VMEM((B, W, W), F32)]
                       + [pltpu.VMEM((B, ts, W), BF16)] * 5 + [pltpu.VMEM((B, ts // RW_CHUNK, W, W), BF16)]
                       + [pltpu.VMEM((B, ts, W), F32)] * 6 + [pltpu.VMEM((B, ts // RW_CHUNK, W, W), F32)],
        compiler_params=_params("arbitrary"),
        name="rwkv7",
    )(p.reshape(B, S, -1), mu, w0, a0, wa2, g2, k_k, k_a, r_k, ln_w, ln_b)
    return out.reshape(B * S, W)


def _sb_kernel(q_ref, k_ref, v_ref, qg_ref, kg_ref, o_ref, kb, vb, acc_o, acc_l, knorm,
               *, tq, tk, seq, prep, nsub):
    i = pl.program_id(2)
    lane = _iota((1, LANES), 1)
    mean_bd = _head_block_matrix(LANES, 1.0 / HEAD_DIM)
    ones = jnp.ones((LANES, LANES), BF16)

    @pl.when(i == 0)
    def _():
        knorm[...] = jnp.zeros_like(knorm)

        def body(c, carry):
            rows = pl.ds(pl.multiple_of(c * prep, prep), prep)
            kf = k_ref[rows, :]
            ms = _nn_rhs_exact(kf * kf, mean_bd)
            kn = (kf * lax.rsqrt(ms + NORM_EPS) * kg_ref[...]).astype(BF16)
            kb[rows, :] = kn
            vb[rows, :] = v_ref[rows, :].astype(BF16)
            kn = kn.astype(F32)
            sq = _nn_rhs_exact(kn * kn, ones)
            knorm[...] = jnp.maximum(knorm[...], jnp.max(sq.reshape(prep // 8, 8, LANES), axis=0))
            return carry
        lax.fori_loop(0, seq // prep, body, 0)

    subs = range(nsub)
    tile = [i * nsub + s for s in subs]
    q = q_ref[...]
    ms = _nn_rhs_exact(q * q, mean_bd)
    qn = q * lax.rsqrt(ms + NORM_EPS) * qg_ref[...] * (HEAD_DIM ** -0.5 * LOG2_E)
    q_lo = jnp.where(lane < HEAD_DIM, qn, 0.0).astype(BF16)
    q_hi = jnp.where(lane >= HEAD_DIM, qn, 0.0).astype(BF16)
    q2 = [jnp.concatenate([q_lo[s * tq:(s + 1) * tq], q_hi[s * tq:(s + 1) * tq]], axis=0) for s in subs]

    qf = (q_lo + q_hi).astype(F32)
    qsq = jnp.max(_nn_rhs_exact(qf * qf, ones))
    zmax = jnp.sqrt(qsq * jnp.max(knorm[...])) * SB_NORM_MARGIN
    stop_at = SB_ZERO_EXPONENT + zmax * BF16_ROUNDING

    def softplus2(z2):
        return jnp.maximum(z2, 0.0) + jnp.log2(1.0 + jnp.exp2(-jnp.abs(z2)))

    def rev_incl(n):
        return jnp.where(_iota((n, n), 0) >= _iota((n, n), 1), 1.0, 0.0).astype(BF16)

    prev = [pl.ds(pl.multiple_of(jnp.maximum(tile[s] - 1, 0) * tk, tk), tk) for s in subs]
    diag = [pl.ds(pl.multiple_of(tile[s] * tk, tk), tk) for s in subs]
    col = _iota((2 * tq, 2 * tk), 1)
    causal = (col - tk) < (_iota((2 * tq, 2 * tk), 0) & (tq - 1))
    valid = [causal & (col >= jnp.where(i > 0, 0, tk))] + [causal] * (nsub - 1)
    rev = rev_incl(tk)
    z2 = [_nt(q2[s], jnp.concatenate([kb[prev[s], :], kb[diag[s], :]], axis=0)) for s in subs]
    sp = [jnp.where(valid[s], softplus2(z2[s]), 0.0).astype(BF16) for s in subs]
    cum_diag = [_nn(sp[s][:, tk:], rev) for s in subs]
    cum = [jnp.concatenate([_nn(sp[s][:, :tk], rev) + cum_diag[s][:, 0:1], cum_diag[s]], axis=1) for s in subs]
    attn = [jnp.where(valid[s], jnp.exp2(z2[s] - cum[s]), 0.0).astype(BF16) for s in subs]
    for s in subs:
        acc_o[s] = _nn(attn[s], jnp.concatenate([vb[prev[s], :], vb[diag[s], :]], axis=0))
        acc_l[s] = jnp.broadcast_to(cum[s][:, 0:1], (2 * tq, LANES))

    for s in subs:
        def step(j, s=s):
            rows = pl.ds(pl.multiple_of(j * tk, tk), tk)
            z2 = _nt(q2[s], kb[rows, :])
            cum = _nn(softplus2(z2).astype(BF16), rev)
            seen = acc_l[s]
            attn = jnp.exp2(z2 - cum - jnp.concatenate([seen] * (tk // LANES), axis=1))
            acc_o[s] += _nn(attn.astype(BF16), vb[rows, :])
            seen = seen + cum[:, 0:1]
            acc_l[s] = seen
            return jnp.min(seen)

        def cond(carry, s=s):
            t, low = carry
            return (t < tile[s]) & (low < stop_at)

        def body(carry, s=s, step=step):
            t, _ = carry
            return t + 1, step(tile[s] - 1 - t)
        lax.while_loop(cond, body, (1, jnp.min(cum[s][:, 0:1])))

    for s in subs:
        o_ref[s * tq:(s + 1) * tq, :] = jnp.where(lane < HEAD_DIM, acc_o[s, 0:tq, :], acc_o[s, tq:2 * tq, :])


def _stick_breaking(p, B, S, qg, kg, tq=256, tk=256, nsub=2):
    assert tq == tk and S % (nsub * tq) == 0
    nq = S // (nsub * tq)
    T = B * S
    prep = min(512, S)
    return pl.pallas_call(
        functools.partial(_sb_kernel, tq=tq, tk=tk, seq=S, prep=prep, nsub=nsub),
        grid=(B, 2, nq),
        in_specs=[pl.BlockSpec((nsub * tq, LANES), lambda b, h, i: (b * nq + i, COL_SB_Q + h)),
                  pl.BlockSpec((S, LANES), lambda b, h, i: (b, COL_SB_K + h)),
                  pl.BlockSpec((S, LANES), lambda b, h, i: (b, COL_SB_V + h)),
                  pl.BlockSpec((1, LANES), lambda b, h, i: (0, 0)),
                  pl.BlockSpec((1, LANES), lambda b, h, i: (0, 0))],
        out_specs=pl.BlockSpec((nsub * tq, LANES), lambda b, h, i: (b * nq + i, h)),
        out_shape=jax.ShapeDtypeStruct((T, WIDTH), F32),
        scratch_shapes=[pltpu.VMEM((S, LANES), BF16), pltpu.VMEM((S, LANES), BF16),
                        pltpu.VMEM((nsub, 2 * tq, LANES), F32), pltpu.VMEM((nsub, 2 * tq, LANES), F32),
                        pltpu.VMEM((8, LANES), F32)],
        compiler_params=_params("arbitrary", "arbitrary", "arbitrary"),
        name="stick_breaking",
    )(p, p, p, qg, kg)


def _ret_kernel(q_ref, k_ref, v_ref, gt_ref, cos_ref, sin_ref, gn_ref, dm_ref, kdec_ref, qdec_ref, cdec_ref,
                o_ref, st_ref, *, ts):
    C = RET_CHUNK
    nch = ts // C
    pairs = range(2)
    chunks = range(nch)
    i = pl.program_id(1)

    @pl.when(i == 0)
    def _():
        st_ref[...] = jnp.zeros_like(st_ref)

    lane = _iota((1, LANES), 1)
    half = HEAD_DIM // 2
    first_half = (lane & (HEAD_DIM - 1)) < half
    same_head = (_iota((LANES, LANES), 0) >> HEAD_SHIFT) == (_iota((LANES, LANES), 1) >> HEAD_SHIFT)
    mean_bd = _head_block_matrix(LANES, 1.0 / HEAD_DIM)
    cos = cos_ref[...]
    sin = sin_ref[...]

    def rotary(t):
        swapped = jnp.where(first_half, pltpu.roll(t, LANES - half, 1), pltpu.roll(t, half, 1))
        return t * cos + swapped * sin

    def rows(t, c):
        return t[c * C:(c + 1) * C]

    cols = [slice(h * LANES, (h + 1) * LANES) for h in pairs]
    q = [rotary(q_ref[:, cols[h]]) for h in pairs]
    k = [rotary(k_ref[:, cols[h]]) * (HEAD_DIM ** -0.5) for h in pairs]
    vb = [v_ref[:, cols[h]].astype(BF16) for h in pairs]
    kb = [k[h].astype(BF16) for h in pairs]
    q_lo = [jnp.where(lane < HEAD_DIM, q[h], 0.0).astype(BF16) for h in pairs]
    q_hi = [jnp.where(lane >= HEAD_DIM, q[h], 0.0).astype(BF16) for h in pairs]
    qd = [(q[h] * jnp.concatenate([qdec_ref[h]] * nch, axis=0)).astype(BF16) for h in pairs]
    kd = [k[h] * jnp.concatenate([kdec_ref[h]] * nch, axis=0) for h in pairs]

    hc = [(h, c) for h in pairs for c in chunks]
    scores = {(h, c): _nt(jnp.concatenate([rows(q_lo[h], c), rows(q_hi[h], c)], axis=0), rows(kb[h], c)) * dm_ref[h]
              for h, c in hc}
    kv = {(h, c): _nn(rows(kd[h], c).T.astype(BF16), rows(vb[h], c)) for h, c in hc}
    intra2 = {(h, c): _nn(scores[h, c].astype(BF16), rows(vb[h], c)) for h, c in hc}
    state = {}
    for h in pairs:
        st = st_ref[h]
        for c in chunks:
            state[h, c] = st
            st = cdec_ref[h] * st + jnp.where(same_head, kv[h, c], 0.0)
        st_ref[h] = st
    inter = {(h, c): _nn(rows(qd[h], c), state[h, c].astype(BF16)) for h, c in hc}
    for h in pairs:
        o = jnp.concatenate([jnp.where(lane < HEAD_DIM, intra2[h, c][0:C], intra2[h, c][C:2 * C]) + inter[h, c]
                             for c in chunks], axis=0)
        oc = o - _nn_rhs_exact(o, mean_bd)
        var = _nn_rhs_exact(oc * oc, mean_bd)
        on = oc * lax.rsqrt(var + RET_GN_EPS) * gn_ref[:, cols[h]]
        gt = gt_ref[:, cols[h]]
        o_ref[:, cols[h]] = on * (gt * _sigmoid(gt))


def _retention_tables(S):
    C = RET_CHUNK
    inv_freq = ROPE_BASE ** (-jnp.arange(0, HEAD_DIM, 2, dtype=F32) / HEAD_DIM)
    ang = jnp.arange(S, dtype=F32)[:, None] * inv_freq[None, :]
    cos = jnp.tile(jnp.cos(ang), (1, 4))
    sin = jnp.tile(jnp.concatenate([-jnp.sin(ang), jnp.sin(ang)], axis=1), (1, 2))
    log_gamma = jnp.log(1.0 - 2.0 ** (-5.0 - jnp.arange(N_HEADS, dtype=F32)))
    idx = jnp.arange(C, dtype=F32)
    rel = idx[:, None] - idx[None, :]
    intra = jnp.where(rel >= 0, jnp.exp(jnp.maximum(rel, 0.0) * log_gamma[:, None, None]), 0.0)
    dm = intra.reshape(2, 2 * C, C)
    lanes = lambda t: jnp.repeat(t.reshape(2, 2, C).transpose(0, 2, 1), HEAD_DIM, axis=2)
    kdec = lanes(jnp.exp((C - 1 - idx)[None, :] * log_gamma[:, None]))
    qdec = lanes(jnp.exp((idx + 1.0)[None, :] * log_gamma[:, None]))
    cdec = jnp.repeat(jnp.exp(C * log_gamma).reshape(2, 1, 2), HEAD_DIM, axis=2)
    return cos, sin, dm, kdec, qdec, cdec


def _retention(p, B, S, gn, tables, ts=1024):
    ts = min(ts, S)
    nb = S // ts
    T = B * S
    C = RET_CHUNK
    cos, sin, dm, kdec, qdec, cdec = tables
    col = lambda c0: pl.BlockSpec((ts, WIDTH), lambda b, i: (b * nb + i, c0 // 2))
    tab = pl.BlockSpec((ts, LANES), lambda b, i: (i, 0))
    full = lambda a: pl.BlockSpec(a.shape, lambda b, i: (0,) * a.ndim)
    return pl.pallas_call(
        functools.partial(_ret_kernel, ts=ts),
        grid=(B, nb),
        in_specs=[col(COL_RET_Q), col(COL_RET_K), col(COL_RET_V), col(COL_RET_G), tab, tab,
                  full(gn), full(dm), full(kdec), full(qdec), full(cdec)],
        out_specs=pl.BlockSpec((ts, WIDTH), lambda b, i: (b * nb + i, 0)),
        out_shape=jax.ShapeDtypeStruct((T, WIDTH), F32),
        scratch_shapes=[pltpu.VMEM((2, LANES, LANES), F32)],
        compiler_params=_params("arbitrary", "arbitrary"),
        name="retention",
    )(p, p, p, p, cos, sin, gn, dm, kdec, qdec, cdec)


def _conv_kernel(ua_ref, ub_ref, dw_ref, db_ref, lnw_ref, lnb_ref, o_ref, buf, shifted, *, ts):
    i = pl.program_id(1)
    span = ts + CONV_HALO - SUBLANES

    @pl.when(i == 0)
    def _():
        buf[0:CONV_HALO, :] = jnp.zeros((CONV_HALO, WIDTH), F32)

    buf[CONV_HALO:, :] = ua_ref[...] * _sigmoid(ub_ref[...])
    for s in range(1, SUBLANES):
        shifted[s] = buf[s:s + span, :]
    acc = jnp.zeros((ts, WIDTH), F32) + db_ref[...]
    for j in range(CONV_WIDTH):
        start = CONV_HALO - (CONV_WIDTH - 1) + j
        s = start % SUBLANES
        window = buf[start:start + ts, :] if s == 0 else shifted[s, start - s:start - s + ts, :]
        acc = acc + dw_ref[j:j + 1, :] * window
    buf[0:CONV_HALO, :] = buf[ts:ts + CONV_HALO, :]
    xc = acc - jnp.mean(acc, axis=-1, keepdims=True)
    var = jnp.mean(xc * xc, axis=-1, keepdims=True)
    y = xc * lax.rsqrt(var + CONV_LN_EPS) * lnw_ref[...] + lnb_ref[...]
    o_ref[...] = y * _sigmoid(y)


def _conformer_conv(p, B, S, dw, db, ln_w, ln_b, ts=1024):
    ts = min(ts, S)
    nb = S // ts
    T = B * S
    vec = pl.BlockSpec((1, WIDTH), lambda b, i: (0, 0))
    return pl.pallas_call(
        functools.partial(_conv_kernel, ts=ts),
        grid=(B, nb),
        in_specs=[pl.BlockSpec((ts, WIDTH), lambda b, i: (b * nb + i, COL_CONV_A)),
                  pl.BlockSpec((ts, WIDTH), lambda b, i: (b * nb + i, COL_CONV_B)),
                  pl.BlockSpec((CONV_HALO, WIDTH), lambda b, i: (0, 0)), vec, vec, vec],
        out_specs=pl.BlockSpec((ts, WIDTH), lambda b, i: (b * nb + i, 0)),
        out_shape=jax.ShapeDtypeStruct((T, WIDTH), F32),
        scratch_shapes=[pltpu.VMEM((ts + CONV_HALO, WIDTH), F32),
                        pltpu.VMEM((SUBLANES, ts + CONV_HALO - SUBLANES, WIDTH), F32)],
        compiler_params=_params("arbitrary", "arbitrary"),
        name="conformer_conv",
    )(p, p, dw, db, ln_w, ln_b)


def _merge_kernel(x_ref, g_ref, y0_ref, y1_ref, y2_ref, y3_ref, wg_ref, wb_ref, wo_ref, o_ref):
    branch = [_nn(y_ref[...].astype(BF16), wb_ref[n]) for n, y_ref in enumerate((y0_ref, y1_ref, y2_ref, y3_ref))]
    x = x_ref[...]
    h = _rmsnorm_rows(x, g_ref[...]).astype(BF16)
    merged = None
    for n in range(len(branch)):
        term = _sigmoid(_nn(h, wg_ref[n])) * branch[n]
        merged = term if merged is None else merged + term
    o_ref[...] = x + _nn(merged.astype(BF16), wo_ref[...])


def _merge(x2, gain, ys, wg, wb, wo, tm=512):
    T, D = x2.shape
    tm = min(tm, T)
    row = lambda n: pl.BlockSpec((tm, n), lambda i: (i, 0))
    return pl.pallas_call(
        _merge_kernel,
        grid=(T // tm,),
        in_specs=[row(D), pl.BlockSpec((1, D), lambda i: (0, 0)), row(WIDTH), row(WIDTH), row(WIDTH), row(WIDTH),
                  pl.BlockSpec(wg.shape, lambda i: (0, 0, 0)),
                  pl.BlockSpec(wb.shape, lambda i: (0, 0, 0)),
                  pl.BlockSpec(wo.shape, lambda i: (0, 0))],
        out_specs=row(D),
        out_shape=jax.ShapeDtypeStruct((T, D), F32),
        compiler_params=_params("parallel"),
        name="merge",
    )(x2, gain, *ys, wg, wb, wo)


def _ffn_kernel(x_ref, g_ref, w1_ref, w3_ref, w2_ref, o_ref, h_s):
    f = pl.program_id(1)

    @pl.when(f == 0)
    def _():
        x = x_ref[...]
        h_s[...] = _rmsnorm_rows(x, g_ref[...]).astype(BF16)
        o_ref[...] = x

    h = h_s[...]
    a = _nn(h, w1_ref[...])
    b = _nn(h, w3_ref[...])
    o_ref[...] += _nn((a * _sigmoid(a) * b).astype(BF16), w2_ref[...])


def _ffn(x2, gain, w1, w3, w2, tm=1024, tf=1408):
    T, D = x2.shape
    tm = min(tm, T)
    nf = w1.shape[1] // tf
    return pl.pallas_call(
        _ffn_kernel,
        grid=(T // tm, nf),
        in_specs=[pl.BlockSpec((tm, D), lambda i, f: (i, 0)),
                  pl.BlockSpec((1, D), lambda i, f: (0, 0)),
                  pl.BlockSpec((D, tf), lambda i, f: (0, f)),
                  pl.BlockSpec((D, tf), lambda i, f: (0, f)),
                  pl.BlockSpec((tf, D), lambda i, f: (f, 0))],
        out_specs=pl.BlockSpec((tm, D), lambda i, f: (i, 0)),
        out_shape=jax.ShapeDtypeStruct((T, D), F32),
        scratch_shapes=[pltpu.VMEM((tm, D), BF16)],
        compiler_params=_params("parallel", "arbitrary"),
        name="ffn",
    )(x2, gain, w1, w3, w2)


def _moe_kernel(x_ref, g_ref, rt_ref, w1_ref, w3_ref, w2_ref, o_ref, h_s, gate_s, rank_s, rank_t_s, count_s,
                *, tm, cap, small):
    e = pl.program_id(1)
    lane = _iota((1, LANES), 1)

    @pl.when(e == 0)
    def _():
        x = x_ref[...]
        h = _rmsnorm_rows(x, g_ref[...])
        h_s[...] = h.astype(BF16)
        o_ref[...] = x
        logits = jnp.where(lane < N_EXPERTS, _nn_3pass(h, rt_ref[...]), -jnp.inf)
        lane_f = lane.astype(F32)
        m1 = jnp.max(logits, axis=-1, keepdims=True)
        i1 = jnp.min(jnp.where(logits == m1, lane_f, float(LANES)), axis=-1, keepdims=True)
        rest = jnp.where(lane_f == i1, -jnp.inf, logits)
        m2 = jnp.max(rest, axis=-1, keepdims=True)
        i2 = jnp.min(jnp.where(rest == m2, lane_f, float(LANES)), axis=-1, keepdims=True)
        e2 = jnp.exp(m2 - m1)
        den = 1.0 + e2
        gate_s[...] = jnp.where(lane_f == i1, 1.0 / den, 0.0) + jnp.where(lane_f == i2, e2 / den, 0.0)
        chosen = jnp.where((lane_f == i1) | (lane_f == i2), 1.0, 0.0)
        before = jnp.where(_iota((tm, tm), 1) < _iota((tm, tm), 0), 1.0, 0.0).astype(BF16)
        rank = jnp.where(chosen > 0.0, _nn(before, chosen.astype(BF16)), -1.0)
        rank_s[...] = rank
        rank_t_s[...] = rank.T
        count_s[...] = jnp.sum(chosen, axis=0, keepdims=True)

    mine = lane == e
    gate = jnp.sum(jnp.where(mine, gate_s[...], 0.0), axis=-1, keepdims=True)
    rank_col = jnp.sum(jnp.where(mine, rank_s[...], 0.0), axis=-1, keepdims=True)
    rank_row = rank_t_s[pl.ds(e, 1), :]
    count = jnp.sum(jnp.where(mine, count_s[...], 0.0))
    def one_pass(first, rows):
        lanes = -(-rows // LANES) * LANES
        slot_rows = _iota((rows, tm), 0).astype(F32)
        slot_cols = _iota((tm, lanes), 1).astype(F32)
        take = jnp.where(rank_row - first == slot_rows, 1.0, 0.0).astype(BF16)
        hit = rank_col - first == slot_cols
        if lanes > rows:
            hit = hit & (slot_cols < rows)
        put = jnp.where(hit, 1.0, 0.0).astype(BF16)
        xg = _nn(take, h_s[...]).astype(BF16)
        a = _nn(xg, w1_ref[0])
        b = _nn(xg, w3_ref[0])
        y = _nn((a * _sigmoid(a) * b).astype(BF16), w2_ref[0]).astype(BF16)
        if lanes > rows:
            y = jnp.concatenate([y, jnp.zeros((lanes - rows, y.shape[1]), BF16)], axis=0)
        o_ref[...] += gate * _nn(put, y)

    lower = 0.0
    for size in small:
        @pl.when((count > lower) & (count <= size))
        def _(size=size):
            one_pass(jnp.float32(0.0), size)
        lower = float(size)

    @pl.when(count > lower)
    def _():
        def more(first):
            one_pass(first, cap)
            return first + cap
        lax.while_loop(lambda first: first < count, more, jnp.float32(0.0))


def _moe(x2, gain, router_pad, w1, w3, w2, tm=1024, cap=320, small=(256, 288)):
    T, D = x2.shape
    tm = min(tm, T)
    cap = min(cap, tm)
    small = tuple(s for s in small if s < cap)
    E, _, F = w1.shape
    return pl.pallas_call(
        functools.partial(_moe_kernel, tm=tm, cap=cap, small=small),
        grid=(T // tm, E),
        in_specs=[pl.BlockSpec((tm, D), lambda i, e: (i, 0)),
                  pl.BlockSpec((1, D), lambda i, e: (0, 0)),
                  pl.BlockSpec((D, LANES), lambda i, e: (0, 0)),
                  pl.BlockSpec((1, D, F), lambda i, e: (e, 0, 0)),
                  pl.BlockSpec((1, D, F), lambda i, e: (e, 0, 0)),
                  pl.BlockSpec((1, F, D), lambda i, e: (e, 0, 0))],
        out_specs=pl.BlockSpec((tm, D), lambda i, e: (i, 0)),
        out_shape=jax.ShapeDtypeStruct((T, D), F32),
        scratch_shapes=[pltpu.VMEM((tm, D), BF16), pltpu.VMEM((tm, LANES), F32), pltpu.VMEM((tm, LANES), F32),
                        pltpu.VMEM((LANES, tm), F32), pltpu.VMEM((1, LANES), F32)],
        compiler_params=_params("parallel", "arbitrary"),
        name="moe",
    )(x2, gain, router_pad, w1, w3, w2)


def kernel(x, norm_mix, w_in, rw_mu, rw_w0, rw_w2, rw_a0, rw_a2, rw_g2, rw_k_k, rw_k_a, rw_r_k, rw_ln_w, rw_ln_b, sb_q_norm, sb_k_norm, ret_gn, conv_dw, conv_b, conv_ln_w, conv_ln_b, w_gate, w_branch, w_out, norm_ffn, ffn_w1, ffn_w3, ffn_w2, router, moe_w1, moe_w3, moe_w2):
    B, S, D = x.shape
    depth = norm_mix.shape[0]
    x2 = x.reshape(B * S, D)
    tables = _retention_tables(S)
    row = lambda t: t.reshape(1, -1)
    for l in range(depth):
        p = _inproj(x2, row(norm_mix[l]), w_in[l].astype(BF16))
        zeros = jnp.zeros_like(rw_w2[l])
        wa2 = jnp.concatenate([jnp.concatenate([rw_w2[l], zeros], axis=1),
                               jnp.concatenate([zeros, rw_a2[l]], axis=1)], axis=0).astype(BF16)
        y_rw = _rwkv(p, B, S, row(rw_mu[l]), row(rw_w0[l]), row(rw_a0[l]), wa2, rw_g2[l].astype(BF16),
                     row(rw_k_k[l]), row(rw_k_a[l]), row(rw_r_k[l]), row(rw_ln_w[l]), row(rw_ln_b[l]))
        y_sb = _stick_breaking(p, B, S, row(jnp.tile(sb_q_norm[l], 2)), row(jnp.tile(sb_k_norm[l], 2)))
        y_ret = _retention(p, B, S, row(ret_gn[l]), tables)
        dw = jnp.concatenate([conv_dw[l], jnp.zeros((CONV_HALO - CONV_WIDTH, WIDTH), F32)], axis=0)
        y_conv = _conformer_conv(p, B, S, dw, row(conv_b[l]), row(conv_ln_w[l]), row(conv_ln_b[l]))
        x2 = _merge(x2, row(norm_mix[l]), (y_rw, y_sb, y_ret, y_conv),
                    w_gate[l].astype(BF16), w_branch[l].astype(BF16), w_out[l].astype(BF16))
        if l % 2 == 0:
            x2 = _ffn(x2, row(norm_ffn[l]), ffn_w1[l // 2].astype(BF16), ffn_w3[l // 2].astype(BF16),
                      ffn_w2[l // 2].astype(BF16))
        else:
            rt = jnp.concatenate([router[l // 2], jnp.zeros((D, LANES - N_EXPERTS), F32)], axis=1)
            x2 = _moe(x2, row(norm_ffn[l]), rt, moe_w1[l // 2].astype(BF16), moe_w3[l // 2].astype(BF16),
                      moe_w2[l // 2].astype(BF16))
    return x2.reshape(B, S, D)
```

```python
import functools

import jax
import jax.numpy as jnp
from jax import lax
from jax.experimental import pallas as pl
from jax.experimental.pallas import tpu as pltpu

F32 = jnp.float32
BF16 = jnp.bfloat16

LANES = 128
SUBLANES = 8
VMEM_LIMIT = 56 * 1024 * 1024

D_MODEL = 1024
HEAD_DIM = 64
HEAD_SHIFT = HEAD_DIM.bit_length() - 1
N_HEADS = 4
WIDTH = N_HEADS * HEAD_DIM
NORM_EPS = 1e-6
LOG2_E = 1.4426950408889634
RW_DECAY_SCALE = 0.606531
RW_LN_EPS = 64e-5
RW_CHUNK = 64
RW_BASE = 8
RW_MIN_NORM = 1e-12
RW_DECAY_LORA = 64
BF16_ROUNDING = 2.0 ** -8
SB_ZERO_EXPONENT = 152.0
SB_NORM_MARGIN = 1.01
RET_CHUNK = 128
RET_GN_EPS = 1e-5
ROPE_BASE = 10000.0
CONV_WIDTH = 31
CONV_HALO = 32
CONV_LN_EPS = 1e-5
N_EXPERTS = 8
N_IN = 3328
COL_RW = 0
COL_SB_Q, COL_SB_K, COL_SB_V = 8, 10, 12
COL_RET_Q, COL_RET_K, COL_RET_V, COL_RET_G = 14, 16, 18, 20
COL_CONV_A, COL_CONV_B = 11, 12


def _nn(a, b):
    return lax.dot_general(a, b, (((1,), (0,)), ((), ())), preferred_element_type=F32)


def _nt(a, b):
    return lax.dot_general(a, b, (((1,), (1,)), ((), ())), preferred_element_type=F32)


def _mm(a, b):
    return _nn(a.astype(BF16), b.astype(BF16))


def _split(x):
    hi = x.astype(BF16)
    lo = (x - hi.astype(F32)).astype(BF16)
    return hi, lo


def _nn_rhs_exact(x, m):
    hi, lo = _split(x)
    return _nn(hi, m) + _nn(lo, m)


def _nn_lhs_exact(m, x):
    hi, lo = _split(x)
    return _nn(m, hi) + _nn(m, lo)


def _nn_3pass(a, b):
    ah, al = _split(a)
    bh, bl = _split(b)
    return _nn(ah, bh) + (_nn(ah, bl) + _nn(al, bh))


def _iota(shape, axis):
    return lax.broadcasted_iota(jnp.int32, shape, axis)


def _head_block_matrix(n, value):
    same = (_iota((n, n), 0) >> HEAD_SHIFT) == (_iota((n, n), 1) >> HEAD_SHIFT)
    return jnp.where(same, value, 0.0).astype(BF16)


def _sigmoid(x):
    return 0.5 * jnp.tanh(0.5 * x) + 0.5


def _rmsnorm_rows(x, gain):
    return x * lax.rsqrt(jnp.mean(x * x, axis=-1, keepdims=True) + NORM_EPS) * gain


def _params(*sem):
    return pltpu.CompilerParams(dimension_semantics=sem, vmem_limit_bytes=VMEM_LIMIT)


def _inproj_kernel(x_ref, g_ref, w_ref, p_ref):
    h = _rmsnorm_rows(x_ref[...], g_ref[...])
    p_ref[...] = _nn(h.astype(BF16), w_ref[...])


def _inproj(x2, gain, w_bf16, tm=512):
    T, D = x2.shape
    N = w_bf16.shape[1]
    return pl.pallas_call(
        _inproj_kernel,
        grid=(T // tm,),
        in_specs=[pl.BlockSpec((tm, D), lambda i: (i, 0)),
                  pl.BlockSpec((1, D), lambda i: (0, 0)),
                  pl.BlockSpec((D, N), lambda i: (0, 0))],
        out_specs=pl.BlockSpec((tm, N), lambda i: (i, 0)),
        out_shape=jax.ShapeDtypeStruct((T, N), F32),
        compiler_params=_params("parallel"),
        name="inproj",
    )(x2, gain, w_bf16)


def _rw_kernel(p_ref, mu_ref, w0_ref, a0_ref, wa2_ref, g2_ref, kk_ref, ka_ref, rk_ref, lnw_ref, lnb_ref,
               o_ref, prev_ref, zt_ref, at_s, bt_s, kt_s, rt_s, qh_s, gmat_s, v_s, ecl_s, y0_s, y_s, bonus_s, g_s, hmat_s,
               *, ts, nbatch, group):
    C = RW_CHUNK
    W = WIDTH
    i = pl.program_id(0)

    @pl.when(i == 0)
    def _():
        prev_ref[...] = jnp.zeros_like(prev_ref)
        zt_ref[...] = jnp.zeros_like(zt_ref)

    lane = _iota((1, LANES), 1)
    row = _iota((ts, 1), 0)
    r4 = _iota((W, W), 0)
    c4 = _iota((W, W), 1)
    same_head = (r4 >> HEAD_SHIFT) == (c4 >> HEAD_SHIFT)
    tri4 = jnp.where(same_head & (c4 <= r4), 1.0, 0.0).astype(BF16)
    ones_bd = _head_block_matrix(W, 1.0)

    for b in range(nbatch):
        p = p_ref[b]
        shifted = jnp.where(row == 0, prev_ref[b, 0:1, :], pltpu.roll(p, 1, 0))
        prev_ref[b, 0:1, :] = p_ref[b, ts - 1:ts, :]
        pm = p + (shifted - p) * mu_ref[...]
        r = pm[:, 0:W]
        k = pm[:, W:2 * W]
        v = pm[:, 2 * W:3 * W]
        lora = pm[:, 3 * W:3 * W + LANES]
        pg = pm[:, 3 * W + LANES:]
        wa = _mm(jnp.where(lane < RW_DECAY_LORA, jnp.tanh(lora), lora), wa2_ref[...])
        lw = -RW_DECAY_SCALE * _sigmoid(w0_ref[...] + wa[:, 0:W])
        a = _sigmoid(a0_ref[...] + wa[:, W:])
        g_s[b] = _mm(_sigmoid(pg), g2_ref[...])
        kk = k * kk_ref[...]
        kk = kk * jnp.minimum(lax.rsqrt(_nn_rhs_exact(kk * kk, ones_bd)), 1.0 / RW_MIN_NORM)
        kp = k * (1.0 + (a - 1.0) * ka_ref[...])
        bonus_s[b] = _nn_rhs_exact(r * kp * rk_ref[...], ones_bd) * v
        cl = jnp.concatenate([_nn_lhs_exact(tri4, lw[n * W:(n + 1) * W, :]) for n in range(ts // W)], axis=0)
        ecl = jnp.exp(cl)
        eml = jnp.exp(-cl)
        at_s[b] = (-kk * jnp.exp(cl - lw)).astype(BF16)
        bt_s[b] = (kk * a * eml).astype(BF16)
        kt_s[b] = (kp * eml).astype(BF16)
        rt_s[b] = (r * ecl).astype(BF16)
        v_s[b] = v
        ecl_s[b] = ecl

    head = _iota((1, W), 1) >> HEAD_SHIFT
    tw = _iota((C, W), 0)
    sw = _iota((C, W), 1) & (C - 1)
    strict = tw > sw
    incl = tw >= sw
    eye = jnp.where(tw == sw, 1.0, 0.0)

    def same_block(size):
        return (tw & -size) == (sw & -size)

    def stack4(x):
        return jnp.concatenate([jnp.where(head == h, x, jnp.zeros_like(x)) for h in range(N_HEADS)], axis=0)

    def block_diag(x):
        return jnp.concatenate([x.astype(BF16)] * N_HEADS, axis=0) * ones_bd

    chunks = [(b, slice(c * C, (c + 1) * C)) for c in range(ts // C) for b in range(nbatch)]
    for g0 in range(0, len(chunks), group):
        grp = chunks[g0:g0 + group]
        n = range(len(grp))
        at = [at_s[b, rows, :] for b, rows in grp]
        rt = [rt_s[b, rows, :] for b, rows in grp]
        b4 = [stack4(bt_s[b, rows, :]) for b, rows in grp]
        k4 = [stack4(kt_s[b, rows, :]) for b, rows in grp]
        v4 = [stack4(v_s[b, rows, :].astype(BF16)) for b, rows in grp]
        pq = [_nt(jnp.concatenate([at[j], rt[j]], axis=0), jnp.concatenate([b4[j], k4[j]], axis=0)) for j in n]
        lab = [jnp.where(strict, pq[j][:C, :W], 0.0) for j in n]
        lak = [jnp.where(strict, pq[j][:C, W:], 0.0) for j in n]
        mrb = [jnp.where(incl, pq[j][C:, :W], 0.0) for j in n]
        mrk = [jnp.where(incl, pq[j][C:, W:], 0.0) for j in n]
        m = [jnp.where(same_block(RW_BASE), lab[j], 0.0) for j in n]
        tinv = [eye + m[j] for j in n]
        for _ in range(2):
            m = [_nn(m[j].astype(BF16), block_diag(m[j])) for j in n]
            tinv = [tinv[j] + _nn(tinv[j].astype(BF16), block_diag(m[j])) for j in n]
        size = RW_BASE
        while size < C:
            below = same_block(2 * size) & jnp.logical_not(same_block(size))
            e_bd = [block_diag(jnp.where(below, lab[j], 0.0)) for j in n]
            half = [_nn(tinv[j].astype(BF16), e_bd[j]) for j in n]
            tinv = [tinv[j] + _nn(half[j].astype(BF16), block_diag(tinv[j])) for j in n]
            size *= 2
        tb = [tinv[j].astype(BF16) for j in n]
        mb = [mrb[j].astype(BF16) for j in n]
        w = [_nn(tb[j], stack4(at[j])) for j in n]
        u0 = [_nn(tb[j], stack4(_nn(lak[j].astype(BF16), v4[j])).astype(BF16)) for j in n]
        qh = [rt[j].astype(F32) + _nn(mb[j], stack4(w[j]).astype(BF16)) for j in n]
        y0 = [_nn(mb[j], stack4(u0[j]).astype(BF16)) + _nn(mrk[j].astype(BF16), v4[j]) for j in n]
        pad = jnp.zeros((C, W), F32)
        for j, (b, rows) in enumerate(grp):
            c = rows.start // C
            wt = jnp.concatenate([w[j], pad], axis=0).T.astype(BF16)
            uvt = jnp.concatenate([u0[j], v_s[b, rows, :]], axis=0).T.astype(BF16)
            bt = bt_s[b, rows, :]
            gmat = _nn(wt, jnp.concatenate([bt, jnp.zeros_like(bt)], axis=0))
            hmat = _nn(uvt, jnp.concatenate([bt, kt_s[b, rows, :]], axis=0))
            gmat_s[b, c] = jnp.where(same_head, gmat, 0.0).astype(BF16)
            hmat_s[b, c] = jnp.where(same_head, hmat, 0.0)
            qh_s[b, rows, :] = qh[j].astype(BF16)
            y0_s[b, rows, :] = y0[j]

    def chunk(c, carry):
        rows = pl.ds(pl.multiple_of(c * C, C), C)
        for b in range(nbatch):
            zt = zt_ref[b]
            ztb = zt.astype(BF16)
            y_s[b, rows, :] = _nt(qh_s[b, rows, :], ztb) + y0_s[b, rows, :]
            g_end = ecl_s[b, pl.ds(c * C + C - 1, 1), :]
            zt_ref[b] = (zt + _nn(ztb, gmat_s[b, c]) + hmat_s[b, c]) * g_end
        return carry

    lax.fori_loop(0, ts // C, chunk, 0)

    mean_bd = _head_block_matrix(W, 1.0 / HEAD_DIM)
    for b in range(nbatch):
        y = y_s[b]
        yc = y - _nn_rhs_exact(y, mean_bd)
        var = _nn_rhs_exact(yc * yc, mean_bd)
        y = yc * lax.rsqrt(var + RW_LN_EPS) * lnw_ref[...] + lnb_ref[...]
        o_ref[b] = (y + bonus_s[b]) * g_s[b]


def _rwkv(p, B, S, mu, w0, a0, wa2, g2, k_k, k_a, r_k, ln_w, ln_b, ts=512, group=8):
    ts = min(ts, S)
    W = WIDTH
    vec = lambda n: pl.BlockSpec((1, n), lambda i: (0, 0))
    full = lambda a: pl.BlockSpec(a.shape, lambda i: (0, 0))
    out = pl.pallas_call(
        functools.partial(_rw_kernel, ts=ts, nbatch=B, group=group),
        grid=(S // ts,),
        in_specs=[pl.BlockSpec((B, ts, 4 * W), lambda i: (0, i, COL_RW)),
                  vec(4 * W), vec(W), vec(W), full(wa2), full(g2), vec(W), vec(W), vec(W), vec(W), vec(W)],
        out_specs=pl.BlockSpec((B, ts, W), lambda i: (0, i, 0)),
        out_shape=jax.ShapeDtypeStruct((B, S, W), F32),
        scratch_shapes=[pltpu.VMEM((B, 8, 4 * W), F32), pltpu.VMEM((B, W, W), F32)]
                       + [pltpu.VMEM((B, ts, W), BF16)] * 5 + [pltpu.VMEM((B, ts // RW_CHUNK, W, W), BF16)]
                       + [pltpu.VMEM((B, ts, W), F32)] * 6 + [pltpu.VMEM((B, ts // RW_CHUNK, W, W), F32)],
        compiler_params=_params("arbitrary"),
        name="rwkv7",
    )(p.reshape(B, S, -1), mu, w0, a0, wa2, g2, k_k, k_a, r_k, ln_w, ln_b)
    return out.reshape(B * S, W)


def _sb_kernel(q_ref, k_ref, v_ref, qg_ref, kg_ref, o_ref, kb, vb, acc_o, acc_l, knorm,
               *, tq, tk, seq, prep, nsub):
    i = pl.program_id(2)
    lane = _iota((1, LANES), 1)
    mean_bd = _head_block_matrix(LANES, 1.0 / HEAD_DIM)
    ones = jnp.ones((LANES, LANES), BF16)

    @pl.when(i == 0)
    def _():
        knorm[...] = jnp.zeros_like(knorm)

        def body(c, carry):
            rows = pl.ds(pl.multiple_of(c * prep, prep), prep)
            kf = k_ref[rows, :]
            ms = _nn_rhs_exact(kf * kf, mean_bd)
            kn = (kf * lax.rsqrt(ms + NORM_EPS) * kg_ref[...]).astype(BF16)
            kb[rows, :] = kn
            vb[rows, :] = v_ref[rows, :].astype(BF16)
            kn = kn.astype(F32)
            sq = _nn_rhs_exact(kn * kn, ones)
            knorm[...] = jnp.maximum(knorm[...], jnp.max(sq.reshape(prep // 8, 8, LANES), axis=0))
            return carry
        lax.fori_loop(0, seq // prep, body, 0)

    subs = range(nsub)
    tile = [i * nsub + s for s in subs]
    q = q_ref[...]
    ms = _nn_rhs_exact(q * q, mean_bd)
    qn = q * lax.rsqrt(ms + NORM_EPS) * qg_ref[...] * (HEAD_DIM ** -0.5 * LOG2_E)
    q_lo = jnp.where(lane < HEAD_DIM, qn, 0.0).astype(BF16)
    q_hi = jnp.where(lane >= HEAD_DIM, qn, 0.0).astype(BF16)
    q2 = [jnp.concatenate([q_lo[s * tq:(s + 1) * tq], q_hi[s * tq:(s + 1) * tq]], axis=0) for s in subs]

    qf = (q_lo + q_hi).astype(F32)
    qsq = jnp.max(_nn_rhs_exact(qf * qf, ones))
    zmax = jnp.sqrt(qsq * jnp.max(knorm[...])) * SB_NORM_MARGIN
    stop_at = SB_ZERO_EXPONENT + zmax * BF16_ROUNDING

    def softplus2(z2):
        return jnp.maximum(z2, 0.0) + jnp.log2(1.0 + jnp.exp2(-jnp.abs(z2)))

    def rev_incl(n):
        return jnp.where(_iota((n, n), 0) >= _iota((n, n), 1), 1.0, 0.0).astype(BF16)

    prev = [pl.ds(pl.multiple_of(jnp.maximum(tile[s] - 1, 0) * tk, tk), tk) for s in subs]
    diag = [pl.ds(pl.multiple_of(tile[s] * tk, tk), tk) for s in subs]
    col = _iota((2 * tq, 2 * tk), 1)
    causal = (col - tk) < (_iota((2 * tq, 2 * tk), 0) & (tq - 1))
    valid = [causal & (col >= jnp.where(i > 0, 0, tk))] + [causal] * (nsub - 1)
    rev = rev_incl(tk)
    z2 = [_nt(q2[s], jnp.concatenate([kb[prev[s], :], kb[diag[s], :]], axis=0)) for s in subs]
    sp = [jnp.where(valid[s], softplus2(z2[s]), 0.0).astype(BF16) for s in subs]
    cum_diag = [_nn(sp[s][:, tk:], rev) for s in subs]
    cum = [jnp.concatenate([_nn(sp[s][:, :tk], rev) + cum_diag[s][:, 0:1], cum_diag[s]], axis=1) for s in subs]
    attn = [jnp.where(valid[s], jnp.exp2(z2[s] - cum[s]), 0.0).astype(BF16) for s in subs]
    for s in subs:
        acc_o[s] = _nn(attn[s], jnp.concatenate([vb[prev[s], :], vb[diag[s], :]], axis=0))
        acc_l[s] = jnp.broadcast_to(cum[s][:, 0:1], (2 * tq, LANES))

    for s in subs:
        def step(j, s=s):
            rows = pl.ds(pl.multiple_of(j * tk, tk), tk)
            z2 = _nt(q2[s], kb[rows, :])
            cum = _nn(softplus2(z2).astype(BF16), rev)
            seen = acc_l[s]
            attn = jnp.exp2(z2 - cum - jnp.concatenate([seen] * (tk // LANES), axis=1))
            acc_o[s] += _nn(attn.astype(BF16), vb[rows, :])
            seen = seen + cum[:, 0:1]
            acc_l[s] = seen
            return jnp.min(seen)

        def cond(carry, s=s):
            t, low = carry
            return (t < tile[s]) & (low < stop_at)

        def body(carry, s=s, step=step):
            t, _ = carry
            return t + 1, step(tile[s] - 1 - t)
        lax.while_loop(cond, body, (1, jnp.min(cum[s][:, 0:1])))

    for s in subs:
        o_ref[s * tq:(s + 1) * tq, :] = jnp.where(lane < HEAD_DIM, acc_o[s, 0:tq, :], acc_o[s, tq:2 * tq, :])


def _stick_breaking(p, B, S, qg, kg, tq=256, tk=256, nsub=2):
    assert tq == tk and S % (nsub * tq) == 0
    nq = S // (nsub * tq)
    T = B * S
    prep = min(512, S)
    return pl.pallas_call(
        functools.partial(_sb_kernel, tq=tq, tk=tk, seq=S, prep=prep, nsub=nsub),
        grid=(B, 2, nq),
        in_specs=[pl.BlockSpec((nsub * tq, LANES), lambda b, h, i: (b * nq + i, COL_SB_Q + h)),
                  pl.BlockSpec((S, LANES), lambda b, h, i: (b, COL_SB_K + h)),
                  pl.BlockSpec((S, LANES), lambda b, h, i: (b, COL_SB_V + h)),
                  pl.BlockSpec((1, LANES), lambda b, h, i: (0, 0)),
                  pl.BlockSpec((1, LANES), lambda b, h, i: (0, 0))],
        out_specs=pl.BlockSpec((nsub * tq, LANES), lambda b, h, i: (b * nq + i, h)),
        out_shape=jax.ShapeDtypeStruct((T, WIDTH), F32),
        scratch_shapes=[pltpu.VMEM((S, LANES), BF16), pltpu.VMEM((S, LANES), BF16),
                        pltpu.VMEM((nsub, 2 * tq, LANES), F32), pltpu.VMEM((nsub, 2 * tq, LANES), F32),
                        pltpu.VMEM((8, LANES), F32)],
        compiler_params=_params("arbitrary", "arbitrary", "arbitrary"),
        name="stick_breaking",
    )(p, p, p, qg, kg)


def _ret_kernel(q_ref, k_ref, v_ref, gt_ref, cos_ref, sin_ref, gn_ref, dm_ref, kdec_ref, qdec_ref, cdec_ref,
                o_ref, st_ref, *, ts):
    C = RET_CHUNK
    nch = ts // C
    pairs = range(2)
    chunks = range(nch)
    i = pl.program_id(1)

    @pl.when(i == 0)
    def _():
        st_ref[...] = jnp.zeros_like(st_ref)

    lane = _iota((1, LANES), 1)
    half = HEAD_DIM // 2
    first_half = (lane & (HEAD_DIM - 1)) < half
    same_head = (_iota((LANES, LANES), 0) >> HEAD_SHIFT) == (_iota((LANES, LANES), 1) >> HEAD_SHIFT)
    mean_bd = _head_block_matrix(LANES, 1.0 / HEAD_DIM)
    cos = cos_ref[...]
    sin = sin_ref[...]

    def rotary(t):
        swapped = jnp.where(first_half, pltpu.roll(t, LANES - half, 1), pltpu.roll(t, half, 1))
        return t * cos + swapped * sin

    def rows(t, c):
        return t[c * C:(c + 1) * C]

    cols = [slice(h * LANES, (h + 1) * LANES) for h in pairs]
    q = [rotary(q_ref[:, cols[h]]) for h in pairs]
    k = [rotary(k_ref[:, cols[h]]) * (HEAD_DIM ** -0.5) for h in pairs]
    vb = [v_ref[:, cols[h]].astype(BF16) for h in pairs]
    kb = [k[h].astype(BF16) for h in pairs]
    q_lo = [jnp.where(lane < HEAD_DIM, q[h], 0.0).astype(BF16) for h in pairs]
    q_hi = [jnp.where(lane >= HEAD_DIM, q[h], 0.0).astype(BF16) for h in pairs]
    qd = [(q[h] * jnp.concatenate([qdec_ref[h]] * nch, axis=0)).astype(BF16) for h in pairs]
    kd = [k[h] * jnp.concatenate([kdec_ref[h]] * nch, axis=0) for h in pairs]

    hc = [(h, c) for h in pairs for c in chunks]
    scores = {(h, c): _nt(jnp.concatenate([rows(q_lo[h], c), rows(q_hi[h], c)], axis=0), rows(kb[h], c)) * dm_ref[h]
              for h, c in hc}
    kv = {(h, c): _nn(rows(kd[h], c).T.astype(BF16), rows(vb[h], c)) for h, c in hc}
    intra2 = {(h, c): _nn(scores[h, c].astype(BF16), rows(vb[h], c)) for h, c in hc}
    state = {}
    for h in pairs:
        st = st_ref[h]
        for c in chunks:
            state[h, c] = st
            st = cdec_ref[h] * st + jnp.where(same_head, kv[h, c], 0.0)
        st_ref[h] = st
    inter = {(h, c): _nn(rows(qd[h], c), state[h, c].astype(BF16)) for h, c in hc}
    for h in pairs:
        o = jnp.concatenate([jnp.where(lane < HEAD_DIM, intra2[h, c][0:C], intra2[h, c][C:2 * C]) + inter[h, c]
                             for c in chunks], axis=0)
        oc = o - _nn_rhs_exact(o, mean_bd)
        var = _nn_rhs_exact(oc * oc, mean_bd)
        on = oc * lax.rsqrt(var + RET_GN_EPS) * gn_ref[:, cols[h]]
        gt = gt_ref[:, cols[h]]
        o_ref[:, cols[h]] = on * (gt * _sigmoid(gt))


def _retention_tables(S):
    C = RET_CHUNK
    inv_freq = ROPE_BASE ** (-jnp.arange(0, HEAD_DIM, 2, dtype=F32) / HEAD_DIM)
    ang = jnp.arange(S, dtype=F32)[:, None] * inv_freq[None, :]
    cos = jnp.tile(jnp.cos(ang), (1, 4))
    sin = jnp.tile(jnp.concatenate([-jnp.sin(ang), jnp.sin(ang)], axis=1), (1, 2))
    log_gamma = jnp.log(1.0 - 2.0 ** (-5.0 - jnp.arange(N_HEADS, dtype=F32)))
    idx = jnp.arange(C, dtype=F32)
    rel = idx[:, None] - idx[None, :]
    intra = jnp.where(rel >= 0, jnp.exp(jnp.maximum(rel, 0.0) * log_gamma[:, None, None]), 0.0)
    dm = intra.reshape(2, 2 * C, C)
    lanes = lambda t: jnp.repeat(t.reshape(2, 2, C).transpose(0, 2, 1), HEAD_DIM, axis=2)
    kdec = lanes(jnp.exp((C - 1 - idx)[None, :] * log_gamma[:, None]))
    qdec = lanes(jnp.exp((idx + 1.0)[None, :] * log_gamma[:, None]))
    cdec = jnp.repeat(jnp.exp(C * log_gamma).reshape(2, 1, 2), HEAD_DIM, axis=2)
    return cos, sin, dm, kdec, qdec, cdec


def _retention(p, B, S, gn, tables, ts=1024):
    ts = min(ts, S)
    nb = S // ts
    T = B * S
    C = RET_CHUNK
    cos, sin, dm, kdec, qdec, cdec = tables
    col = lambda c0: pl.BlockSpec((ts, WIDTH), lambda b, i: (b * nb + i, c0 // 2))
    tab = pl.BlockSpec((ts, LANES), lambda b, i: (i, 0))
    full = lambda a: pl.BlockSpec(a.shape, lambda b, i: (0,) * a.ndim)
    return pl.pallas_call(
        functools.partial(_ret_kernel, ts=ts),
        grid=(B, nb),
        in_specs=[col(COL_RET_Q), col(COL_RET_K), col(COL_RET_V), col(COL_RET_G), tab, tab,
                  full(gn), full(dm), full(kdec), full(qdec), full(cdec)],
        out_specs=pl.BlockSpec((ts, WIDTH), lambda b, i: (b * nb + i, 0)),
        out_shape=jax.ShapeDtypeStruct((T, WIDTH), F32),
        scratch_shapes=[pltpu.VMEM((2, LANES, LANES), F32)],
        compiler_params=_params("arbitrary", "arbitrary"),
        name="retention",
    )(p, p, p, p, cos, sin, gn, dm, kdec, qdec, cdec)


def _conv_kernel(ua_ref, ub_ref, dw_ref, db_ref, lnw_ref, lnb_ref, o_ref, buf, shifted, *, ts):
    i = pl.program_id(1)
    span = ts + CONV_HALO - SUBLANES

    @pl.when(i == 0)
    def _():
        buf[0:CONV_HALO, :] = jnp.zeros((CONV_HALO, WIDTH), F32)

    buf[CONV_HALO:, :] = ua_ref[...] * _sigmoid(ub_ref[...])
    for s in range(1, SUBLANES):
        shifted[s] = buf[s:s + span, :]
    acc = jnp.zeros((ts, WIDTH), F32) + db_ref[...]
    for j in range(CONV_WIDTH):
        start = CONV_HALO - (CONV_WIDTH - 1) + j
        s = start % SUBLANES
        window = buf[start:start + ts, :] if s == 0 else shifted[s, start - s:start - s + ts, :]
        acc = acc + dw_ref[j:j + 1, :] * window
    buf[0:CONV_HALO, :] = buf[ts:ts + CONV_HALO, :]
    xc = acc - jnp.mean(acc, axis=-1, keepdims=True)
    var = jnp.mean(xc * xc, axis=-1, keepdims=True)
    y = xc * lax.rsqrt(var + CONV_LN_EPS) * lnw_ref[...] + lnb_ref[...]
    o_ref[...] = y * _sigmoid(y)


def _conformer_conv(p, B, S, dw, db, ln_w, ln_b, ts=1024):
    ts = min(ts, S)
    nb = S // ts
    T = B * S
    vec = pl.BlockSpec((1, WIDTH), lambda b, i: (0, 0))
    return pl.pallas_call(
        functools.partial(_conv_kernel, ts=ts),
        grid=(B, nb),
        in_specs=[pl.BlockSpec((ts, WIDTH), lambda b, i: (b * nb + i, COL_CONV_A)),
                  pl.BlockSpec((ts, WIDTH), lambda b, i: (b * nb + i, COL_CONV_B)),
                  pl.BlockSpec((CONV_HALO, WIDTH), lambda b, i: (0, 0)), vec, vec, vec],
        out_specs=pl.BlockSpec((ts, WIDTH), lambda b, i: (b * nb + i, 0)),
        out_shape=jax.ShapeDtypeStruct((T, WIDTH), F32),
        scratch_shapes=[pltpu.VMEM((ts + CONV_HALO, WIDTH), F32),
                        pltpu.VMEM((SUBLANES, ts + CONV_HALO - SUBLANES, WIDTH), F32)],
        compiler_params=_params("arbitrary", "arbitrary"),
        name="conformer_conv",
    )(p, p, dw, db, ln_w, ln_b)


def _merge_kernel(x_ref, g_ref, y0_ref, y1_ref, y2_ref, y3_ref, wg_ref, wb_ref, wo_ref, o_ref):
    branch = [_nn(y_ref[...].astype(BF16), wb_ref[n]) for n, y_ref in enumerate((y0_ref, y1_ref, y2_ref, y3_ref))]
    x = x_ref[...]
    h = _rmsnorm_rows(x, g_ref[...]).astype(BF16)
    merged = None
    for n in range(len(branch)):
        term = _sigmoid(_nn(h, wg_ref[n])) * branch[n]
        merged = term if merged is None else merged + term
    o_ref[...] = x + _nn(merged.astype(BF16), wo_ref[...])


def _merge(x2, gain, ys, wg, wb, wo, tm=512):
    T, D = x2.shape
    tm = min(tm, T)
    row = lambda n: pl.BlockSpec((tm, n), lambda i: (i, 0))
    return pl.pallas_call(
        _merge_kernel,
        grid=(T // tm,),
        in_specs=[row(D), pl.BlockSpec((1, D), lambda i: (0, 0)), row(WIDTH), row(WIDTH), row(WIDTH), row(WIDTH),
                  pl.BlockSpec(wg.shape, lambda i: (0, 0, 0)),
                  pl.BlockSpec(wb.shape, lambda i: (0, 0, 0)),
                  pl.BlockSpec(wo.shape, lambda i: (0, 0))],
        out_specs=row(D),
        out_shape=jax.ShapeDtypeStruct((T, D), F32),
        compiler_params=_params("parallel"),
        name="merge",
    )(x2, gain, *ys, wg, wb, wo)


def _ffn_kernel(x_ref, g_ref, w1_ref, w3_ref, w2_ref, o_ref, h_s):
    f = pl.program_id(1)

    @pl.when(f == 0)
    def _():
        x = x_ref[...]
        h_s[...] = _rmsnorm_rows(x, g_ref[...]).astype(BF16)
        o_ref[...] = x

    h = h_s[...]
    a = _nn(h, w1_ref[...])
    b = _nn(h, w3_ref[...])
    o_ref[...] += _nn((a * _sigmoid(a) * b).astype(BF16), w2_ref[...])


def _ffn(x2, gain, w1, w3, w2, tm=1024, tf=1408):
    T, D = x2.shape
    tm = min(tm, T)
    nf = w1.shape[1] // tf
    return pl.pallas_call(
        _ffn_kernel,
        grid=(T // tm, nf),
        in_specs=[pl.BlockSpec((tm, D), lambda i, f: (i, 0)),
                  pl.BlockSpec((1, D), lambda i, f: (0, 0)),
                  pl.BlockSpec((D, tf), lambda i, f: (0, f)),
                  pl.BlockSpec((D, tf), lambda i, f: (0, f)),
                  pl.BlockSpec((tf, D), lambda i, f: (f, 0))],
        out_specs=pl.BlockSpec((tm, D), lambda i, f: (i, 0)),
        out_shape=jax.ShapeDtypeStruct((T, D), F32),
        scratch_shapes=[pltpu.VMEM((tm, D), BF16)],
        compiler_params=_params("parallel", "arbitrary"),
        name="ffn",
    )(x2, gain, w1, w3, w2)


def _moe_kernel(x_ref, g_ref, rt_ref, w1_ref, w3_ref, w2_ref, o_ref, h_s, gate_s, rank_s, rank_t_s, count_s,
                *, tm, cap, small):
    e = pl.program_id(1)
    lane = _iota((1, LANES), 1)

    @pl.when(e == 0)
    def _():
        x = x_ref[...]
        h = _rmsnorm_rows(x, g_ref[...])
        h_s[...] = h.astype(BF16)
        o_ref[...] = x
        logits = jnp.where(lane < N_EXPERTS, _nn_3pass(h, rt_ref[...]), -jnp.inf)
        lane_f = lane.astype(F32)
        m1 = jnp.max(logits, axis=-1, keepdims=True)
        i1 = jnp.min(jnp.where(logits == m1, lane_f, float(LANES)), axis=-1, keepdims=True)
        rest = jnp.where(lane_f == i1, -jnp.inf, logits)
        m2 = jnp.max(rest, axis=-1, keepdims=True)
        i2 = jnp.min(jnp.where(rest == m2, lane_f, float(LANES)), axis=-1, keepdims=True)
        e2 = jnp.exp(m2 - m1)
        den = 1.0 + e2
        gate_s[...] = jnp.where(lane_f == i1, 1.0 / den, 0.0) + jnp.where(lane_f == i2, e2 / den, 0.0)
        chosen = jnp.where((lane_f == i1) | (lane_f == i2), 1.0, 0.0)
        before = jnp.where(_iota((tm, tm), 1) < _iota((tm, tm), 0), 1.0, 0.0).astype(BF16)
        rank = jnp.where(chosen > 0.0, _nn(before, chosen.astype(BF16)), -1.0)
        rank_s[...] = rank
        rank_t_s[...] = rank.T
        count_s[...] = jnp.sum(chosen, axis=0, keepdims=True)

    mine = lane == e
    gate = jnp.sum(jnp.where(mine, gate_s[...], 0.0), axis=-1, keepdims=True)
    rank_col = jnp.sum(jnp.where(mine, rank_s[...], 0.0), axis=-1, keepdims=True)
    rank_row = rank_t_s[pl.ds(e, 1), :]
    count = jnp.sum(jnp.where(mine, count_s[...], 0.0))
    def one_pass(first, rows):
        lanes = -(-rows // LANES) * LANES
        slot_rows = _iota((rows, tm), 0).astype(F32)
        slot_cols = _iota((tm, lanes), 1).astype(F32)
        take = jnp.where(rank_row - first == slot_rows, 1.0, 0.0).astype(BF16)
        hit = rank_col - first == slot_cols
        if lanes > rows:
            hit = hit & (slot_cols < rows)
        put = jnp.where(hit, 1.0, 0.0).astype(BF16)
        xg = _nn(take, h_s[...]).astype(BF16)
        a = _nn(xg, w1_ref[0])
        b = _nn(xg, w3_ref[0])
        y = _nn((a * _sigmoid(a) * b).astype(BF16), w2_ref[0]).astype(BF16)
        if lanes > rows:
            y = jnp.concatenate([y, jnp.zeros((lanes - rows, y.shape[1]), BF16)], axis=0)
        o_ref[...] += gate * _nn(put, y)

    lower = 0.0
    for size in small:
        @pl.when((count > lower) & (count <= size))
        def _(size=size):
            one_pass(jnp.float32(0.0), size)
        lower = float(size)

    @pl.when(count > lower)
    def _():
        def more(first):
            one_pass(first, cap)
            return first + cap
        lax.while_loop(lambda first: first < count, more, jnp.float32(0.0))


def _moe(x2, gain, router_pad, w1, w3, w2, tm=1024, cap=320, small=(256, 288)):
    T, D = x2.shape
    tm = min(tm, T)
    cap = min(cap, tm)
    small = tuple(s for s in small if s < cap)
    E, _, F = w1.shape
    return pl.pallas_call(
        functools.partial(_moe_kernel, tm=tm, cap=cap, small=small),
        grid=(T // tm, E),
        in_specs=[pl.BlockSpec((tm, D), lambda i, e: (i, 0)),
                  pl.BlockSpec((1, D), lambda i, e: (0, 0)),
                  pl.BlockSpec((D, LANES), lambda i, e: (0, 0)),
                  pl.BlockSpec((1, D, F), lambda i, e: (e, 0, 0)),
                  pl.BlockSpec((1, D, F), lambda i, e: (e, 0, 0)),
                  pl.BlockSpec((1, F, D), lambda i, e: (e, 0, 0))],
        out_specs=pl.BlockSpec((tm, D), lambda i, e: (i, 0)),
        out_shape=jax.ShapeDtypeStruct((T, D), F32),
        scratch_shapes=[pltpu.VMEM((tm, D), BF16), pltpu.VMEM((tm, LANES), F32), pltpu.VMEM((tm, LANES), F32),
                        pltpu.VMEM((LANES, tm), F32), pltpu.VMEM((1, LANES), F32)],
        compiler_params=_params("parallel", "arbitrary"),
        name="moe",
    )(x2, gain, router_pad, w1, w3, w2)


def kernel(x, norm_mix, w_in, rw_mu, rw_w0, rw_w2, rw_a0, rw_a2, rw_g2, rw_k_k, rw_k_a, rw_r_k, rw_ln_w, rw_ln_b, sb_q_norm, sb_k_norm, ret_gn, conv_dw, conv_b, conv_ln_w, conv_ln_b, w_gate, w_branch, w_out, norm_ffn, ffn_w1, ffn_w3, ffn_w2, router, moe_w1, moe_w3, moe_w2):
    B, S, D = x.shape
    depth = norm_mix.shape[0]
    x2 = x.reshape(B * S, D)
    tables = _retention_tables(S)
    row = lambda t: t.reshape(1, -1)
    for l in range(depth):
        p = _inproj(x2, row(norm_mix[l]), w_in[l].astype(BF16))
        zeros = jnp.zeros_like(rw_w2[l])
        wa2 = jnp.concatenate([jnp.concatenate([rw_w2[l], zeros], axis=1),
                               jnp.concatenate([zeros, rw_a2[l]], axis=1)], axis=0).astype(BF16)
        y_rw = _rwkv(p, B, S, row(rw_mu[l]), row(rw_w0[l]), row(rw_a0[l]), wa2, rw_g2[l].astype(BF16),
                     row(rw_k_k[l]), row(rw_k_a[l]), row(rw_r_k[l]), row(rw_ln_w[l]), row(rw_ln_b[l]))
        y_sb = _stick_breaking(p, B, S, row(jnp.tile(sb_q_norm[l], 2)), row(jnp.tile(sb_k_norm[l], 2)))
        y_ret = _retention(p, B, S, row(ret_gn[l]), tables)
        dw = jnp.concatenate([conv_dw[l], jnp.zeros((CONV_HALO - CONV_WIDTH, WIDTH), F32)], axis=0)
        y_conv = _conformer_conv(p, B, S, dw, row(conv_b[l]), row(conv_ln_w[l]), row(conv_ln_b[l]))
        x2 = _merge(x2, row(norm_mix[l]), (y_rw, y_sb, y_ret, y_conv),
                    w_gate[l].astype(BF16), w_branch[l].astype(BF16), w_out[l].astype(BF16))
        if l % 2 == 0:
            x2 = _ffn(x2, row(norm_ffn[l]), ffn_w1[l // 2].astype(BF16), ffn_w3[l // 2].astype(BF16),
                      ffn_w2[l // 2].astype(BF16))
        else:
            rt = jnp.concatenate([router[l // 2], jnp.zeros((D, LANES - N_EXPERTS), F32)], axis=1)
            x2 = _moe(x2, row(norm_ffn[l]), rt, moe_w1[l // 2].astype(BF16), moe_w3[l // 2].astype(BF16),
                      moe_w2[l // 2].astype(BF16))
    return x2.reshape(B, S, D)
```
